```python
import math
import jax, jax.numpy as jnp
from jax import lax
import numpy as np

D_MODEL = 1024
BATCH = 4
SEQ = 4096
DEPTH = 4

N_MIXERS = 4
MIXER_WIDTH = D_MODEL // 4
D_MIX = N_MIXERS * MIXER_WIDTH
SGU_HEADS = 4
SGU_CHUNK = 128
S5_GROUP = 16
S5_GROUPS = MIXER_WIDTH // S5_GROUP
S5_STATE = 64
S5_DT_MIN = 1e-3
S5_DT_MAX = 1e-1
LRU_HEADS = 4
LRU_CONV = 4
LRU_C = 8.0
FOX_HEADS = 4
FOX_HEAD_DIM = MIXER_WIDTH // FOX_HEADS
ATTN_BLOCK = 128
D_FF = 4 * D_MODEL
RMS_EPS = 1e-6
D_IN_PROJ = 8 * MIXER_WIDTH + FOX_HEADS

kernel_name = "hybrid_parallel_heads_sgu_s5_rglru_fox"


def rms_norm(x, g):
    xf = x.astype(jnp.float32)
    y = xf * lax.rsqrt(jnp.mean(jnp.square(xf), axis=-1, keepdims=True) + RMS_EPS)
    return (y * g.astype(jnp.float32)).astype(x.dtype)


def group_rms_norm(y, g):
    bsz, seq, _ = y.shape
    yf = y.astype(jnp.float32).reshape(bsz, seq, N_MIXERS, MIXER_WIDTH)
    yf = yf * lax.rsqrt(jnp.mean(jnp.square(yf), axis=-1, keepdims=True) + RMS_EPS)
    return (yf.reshape(bsz, seq, D_MIX) * g.astype(jnp.float32)).astype(y.dtype)


def _linear_combine(e1, e2):
    a1, b1 = e1
    a2, b2 = e2
    return a1 * a2, a2 * b1 + b2


def _complex_linear_combine(e1, e2):
    a1r, a1i, b1r, b1i = e1
    a2r, a2i, b2r, b2i = e2
    ar = a2r * a1r - a2i * a1i
    ai = a2r * a1i + a2i * a1r
    br = a2r * b1r - a2i * b1i + b2r
    bi = a2r * b1i + a2i * b1r + b2i
    return ar, ai, br, bi


def sgu_mixer(u, v, norm_g, w_s, b_s):
    bsz, seq, _ = u.shape
    u = jax.nn.gelu(u)
    v = rms_norm(jax.nn.gelu(v), norm_g)
    mask = jnp.tril(jnp.ones((SGU_CHUNK, SGU_CHUNK), w_s.dtype))
    vh = v.reshape(bsz, seq // SGU_CHUNK, SGU_CHUNK, SGU_HEADS, MIXER_WIDTH // SGU_HEADS)
    mixed = jnp.einsum('hts,bcshd->bcthd', w_s * mask, vh) + b_s.T[None, None, :, :, None]
    return u * mixed.reshape(bsz, seq, MIXER_WIDTH)


def s5_mixer(u, lam_re, lam_im, log_dt, b_re, b_im, c_re, c_im, d, glu_w, glu_b):
    bsz, seq, _ = u.shape
    f32 = jnp.float32
    uf = u.astype(f32)
    lam_re = lam_re.astype(f32)
    lam_im = lam_im.astype(f32)
    dt = jnp.exp(log_dt.astype(f32))[:, None]
    mag = jnp.exp(lam_re * dt)
    abar_re = mag * jnp.cos(lam_im * dt)
    abar_im = mag * jnp.sin(lam_im * dt)
    denom = jnp.square(lam_re) + jnp.square(lam_im)
    num_re = abar_re - 1.0
    num_im = abar_im
    fac_re = (num_re * lam_re + num_im * lam_im) / denom
    fac_im = (num_im * lam_re - num_re * lam_im) / denom
    b_re = b_re.astype(f32)
    b_im = b_im.astype(f32)
    bbar_re = fac_re[..., None] * b_re - fac_im[..., None] * b_im
    bbar_im = fac_re[..., None] * b_im + fac_im[..., None] * b_re
    ug = uf.reshape(bsz, seq, S5_GROUPS, S5_GROUP)
    bu_re = jnp.einsum('blgh,gph->blgp', ug, bbar_re)
    bu_im = jnp.einsum('blgh,gph->blgp', ug, bbar_im)
    a_re = jnp.broadcast_to(abar_re, bu_re.shape)
    a_im = jnp.broadcast_to(abar_im, bu_im.shape)
    _, _, s_re, s_im = lax.associative_scan(
        _complex_linear_combine, (a_re, a_im, bu_re, bu_im), axis=1)
    y = (jnp.einsum('blgp,ghp->blgh', s_re, c_re.astype(f32))
         - jnp.einsum('blgp,ghp->blgh', s_im, c_im.astype(f32)))
    y = y.reshape(bsz, seq, MIXER_WIDTH) + d.astype(f32) * uf
    y = jax.nn.gelu(y)
    y = y * jax.nn.sigmoid(y @ glu_w.astype(f32) + glu_b.astype(f32))
    return y.astype(u.dtype)


def causal_depthwise_conv(x, w, b):
    out = lax.conv_general_dilated(
        x, w[:, None, :], window_strides=(1,), padding=[(LRU_CONV - 1, 0)],
        dimension_numbers=('NWC', 'WIO', 'NWC'), feature_group_count=x.shape[-1])
    return out + b


def rglru_mixer(x, gate, conv_w, conv_b, wa, ba, wx, bx, lam):
    bsz, seq, _ = x.shape
    f32 = jnp.float32
    xc = causal_depthwise_conv(x, conv_w, conv_b)
    xh = xc.reshape(bsz, seq, LRU_HEADS, MIXER_WIDTH // LRU_HEADS)
    r = jax.nn.sigmoid(jnp.einsum('blhi,hij->blhj', xh, wa) + ba).reshape(bsz, seq, MIXER_WIDTH)
    i = jax.nn.sigmoid(jnp.einsum('blhi,hij->blhj', xh, wx) + bx).reshape(bsz, seq, MIXER_WIDTH)
    log_a = -LRU_C * r.astype(f32) * jax.nn.softplus(-lam.astype(f32))
    a = jnp.exp(log_a)
    b = jnp.sqrt(-jnp.expm1(2.0 * log_a)) * (i * xc).astype(f32)
    _, h = lax.associative_scan(_linear_combine, (a, b), axis=1)
    return h.astype(x.dtype) * jax.nn.gelu(gate)


def forgetting_attention(q, k, v, log_f):
    bsz, seq, n_heads, hd = q.shape
    c = jnp.cumsum(log_f, axis=1).transpose(0, 2, 1)
    scale = hd ** -0.5
    neg = jnp.finfo(jnp.float32).min
    out_blocks = []
    for blk in range(seq // ATTN_BLOCK):
        q0, q1 = blk * ATTN_BLOCK, (blk + 1) * ATTN_BLOCK
        s = jnp.einsum('bqhd,bkhd->bhqk', q[:, q0:q1], k[:, :q1],
                       preferred_element_type=jnp.float32) * scale
        s = s + c[:, :, q0:q1, None] - c[:, :, None, :q1]
        q_pos = jnp.arange(q0, q1)[:, None]
        k_pos = jnp.arange(q1)[None, :]
        s = jnp.where(k_pos <= q_pos, s, neg)
        p = jax.nn.softmax(s, axis=-1).astype(v.dtype)
        out_blocks.append(jnp.einsum('bhqk,bkhd->bqhd', p, v[:, :q1]))
    return jnp.concatenate(out_blocks, axis=1)


def fox_mixer(q, k, v, f_logit, b_f):
    bsz, seq, _ = q.shape
    shp = (bsz, seq, FOX_HEADS, FOX_HEAD_DIM)
    log_f = jax.nn.log_sigmoid((f_logit + b_f).astype(jnp.float32))
    o = forgetting_attention(q.reshape(shp), k.reshape(shp), v.reshape(shp), log_f)
    return o.reshape(bsz, seq, MIXER_WIDTH)


def setup_inputs(seed: int = 0) -> dict:
    key = jax.random.key(seed)
    ks = iter(jax.random.split(key, 40))
    f32 = jnp.float32

    def nrm(shape, scale):
        return jax.random.normal(next(ks), shape, f32) * scale

    def gain(shape):
        return 1.0 + 0.02 * jax.random.normal(next(ks), shape, f32)

    L = DEPTH
    W = MIXER_WIDTH
    hd_lru = W // LRU_HEADS
    x = jax.random.normal(next(ks), (BATCH, SEQ, D_MODEL), f32)
    norm1_g = gain((L, D_MODEL))
    w_in = nrm((L, D_MODEL, D_IN_PROJ), D_MODEL ** -0.5)
    sgu_norm_g = gain((L, W))
    sgu_w = nrm((L, SGU_HEADS, SGU_CHUNK, SGU_CHUNK), SGU_CHUNK ** -0.5)
    sgu_b = 1.0 + 0.1 * jax.random.normal(next(ks), (L, SGU_HEADS, SGU_CHUNK), f32)
    s5_lambda_re = -0.5 + 0.01 * jax.random.normal(next(ks), (L, S5_GROUPS, S5_STATE), f32)
    s5_lambda_im = (jnp.pi * jnp.arange(S5_STATE, dtype=f32))[None, None, :] \
        + 0.01 * jax.random.normal(next(ks), (L, S5_GROUPS, S5_STATE), f32)
    s5_log_dt = jax.random.uniform(next(ks), (L, S5_GROUPS), f32,
                                   minval=math.log(S5_DT_MIN), maxval=math.log(S5_DT_MAX))
    s5_b_re = nrm((L, S5_GROUPS, S5_STATE, S5_GROUP), (2.0 * S5_GROUP) ** -0.5)
    s5_b_im = nrm((L, S5_GROUPS, S5_STATE, S5_GROUP), (2.0 * S5_GROUP) ** -0.5)
    s5_c_re = nrm((L, S5_GROUPS, S5_GROUP, S5_STATE), (2.0 * S5_STATE) ** -0.5)
    s5_c_im = nrm((L, S5_GROUPS, S5_GROUP, S5_STATE), (2.0 * S5_STATE) ** -0.5)
    s5_d = nrm((L, W), 0.5)
    s5_glu_w = nrm((L, W, W), W ** -0.5)
    s5_glu_b = nrm((L, W), 0.01)
    lru_conv_w = nrm((L, LRU_CONV, W), LRU_CONV ** -0.5)
    lru_conv_b = nrm((L, W), 0.01)
    lru_wa = nrm((L, LRU_HEADS, hd_lru, hd_lru), hd_lru ** -0.5)
    lru_ba = nrm((L, LRU_HEADS, hd_lru), 0.01)
    lru_wx = nrm((L, LRU_HEADS, hd_lru, hd_lru), hd_lru ** -0.5)
    lru_bx = nrm((L, LRU_HEADS, hd_lru), 0.01)
    a_pow = jax.random.uniform(next(ks), (L, W), f32, minval=0.9, maxval=0.999)
    sig = a_pow ** (1.0 / LRU_C)
    lru_lambda = jnp.log(sig) - jnp.log1p(-sig)
    fox_fgate_b = 2.0 + 0.1 * jax.random.normal(next(ks), (L, FOX_HEADS), f32)
    mix_norm_g = gain((L, D_MIX))
    w_out = nrm((L, D_MIX, D_MODEL), D_MIX ** -0.5)
    norm2_g = gain((L, D_MODEL))
    w_mlp_in = nrm((L, D_MODEL, D_FF), D_MODEL ** -0.5)
    w_mlp_out = nrm((L, D_FF, D_MODEL), D_FF ** -0.5)
    final_g = gain((D_MODEL,))
    return {
        "x": x, "norm1_g": norm1_g, "w_in": w_in,
        "sgu_norm_g": sgu_norm_g, "sgu_w": sgu_w, "sgu_b": sgu_b,
        "s5_lambda_re": s5_lambda_re, "s5_lambda_im": s5_lambda_im, "s5_log_dt": s5_log_dt,
        "s5_b_re": s5_b_re, "s5_b_im": s5_b_im, "s5_c_re": s5_c_re, "s5_c_im": s5_c_im,
        "s5_d": s5_d, "s5_glu_w": s5_glu_w, "s5_glu_b": s5_glu_b,
        "lru_conv_w": lru_conv_w, "lru_conv_b": lru_conv_b, "lru_wa": lru_wa, "lru_ba": lru_ba,
        "lru_wx": lru_wx, "lru_bx": lru_bx, "lru_lambda": lru_lambda,
        "fox_fgate_b": fox_fgate_b,
        "mix_norm_g": mix_norm_g, "w_out": w_out, "norm2_g": norm2_g,
        "w_mlp_in": w_mlp_in, "w_mlp_out": w_mlp_out, "final_g": final_g,
    }


def reference(x, norm1_g, w_in, sgu_norm_g, sgu_w, sgu_b,
              s5_lambda_re, s5_lambda_im, s5_log_dt, s5_b_re, s5_b_im, s5_c_re, s5_c_im,
              s5_d, s5_glu_w, s5_glu_b,
              lru_conv_w, lru_conv_b, lru_wa, lru_ba, lru_wx, lru_bx, lru_lambda,
              fox_fgate_b, mix_norm_g, w_out, norm2_g, w_mlp_in, w_mlp_out, final_g):
    split_points = [MIXER_WIDTH * i for i in range(1, 9)]
    for l in range(DEPTH):
        h = rms_norm(x, norm1_g[l])
        z = h @ w_in[l]
        a_u, a_v, b_in, c_x, c_gate, d_q, d_k, d_v, d_f = jnp.split(z, split_points, axis=-1)
        y_a = sgu_mixer(a_u, a_v, sgu_norm_g[l], sgu_w[l], sgu_b[l])
        y_b = s5_mixer(b_in, s5_lambda_re[l], s5_lambda_im[l], s5_log_dt[l],
                       s5_b_re[l], s5_b_im[l], s5_c_re[l], s5_c_im[l],
                       s5_d[l], s5_glu_w[l], s5_glu_b[l])
        y_c = rglru_mixer(c_x, c_gate, lru_conv_w[l], lru_conv_b[l],
                          lru_wa[l], lru_ba[l], lru_wx[l], lru_bx[l], lru_lambda[l])
        y_d = fox_mixer(d_q, d_k, d_v, d_f, fox_fgate_b[l])
        y = jnp.concatenate([y_a, y_b, y_c, y_d], axis=-1)
        y = group_rms_norm(y, mix_norm_g[l])
        x = x + y @ w_out[l]
        h = rms_norm(x, norm2_g[l])
        x = x + jnp.square(jax.nn.relu(h @ w_mlp_in[l])) @ w_mlp_out[l]
    return rms_norm(x, final_g)
```

```python
import functools

import jax
import jax.numpy as jnp
from jax import lax
from jax.experimental import pallas as pl
from jax.experimental.pallas import tpu as pltpu

D_MODEL = 1024
BATCH = 4
SEQ = 4096
DEPTH = 4
N_TOK = BATCH * SEQ
MIXER_WIDTH = 256
SGU_HEADS = 4
SGU_CHUNK = 128
S5_GROUP = 16
S5_GROUPS = 16
S5_STATE = 64
LRU_HEADS = 4
LRU_CONV = 4
LRU_C = 8.0
FOX_HEADS = 4
FOX_HEAD_DIM = 64
D_FF = 4 * D_MODEL
RMS_EPS = 1e-6

LANES = 128
SUBLANES = 8
HEAD_PAD = LANES
ZMIX_COLS = 5 * MIXER_WIDTH
QKV_COLS = 3 * FOX_HEADS * HEAD_PAD
S5_NSTATE = S5_GROUPS * S5_STATE
S5_ROWS = 2 * S5_NSTATE // LANES

TM_IN = 512
TM_MERGE = 512
TM_MLP = 1024
TF_MLP = 512
T_SGU = 512
T_SCAN = 128
T_CUM = 512
TQ = 512
TK = 512
VMEM_LIMIT = 48 * 1024 * 1024

F32 = jnp.float32
BF16 = jnp.bfloat16


def _params(n_axes):
    return pltpu.CompilerParams(dimension_semantics=("arbitrary",) * n_axes,
                                vmem_limit_bytes=VMEM_LIMIT)


def _rms(x, g, width):
    ms = jnp.sum(jnp.square(x), axis=-1, keepdims=True) * (1.0 / width)
    return x * lax.rsqrt(ms + RMS_EPS) * g


def _softplus(x):
    return jnp.maximum(x, 0.0) + jnp.log1p(jnp.exp(-jnp.abs(x)))


def _in_proj_kernel(x_ref, g_ref, wmix_ref, wqkv_ref, wf_ref, zmix_ref, qkv_ref, zf_ref):
    h = _rms(x_ref[...], g_ref[...], D_MODEL).astype(BF16)
    zmix_ref[...] = jnp.dot(h, wmix_ref[...], preferred_element_type=F32)
    qkv_ref[...] = jnp.dot(h, wqkv_ref[...], preferred_element_type=F32).astype(BF16)
    zf_ref[...] = jnp.dot(h, wf_ref[...], preferred_element_type=F32)


def _in_proj(x, g, wmix, wqkv, wf):
    row = lambda i: (i, 0)
    full = lambda i: (0, 0)
    return pl.pallas_call(
        _in_proj_kernel,
        grid=(N_TOK // TM_IN,),
        in_specs=[pl.BlockSpec((TM_IN, D_MODEL), row),
                  pl.BlockSpec((1, D_MODEL), full),
                  pl.BlockSpec((D_MODEL, ZMIX_COLS), full),
                  pl.BlockSpec((D_MODEL, QKV_COLS), full),
                  pl.BlockSpec((D_MODEL, LANES), full)],
        out_specs=[pl.BlockSpec((TM_IN, ZMIX_COLS), row),
                   pl.BlockSpec((TM_IN, QKV_COLS), row),
                   pl.BlockSpec((TM_IN, LANES), row)],
        out_shape=[jax.ShapeDtypeStruct((N_TOK, ZMIX_COLS), F32),
                   jax.ShapeDtypeStruct((N_TOK, QKV_COLS), BF16),
                   jax.ShapeDtypeStruct((N_TOK, LANES), F32)],
        compiler_params=_params(1),
        name="in_proj",
    )(x, g, wmix, wqkv, wf)


def _split3(x):
    hi = x.astype(BF16)
    r1 = x - hi.astype(F32)
    mid = r1.astype(BF16)
    lo = (r1 - mid.astype(F32)).astype(BF16)
    return hi, mid, lo


def _fcum_kernel(zf_ref, bf_ref, ccol_ref, crow_ref, carry_ref):
    @pl.when(pl.program_id(1) == 0)
    def _():
        carry_ref[...] = jnp.zeros_like(carry_ref)

    logit = zf_ref[...] + bf_ref[...]
    log_f = -_softplus(-logit)
    r = lax.broadcasted_iota(jnp.int32, (T_CUM, T_CUM), 0)
    c = lax.broadcasted_iota(jnp.int32, (T_CUM, T_CUM), 1)
    tril = jnp.where(c <= r, 1.0, 0.0).astype(BF16)
    hi, mid, lo = _split3(log_f)
    cs = (jnp.dot(tril, hi, preferred_element_type=F32)
          + jnp.dot(tril, mid, preferred_element_type=F32)
          + jnp.dot(tril, lo, preferred_element_type=F32))
    cum = cs + carry_ref[...]
    carry_ref[...] = cum[T_CUM - 1:T_CUM, :]
    cum_t = cum.T
    for h in range(FOX_HEADS):
        ccol_ref[:, h * LANES:(h + 1) * LANES] = jnp.broadcast_to(cum[:, h:h + 1], (T_CUM, LANES))
        crow_ref[0, h * SUBLANES:(h + 1) * SUBLANES, :] = jnp.broadcast_to(
            cum_t[h:h + 1, :], (SUBLANES, T_CUM))


def _fcum(zf, bf):
    n_c = SEQ // T_CUM
    return pl.pallas_call(
        _fcum_kernel,
        grid=(BATCH, n_c),
        in_specs=[pl.BlockSpec((T_CUM, LANES), lambda b, c: (b * n_c + c, 0)),
                  pl.BlockSpec((1, LANES), lambda b, c: (0, 0))],
        out_specs=[pl.BlockSpec((T_CUM, FOX_HEADS * LANES), lambda b, c: (b * n_c + c, 0)),
                   pl.BlockSpec((1, FOX_HEADS * SUBLANES, T_CUM), lambda b, c: (b, 0, c))],
        out_shape=[jax.ShapeDtypeStruct((N_TOK, FOX_HEADS * LANES), F32),
                   jax.ShapeDtypeStruct((BATCH, FOX_HEADS * SUBLANES, SEQ), F32)],
        scratch_shapes=[pltpu.VMEM((1, LANES), F32)],
        compiler_params=_params(2),
        name="forget_cumsum",
    )(zf, bf)


def _sgu_kernel(zu_ref, zv_ref, g_ref, w_ref, b_ref, o_ref):
    u = jax.nn.gelu(zu_ref[...])
    v = _rms(jax.nn.gelu(zv_ref[...]), g_ref[...], MIXER_WIDTH)
    hd = MIXER_WIDTH // SGU_HEADS
    lane_head = lax.broadcasted_iota(jnp.int32, (SGU_CHUNK, MIXER_WIDTH), 1) // hd
    t_idx = lax.broadcasted_iota(jnp.int32, (SGU_CHUNK, SGU_HEADS * SGU_CHUNK), 0)
    s_idx = lax.broadcasted_iota(jnp.int32, (SGU_CHUNK, SGU_HEADS * SGU_CHUNK), 1) % SGU_CHUNK
    wm = jnp.where(s_idx <= t_idx, w_ref[...], 0.0).astype(BF16)
    bias = b_ref[...]
    for c in range(T_SGU // SGU_CHUNK):
        rows = slice(c * SGU_CHUNK, (c + 1) * SGU_CHUNK)
        vc = v[rows]
        vstack = jnp.concatenate(
            [jnp.where(lane_head == h, vc, 0.0) for h in range(SGU_HEADS)], axis=0).astype(BF16)
        mixed = jnp.dot(wm, vstack, preferred_element_type=F32) + bias
        o_ref[rows, :] = u[rows] * mixed


def _sgu(zmix, g, w, b):
    return pl.pallas_call(
        _sgu_kernel,
        grid=(N_TOK // T_SGU,),
        in_specs=[pl.BlockSpec((T_SGU, MIXER_WIDTH), lambda i: (i, 0)),
                  pl.BlockSpec((T_SGU, MIXER_WIDTH), lambda i: (i, 1)),
                  pl.BlockSpec((1, MIXER_WIDTH), lambda i: (0, 0)),
                  pl.BlockSpec((SGU_CHUNK, SGU_HEADS * SGU_CHUNK), lambda i: (0, 0)),
                  pl.BlockSpec((SGU_CHUNK, MIXER_WIDTH), lambda i: (0, 0))],
        out_specs=pl.BlockSpec((T_SGU, MIXER_WIDTH), lambda i: (i, 0)),
        out_shape=jax.ShapeDtypeStruct((N_TOK, MIXER_WIDTH), F32),
        compiler_params=_params(1),
        name="sgu",
    )(zmix, zmix, g, w, b)


def _s5_kernel(u_ref, bbd_ref, lam_ref, cbd_ref, d_ref, gw_ref, gb_ref, o_ref, s_ref, h_ref):
    @pl.when(pl.program_id(0) == 0)
    def _():
        h_ref[...] = jnp.zeros_like(h_ref)

    for b in range(BATCH):
        bu = jnp.dot(u_ref[b].astype(BF16), bbd_ref[...], preferred_element_type=F32)
        for j in range(S5_ROWS):
            s_ref[b, pl.ds(j, T_SCAN, stride=S5_ROWS), :] = bu[:, j * LANES:(j + 1) * LANES]

    lam_re = lam_ref[0:SUBLANES, :]
    lam_im = lam_ref[SUBLANES:S5_ROWS, :]

    def step(t, carry):
        base = pl.multiple_of(t * S5_ROWS, S5_ROWS)
        new = []
        for b in range(BATCH):
            h_re, h_im = carry[2 * b], carry[2 * b + 1]
            n_re = lam_re * h_re - lam_im * h_im + s_ref[b, pl.ds(base, SUBLANES), :]
            n_im = lam_re * h_im + lam_im * h_re + s_ref[b, pl.ds(base + SUBLANES, SUBLANES), :]
            s_ref[b, pl.ds(base, SUBLANES), :] = n_re
            s_ref[b, pl.ds(base + SUBLANES, SUBLANES), :] = n_im
            new += [n_re, n_im]
        return tuple(new)

    init = []
    for b in range(BATCH):
        init += [h_ref[b, 0:SUBLANES, :], h_ref[b, SUBLANES:S5_ROWS, :]]
    fin = lax.fori_loop(0, T_SCAN, step, tuple(init), unroll=8)
    for b in range(BATCH):
        h_ref[b, 0:SUBLANES, :] = fin[2 * b]
        h_ref[b, SUBLANES:S5_ROWS, :] = fin[2 * b + 1]

    for b in range(BATCH):
        states = jnp.concatenate(
            [s_ref[b, pl.ds(j, T_SCAN, stride=S5_ROWS), :] for j in range(S5_ROWS)], axis=1)
        u = u_ref[b]
        y = jnp.dot(states.astype(BF16), cbd_ref[...], preferred_element_type=F32) + d_ref[...] * u
        y = jax.nn.gelu(y)
        gate = jnp.dot(y.astype(BF16), gw_ref[...], preferred_element_type=F32) + gb_ref[...]
        o_ref[b] = y * jax.nn.sigmoid(gate)


def _s5(zmix3, bbd, lam, cbd, d, gw, gb):
    full2 = lambda c: (0, 0)
    return pl.pallas_call(
        _s5_kernel,
        grid=(SEQ // T_SCAN,),
        in_specs=[pl.BlockSpec((BATCH, T_SCAN, MIXER_WIDTH), lambda c: (0, c, 4)),
                  pl.BlockSpec((MIXER_WIDTH, 2 * S5_NSTATE), full2),
                  pl.BlockSpec((S5_ROWS, LANES), full2),
                  pl.BlockSpec((2 * S5_NSTATE, MIXER_WIDTH), full2),
                  pl.BlockSpec((1, MIXER_WIDTH), full2),
                  pl.BlockSpec((MIXER_WIDTH, MIXER_WIDTH), full2),
                  pl.BlockSpec((1, MIXER_WIDTH), full2)],
        out_specs=pl.BlockSpec((BATCH, T_SCAN, MIXER_WIDTH), lambda c: (0, c, 0)),
        out_shape=jax.ShapeDtypeStruct((BATCH, SEQ, MIXER_WIDTH), F32),
        scratch_shapes=[pltpu.VMEM((BATCH, T_SCAN * S5_ROWS, LANES), F32),
                        pltpu.VMEM((BATCH, S5_ROWS, LANES), F32)],
        compiler_params=_params(1),
        name="s5",
    )(zmix3, bbd, lam, cbd, d, gw, gb)


LRU_SLOTS = BATCH * MIXER_WIDTH // LANES


def _lru_kernel(z_ref, cw_ref, cb_ref, wa_ref, ba_ref, wx_ref, bx_ref, lam_ref, o_ref,
                tail_ref, a_ref, b_ref, h_ref):
    @pl.when(pl.program_id(0) == 0)
    def _():
        tail_ref[...] = jnp.zeros_like(tail_ref)
        h_ref[...] = jnp.zeros_like(h_ref)

    n_tiles = MIXER_WIDTH // LANES
    decay_rate = LRU_C * _softplus(-lam_ref[...])
    for b in range(BATCH):
        x = z_ref[b, :, 0:MIXER_WIDTH]
        xp = jnp.concatenate([tail_ref[b], x], axis=0)
        tail_ref[b] = x[T_SCAN - SUBLANES:T_SCAN, :]
        xc = cb_ref[...]
        for k in range(LRU_CONV):
            off = SUBLANES - (LRU_CONV - 1) + k
            xc = xc + cw_ref[k:k + 1, :] * xp[off:off + T_SCAN, :]
        xcb = xc.astype(BF16)
        r = jax.nn.sigmoid(jnp.dot(xcb, wa_ref[...], preferred_element_type=F32) + ba_ref[...])
        i = jax.nn.sigmoid(jnp.dot(xcb, wx_ref[...], preferred_element_type=F32) + bx_ref[...])
        log_a = -(r * decay_rate)
        a = jnp.exp(log_a)
        inp = jnp.sqrt(-jnp.tanh(log_a) * (a * a + 1.0)) * (i * xc)
        for j in range(n_tiles):
            slot = b * n_tiles + j
            a_ref[pl.ds(slot, T_SCAN, stride=LRU_SLOTS), :] = a[:, j * LANES:(j + 1) * LANES]
            b_ref[pl.ds(slot, T_SCAN, stride=LRU_SLOTS), :] = inp[:, j * LANES:(j + 1) * LANES]

    def step(t, h):
        base = pl.multiple_of(t * LRU_SLOTS, LRU_SLOTS)
        h = a_ref[pl.ds(base, LRU_SLOTS), :] * h + b_ref[pl.ds(base, LRU_SLOTS), :]
        b_ref[pl.ds(base, LRU_SLOTS), :] = h
        return h

    h_ref[...] = lax.fori_loop(0, T_SCAN, step, h_ref[...], unroll=8)

    for b in range(BATCH):
        h = jnp.concatenate(
            [b_ref[pl.ds(b * n_tiles + j, T_SCAN, stride=LRU_SLOTS), :] for j in range(n_tiles)],
            axis=1)
        o_ref[b] = h * jax.nn.gelu(z_ref[b, :, MIXER_WIDTH:2 * MIXER_WIDTH])


def _lru(zmix3, cw, cb, wa, ba, wx, bx, lam):
    full2 = lambda c: (0, 0)
    vec = pl.BlockSpec((1, MIXER_WIDTH), full2)
    mat = pl.BlockSpec((MIXER_WIDTH, MIXER_WIDTH), full2)
    return pl.pallas_call(
        _lru_kernel,
        grid=(SEQ // T_SCAN,),
        in_specs=[pl.BlockSpec((BATCH, T_SCAN, 2 * MIXER_WIDTH), lambda c: (0, c, 1)),
                  pl.BlockSpec((LRU_CONV, MIXER_WIDTH), full2), vec, mat, vec, mat, vec, vec],
        out_specs=pl.BlockSpec((BATCH, T_SCAN, MIXER_WIDTH), lambda c: (0, c, 0)),
        out_shape=jax.ShapeDtypeStruct((BATCH, SEQ, MIXER_WIDTH), F32),
        scratch_shapes=[pltpu.VMEM((BATCH, SUBLANES, MIXER_WIDTH), F32),
                        pltpu.VMEM((T_SCAN * LRU_SLOTS, LANES), F32),
                        pltpu.VMEM((T_SCAN * LRU_SLOTS, LANES), F32),
                        pltpu.VMEM((LRU_SLOTS, LANES), F32)],
        compiler_params=_params(1),
        name="rglru",
    )(zmix3, cw, cb, wa, ba, wx, bx, lam)


def _attn_kernel(q_ref, k_ref, v_ref, ccol_ref, crow_ref, o_ref):
    i = pl.program_id(2)
    q = q_ref[...]
    cq = jnp.concatenate([ccol_ref[...]] * (TK // LANES), axis=1)
    scale = FOX_HEAD_DIM ** -0.5
    neg = jnp.finfo(F32).min

    def block(j, carry, masked):
        m, l, acc = carry
        start = pl.multiple_of(j * TK, TK)
        ks = k_ref[pl.ds(start, TK), :]
        vs = v_ref[pl.ds(start, TK), :]
        s = lax.dot_general(q, ks, (((1,), (1,)), ((), ())), preferred_element_type=F32) * scale
        ck = jnp.concatenate([crow_ref[0, :, pl.ds(start, TK)]] * (TQ // SUBLANES), axis=0)
        s = s + cq - ck
        if masked:
            rr = lax.broadcasted_iota(jnp.int32, (TQ, TK), 0)
            cc = lax.broadcasted_iota(jnp.int32, (TQ, TK), 1)
            s = jnp.where(cc <= rr, s, neg)
        m_new = jnp.maximum(m, jnp.max(s, axis=1, keepdims=True))
        alpha = jnp.exp(m - m_new)
        p = jnp.exp(s - m_new)
        l = alpha * l + jnp.sum(p, axis=1, keepdims=True)
        acc = alpha * acc + jnp.dot(p.astype(BF16), vs, preferred_element_type=F32)
        return m_new, l, acc

    init = (jnp.full((TQ, 1), neg, F32), jnp.zeros((TQ, 1), F32), jnp.zeros((TQ, HEAD_PAD), F32))
    carry = lax.fori_loop(0, i, lambda j, c: block(j, c, False), init)
    _, l, acc = block(i, carry, True)
    o_ref[...] = acc / l


def _attn(qkv, ccol, crow):
    n_q = SEQ // TQ
    qrow = lambda b, h, i: (b * n_q + i, h)
    return pl.pallas_call(
        _attn_kernel,
        grid=(BATCH, FOX_HEADS, n_q),
        in_specs=[pl.BlockSpec((TQ, HEAD_PAD), qrow),
                  pl.BlockSpec((SEQ, HEAD_PAD), lambda b, h, i: (b, FOX_HEADS + h)),
                  pl.BlockSpec((SEQ, HEAD_PAD), lambda b, h, i: (b, 2 * FOX_HEADS + h)),
                  pl.BlockSpec((TQ, LANES), qrow),
                  pl.BlockSpec((1, SUBLANES, SEQ), lambda b, h, i: (b, h, 0))],
        out_specs=pl.BlockSpec((TQ, HEAD_PAD), qrow),
        out_shape=jax.ShapeDtypeStruct((N_TOK, FOX_HEADS * HEAD_PAD), F32),
        compiler_params=_params(3),
        name="fox_attention",
    )(qkv, qkv, qkv, ccol, crow)


def _merge_kernel(ya_ref, yb_ref, yc_ref, yd_ref, x_ref, g_ref, gd_ref, w_ref, o_ref):
    w = MIXER_WIDTH
    parts = [_rms(ya_ref[...], g_ref[:, 0:w], w),
             _rms(yb_ref[...], g_ref[:, w:2 * w], w),
             _rms(yc_ref[...], g_ref[:, 2 * w:3 * w], w),
             _rms(yd_ref[...], gd_ref[...], w)]
    y = jnp.concatenate(parts, axis=1).astype(BF16)
    o_ref[...] = x_ref[...] + jnp.dot(y, w_ref[...], preferred_element_type=F32)


def _merge(ya, yb, yc, yd, x, g, gd, w):
    row = lambda i: (i, 0)
    full = lambda i: (0, 0)
    k_dim = 3 * MIXER_WIDTH + FOX_HEADS * HEAD_PAD
    mix = pl.BlockSpec((TM_MERGE, MIXER_WIDTH), row)
    return pl.pallas_call(
        _merge_kernel,
        grid=(N_TOK // TM_MERGE,),
        in_specs=[mix, mix, mix,
                  pl.BlockSpec((TM_MERGE, FOX_HEADS * HEAD_PAD), row),
                  pl.BlockSpec((TM_MERGE, D_MODEL), row),
                  pl.BlockSpec((1, 3 * MIXER_WIDTH), full),
                  pl.BlockSpec((1, FOX_HEADS * HEAD_PAD), full),
                  pl.BlockSpec((k_dim, D_MODEL), full)],
        out_specs=pl.BlockSpec((TM_MERGE, D_MODEL), row),
        out_shape=jax.ShapeDtypeStruct((N_TOK, D_MODEL), F32),
        compiler_params=_params(1),
        name="merge_out_proj",
    )(ya, yb, yc, yd, x, g, gd, w)


def _mlp_kernel(x_ref, g_ref, w1_ref, w2_ref, fg_ref, o_ref, h_ref, acc_ref, *, final_norm):
    j = pl.program_id(1)

    @pl.when(j == 0)
    def _():
        h_ref[...] = _rms(x_ref[...], g_ref[...], D_MODEL).astype(BF16)
        acc_ref[...] = jnp.zeros_like(acc_ref)

    a = jnp.dot(h_ref[...], w1_ref[...], preferred_element_type=F32)
    a = jnp.square(jnp.maximum(a, 0.0)).astype(BF16)
    acc_ref[...] += jnp.dot(a, w2_ref[...], preferred_element_type=F32)

    @pl.when(j == pl.num_programs(1) - 1)
    def _():
        out = x_ref[...] + acc_ref[...]
        if final_norm:
            out = _rms(out, fg_ref[...], D_MODEL)
        o_ref[...] = out


def _mlp(x, g, w1, w2, fg, final_norm):
    return pl.pallas_call(
        functools.partial(_mlp_kernel, final_norm=final_norm),
        grid=(N_TOK // TM_MLP, D_FF // TF_MLP),
        in_specs=[pl.BlockSpec((TM_MLP, D_MODEL), lambda i, j: (i, 0)),
                  pl.BlockSpec((1, D_MODEL), lambda i, j: (0, 0)),
                  pl.BlockSpec((D_MODEL, TF_MLP), lambda i, j: (0, j)),
                  pl.BlockSpec((TF_MLP, D_MODEL), lambda i, j: (j, 0)),
                  pl.BlockSpec((1, D_MODEL), lambda i, j: (0, 0))],
        out_specs=pl.BlockSpec((TM_MLP, D_MODEL), lambda i, j: (i, 0)),
        out_shape=jax.ShapeDtypeStruct((N_TOK, D_MODEL), F32),
        scratch_shapes=[pltpu.VMEM((TM_MLP, D_MODEL), BF16),
                        pltpu.VMEM((TM_MLP, D_MODEL), F32)],
        compiler_params=_params(2),
        name="mlp",
    )(x, g, w1, w2, fg)


def _pad_heads(w):
    rows = w.shape[0]
    w = w.reshape(rows, FOX_HEADS, FOX_HEAD_DIM)
    w = jnp.pad(w, ((0, 0), (0, 0), (0, HEAD_PAD - FOX_HEAD_DIM)))
    return w.reshape(rows, FOX_HEADS * HEAD_PAD)


def _block_diag(blocks):
    g, r, c = blocks.shape
    eye = jnp.eye(g, dtype=blocks.dtype)
    return jnp.einsum('grc,gh->grhc', blocks, eye).reshape(g * r, g * c)


def _s5_discretize(lam_re, lam_im, log_dt, b_re, b_im):
    dt = jnp.exp(log_dt)[:, None]
    mag = jnp.exp(lam_re * dt)
    abar_re = mag * jnp.cos(lam_im * dt)
    abar_im = mag * jnp.sin(lam_im * dt)
    denom = jnp.square(lam_re) + jnp.square(lam_im)
    num_re = abar_re - 1.0
    num_im = abar_im
    fac_re = (num_re * lam_re + num_im * lam_im) / denom
    fac_im = (num_im * lam_re - num_re * lam_im) / denom
    bbar_re = fac_re[..., None] * b_re - fac_im[..., None] * b_im
    bbar_im = fac_re[..., None] * b_im + fac_im[..., None] * b_re
    return abar_re, abar_im, bbar_re, bbar_im


def kernel(x, norm1_g, w_in, sgu_norm_g, sgu_w, sgu_b, s5_lambda_re, s5_lambda_im, s5_log_dt, s5_b_re, s5_b_im, s5_c_re, s5_c_im, s5_d, s5_glu_w, s5_glu_b, lru_conv_w, lru_conv_b, lru_wa, lru_ba, lru_wx, lru_bx, lru_lambda, fox_fgate_b, mix_norm_g, w_out, norm2_g, w_mlp_in, w_mlp_out, final_g):
    w = MIXER_WIDTH
    xf = x.reshape(N_TOK, D_MODEL)
    row = lambda v: v.reshape(1, -1)
    for l in range(DEPTH):
        wl = w_in[l]
        wmix = jnp.concatenate([wl[:, 0:2 * w], wl[:, 3 * w:5 * w], wl[:, 2 * w:3 * w]], axis=1).astype(BF16)
        wqkv = jnp.concatenate([_pad_heads(wl[:, (5 + m) * w:(6 + m) * w]) for m in range(3)], axis=1).astype(BF16)
        wf = jnp.pad(wl[:, 8 * w:], ((0, 0), (0, LANES - FOX_HEADS))).astype(BF16)
        zmix, qkv, zf = _in_proj(xf, row(norm1_g[l]), wmix, wqkv, wf)
        zmix3 = zmix.reshape(BATCH, SEQ, ZMIX_COLS)

        bf = jnp.pad(fox_fgate_b[l], (0, LANES - FOX_HEADS)).reshape(1, LANES)
        ccol, crow = _fcum(zf, bf)

        sgu_wcat = jnp.transpose(sgu_w[l], (1, 0, 2)).reshape(SGU_CHUNK, SGU_HEADS * SGU_CHUNK)
        sgu_bias = jnp.repeat(sgu_b[l].T, w // SGU_HEADS, axis=1)
        y_a = _sgu(zmix, row(sgu_norm_g[l]), sgu_wcat, sgu_bias)

        abar_re, abar_im, bbar_re, bbar_im = _s5_discretize(
            s5_lambda_re[l], s5_lambda_im[l], s5_log_dt[l], s5_b_re[l], s5_b_im[l])
        bbd = jnp.concatenate([_block_diag(jnp.transpose(bbar_re, (0, 2, 1))),
                               _block_diag(jnp.transpose(bbar_im, (0, 2, 1)))], axis=1).astype(BF16)
        cbd = jnp.concatenate([_block_diag(jnp.transpose(s5_c_re[l], (0, 2, 1))),
                               -_block_diag(jnp.transpose(s5_c_im[l], (0, 2, 1)))], axis=0).astype(BF16)
        lam = jnp.concatenate([abar_re.reshape(SUBLANES, LANES), abar_im.reshape(SUBLANES, LANES)], axis=0)
        y_b = _s5(zmix3, bbd, lam, cbd, row(s5_d[l]), s5_glu_w[l].astype(BF16), row(s5_glu_b[l]))

        y_c = _lru(zmix3, lru_conv_w[l], row(lru_conv_b[l]),
                   _block_diag(lru_wa[l]).astype(BF16), row(lru_ba[l]),
                   _block_diag(lru_wx[l]).astype(BF16), row(lru_bx[l]), row(lru_lambda[l]))

        y_d = _attn(qkv, ccol, crow)

        g_mix = mix_norm_g[l]
        w_o = w_out[l]
        w_o_pad = jnp.concatenate(
            [w_o[0:3 * w],
             jnp.pad(w_o[3 * w:].reshape(FOX_HEADS, FOX_HEAD_DIM, D_MODEL),
                     ((0, 0), (0, HEAD_PAD - FOX_HEAD_DIM), (0, 0))).reshape(FOX_HEADS * HEAD_PAD, D_MODEL)],
            axis=0).astype(BF16)
        xf = _merge(y_a, y_b.reshape(N_TOK, w), y_c.reshape(N_TOK, w), y_d, xf,
                    row(g_mix[0:3 * w]), _pad_heads(row(g_mix[3 * w:])), w_o_pad)

        xf = _mlp(xf, row(norm2_g[l]), w_mlp_in[l].astype(BF16), w_mlp_out[l].astype(BF16),
                  row(final_g), final_norm=(l == DEPTH - 1))
    return xf.reshape(BATCH, SEQ, D_MODEL)
```

```python
import functools

import jax
import jax.numpy as jnp
import numpy as np
from jax import lax
from jax.experimental import pallas as pl
from jax.experimental.pallas import tpu as pltpu

D_MODEL = 1024
BATCH = 4
SEQ = 4096
DEPTH = 4
N_TOK = BATCH * SEQ
MIXER_WIDTH = 256
SGU_HEADS = 4
SGU_CHUNK = 128
S5_GROUP = 16
S5_GROUPS = 16
S5_STATE = 64
LRU_HEADS = 4
LRU_CONV = 4
LRU_C = 8.0
FOX_HEADS = 4
FOX_HEAD_DIM = 64
D_FF = 4 * D_MODEL
RMS_EPS = 1e-6

LANES = 128
SUBLANES = 8
HEAD_PAD = LANES
ZMIX_COLS = 5 * MIXER_WIDTH
QKV_COLS = 3 * FOX_HEADS * HEAD_PAD
S5_NSTATE = S5_GROUPS * S5_STATE
S5_ROWS = 2 * S5_NSTATE // LANES
S5_PITCH = S5_ROWS + SUBLANES

TM_IN = 512
TM_MERGE = 512
TM_MLP = 1024
TF_MLP = 1024
T_SGU = 512
T_SCAN = 128
T_CUM = 512
TQ = 512
TK = 512
ATTN_HEADS_PER_STEP = 4
VMEM_LIMIT = 48 * 1024 * 1024

F32 = jnp.float32
BF16 = jnp.bfloat16


def _params(n_axes):
    return pltpu.CompilerParams(dimension_semantics=("arbitrary",) * n_axes,
                                vmem_limit_bytes=VMEM_LIMIT)


def _rms(x, g, width):
    ms = jnp.sum(jnp.square(x), axis=-1, keepdims=True) * (1.0 / width)
    return x * lax.rsqrt(ms + RMS_EPS) * g


def _softplus(x):
    return jnp.maximum(x, 0.0) + jnp.log1p(jnp.exp(-jnp.abs(x)))


def _in_proj_kernel(x_ref, g_ref, wmix_ref, wqkv_ref, wf_ref, zmix_ref, qkv_ref, zf_ref):
    h = _rms(x_ref[...], g_ref[...], D_MODEL).astype(BF16)
    zmix_ref[...] = jnp.dot(h, wmix_ref[...], preferred_element_type=F32)
    qkv_ref[...] = jnp.dot(h, wqkv_ref[...], preferred_element_type=F32).astype(BF16)
    zf_ref[...] = jnp.dot(h, wf_ref[...], preferred_element_type=F32)


def _in_proj(x, g, wmix, wqkv, wf):
    row = lambda i: (i, 0)
    full = lambda i: (0, 0)
    return pl.pallas_call(
        _in_proj_kernel,
        grid=(N_TOK // TM_IN,),
        in_specs=[pl.BlockSpec((TM_IN, D_MODEL), row),
                  pl.BlockSpec((1, D_MODEL), full),
                  pl.BlockSpec((D_MODEL, ZMIX_COLS), full),
                  pl.BlockSpec((D_MODEL, QKV_COLS), full),
                  pl.BlockSpec((D_MODEL, LANES), full)],
        out_specs=[pl.BlockSpec((TM_IN, ZMIX_COLS), row),
                   pl.BlockSpec((TM_IN, QKV_COLS), row),
                   pl.BlockSpec((TM_IN, LANES), row)],
        out_shape=[jax.ShapeDtypeStruct((N_TOK, ZMIX_COLS), F32),
                   jax.ShapeDtypeStruct((N_TOK, QKV_COLS), BF16),
                   jax.ShapeDtypeStruct((N_TOK, LANES), F32)],
        compiler_params=_params(1),
        name="in_proj",
    )(x, g, wmix, wqkv, wf)


def _split3(x):
    hi = x.astype(BF16)
    r1 = x - hi.astype(F32)
    mid = r1.astype(BF16)
    lo = (r1 - mid.astype(F32)).astype(BF16)
    return hi, mid, lo


def _bias_placement():
    pq = np.zeros((3 * LANES, FOX_HEADS * HEAD_PAD), np.float32)
    pk = np.zeros_like(pq)
    ones_q = np.zeros((1, FOX_HEADS * HEAD_PAD), np.float32)
    ones_k = np.zeros_like(ones_q)
    for h in range(FOX_HEADS):
        for piece in range(3):
            pq[piece * LANES + h, h * HEAD_PAD + FOX_HEAD_DIM + piece] = 1.0
            pk[piece * LANES + h, h * HEAD_PAD + FOX_HEAD_DIM + 3 + piece] = -1.0
            ones_q[0, h * HEAD_PAD + FOX_HEAD_DIM + 3 + piece] = 1.0
            ones_k[0, h * HEAD_PAD + FOX_HEAD_DIM + piece] = 1.0
    return (jnp.asarray(pq, BF16), jnp.asarray(pk, BF16), jnp.asarray(ones_q), jnp.asarray(ones_k))


def _fcum_kernel(zf_ref, bf_ref, pq_ref, pk_ref, oq_ref, ok_ref, qadd_ref, kadd_ref, carry_ref):
    @pl.when(pl.program_id(1) == 0)
    def _():
        carry_ref[...] = jnp.zeros_like(carry_ref)

    logit = zf_ref[...] + bf_ref[...]
    log_f = -_softplus(-logit)
    r = lax.broadcasted_iota(jnp.int32, (T_CUM, T_CUM), 0)
    c = lax.broadcasted_iota(jnp.int32, (T_CUM, T_CUM), 1)
    tril = jnp.where(c <= r, 1.0, 0.0).astype(BF16)
    hi, mid, lo = _split3(log_f)
    cs = (jnp.dot(tril, hi, preferred_element_type=F32)
          + jnp.dot(tril, mid, preferred_element_type=F32)
          + jnp.dot(tril, lo, preferred_element_type=F32))
    cum = cs + carry_ref[...]
    carry_ref[...] = cum[T_CUM - 1:T_CUM, :]
    pieces = jnp.concatenate(_split3(cum), axis=1)
    qadd_ref[...] = (jnp.dot(pieces, pq_ref[...], preferred_element_type=F32) + oq_ref[...]).astype(BF16)
    kadd_ref[...] = (jnp.dot(pieces, pk_ref[...], preferred_element_type=F32) + ok_ref[...]).astype(BF16)


def _fcum(zf, bf):
    n_c = SEQ // T_CUM
    pq, pk, ones_q, ones_k = _bias_placement()
    full = lambda b, c: (0, 0)
    wide = FOX_HEADS * HEAD_PAD
    return pl.pallas_call(
        _fcum_kernel,
        grid=(BATCH, n_c),
        in_specs=[pl.BlockSpec((T_CUM, LANES), lambda b, c: (b * n_c + c, 0)),
                  pl.BlockSpec((1, LANES), full),
                  pl.BlockSpec((3 * LANES, wide), full),
                  pl.BlockSpec((3 * LANES, wide), full),
                  pl.BlockSpec((1, wide), full),
                  pl.BlockSpec((1, wide), full)],
        out_specs=[pl.BlockSpec((T_CUM, wide), lambda b, c: (b * n_c + c, 0)),
                   pl.BlockSpec((T_CUM, wide), lambda b, c: (b * n_c + c, 0))],
        out_shape=[jax.ShapeDtypeStruct((N_TOK, wide), BF16),
                   jax.ShapeDtypeStruct((N_TOK, wide), BF16)],
        scratch_shapes=[pltpu.VMEM((1, LANES), F32)],
        compiler_params=_params(2),
        name="forget_cumsum",
    )(zf, bf, pq, pk, ones_q, ones_k)


def _sgu_kernel(zu_ref, zv_ref, g_ref, w_ref, b_ref, o_ref):
    u = jax.nn.gelu(zu_ref[...])
    v = _rms(jax.nn.gelu(zv_ref[...]), g_ref[...], MIXER_WIDTH)
    hd = MIXER_WIDTH // SGU_HEADS
    lane_head = lax.broadcasted_iota(jnp.int32, (SGU_CHUNK, MIXER_WIDTH), 1) // hd
    t_idx = lax.broadcasted_iota(jnp.int32, (SGU_CHUNK, SGU_HEADS * SGU_CHUNK), 0)
    s_idx = lax.broadcasted_iota(jnp.int32, (SGU_CHUNK, SGU_HEADS * SGU_CHUNK), 1) % SGU_CHUNK
    wm = jnp.where(s_idx <= t_idx, w_ref[...], 0.0).astype(BF16)
    bias = b_ref[...]
    for c in range(T_SGU // SGU_CHUNK):
        rows = slice(c * SGU_CHUNK, (c + 1) * SGU_CHUNK)
        vc = v[rows]
        vstack = jnp.concatenate(
            [jnp.where(lane_head == h, vc, 0.0) for h in range(SGU_HEADS)], axis=0).astype(BF16)
        mixed = jnp.dot(wm, vstack, preferred_element_type=F32) + bias
        o_ref[rows, :] = u[rows] * mixed


def _sgu(zmix, g, w, b):
    return pl.pallas_call(
        _sgu_kernel,
        grid=(N_TOK // T_SGU,),
        in_specs=[pl.BlockSpec((T_SGU, MIXER_WIDTH), lambda i: (i, 0)),
                  pl.BlockSpec((T_SGU, MIXER_WIDTH), lambda i: (i, 1)),
                  pl.BlockSpec((1, MIXER_WIDTH), lambda i: (0, 0)),
                  pl.BlockSpec((SGU_CHUNK, SGU_HEADS * SGU_CHUNK), lambda i: (0, 0)),
                  pl.BlockSpec((SGU_CHUNK, MIXER_WIDTH), lambda i: (0, 0))],
        out_specs=pl.BlockSpec((T_SGU, MIXER_WIDTH), lambda i: (i, 0)),
        out_shape=jax.ShapeDtypeStruct((N_TOK, MIXER_WIDTH), F32),
        compiler_params=_params(1),
        name="sgu",
    )(zmix, zmix, g, w, b)


def _s5_kernel(u_ref, bbd_ref, lam_ref, cbd_ref, d_ref, gw_ref, gb_ref, o_ref, s_ref, h_ref):
    @pl.when(pl.program_id(0) == 0)
    def _():
        h_ref[...] = jnp.zeros_like(h_ref)

    for b in range(BATCH):
        bu = jnp.dot(u_ref[b].astype(BF16), bbd_ref[...], preferred_element_type=F32)
        for j in range(S5_ROWS):
            s_ref[b, pl.ds(j, T_SCAN, stride=S5_PITCH), :] = bu[:, j * LANES:(j + 1) * LANES]

    lam_re = lam_ref[0:SUBLANES, :]
    lam_im = lam_ref[SUBLANES:S5_ROWS, :]

    def step(t, carry):
        base = pl.multiple_of(t * S5_PITCH, SUBLANES)
        new = []
        for b in range(BATCH):
            h_re, h_im = carry[2 * b], carry[2 * b + 1]
            n_re = lam_re * h_re - lam_im * h_im + s_ref[b, pl.ds(base, SUBLANES), :]
            n_im = lam_re * h_im + lam_im * h_re + s_ref[b, pl.ds(base + SUBLANES, SUBLANES), :]
            s_ref[b, pl.ds(base, SUBLANES), :] = n_re
            s_ref[b, pl.ds(base + SUBLANES, SUBLANES), :] = n_im
            new += [n_re, n_im]
        return tuple(new)

    init = []
    for b in range(BATCH):
        init += [h_ref[b, 0:SUBLANES, :], h_ref[b, SUBLANES:S5_ROWS, :]]
    fin = lax.fori_loop(0, T_SCAN, step, tuple(init), unroll=8)
    for b in range(BATCH):
        h_ref[b, 0:SUBLANES, :] = fin[2 * b]
        h_ref[b, SUBLANES:S5_ROWS, :] = fin[2 * b + 1]

    for b in range(BATCH):
        states = jnp.concatenate(
            [s_ref[b, pl.ds(j, T_SCAN, stride=S5_PITCH), :] for j in range(S5_ROWS)], axis=1)
        u = u_ref[b]
        y = jnp.dot(states.astype(BF16), cbd_ref[...], preferred_element_type=F32) + d_ref[...] * u
        y = jax.nn.gelu(y)
        gate = jnp.dot(y.astype(BF16), gw_ref[...], preferred_element_type=F32) + gb_ref[...]
        o_ref[b] = y * jax.nn.sigmoid(gate)


def _s5(zmix3, bbd, lam, cbd, d, gw, gb):
    full2 = lambda c: (0, 0)
    return pl.pallas_call(
        _s5_kernel,
        grid=(SEQ // T_SCAN,),
        in_specs=[pl.BlockSpec((BATCH, T_SCAN, MIXER_WIDTH), lambda c: (0, c, 4)),
                  pl.BlockSpec((MIXER_WIDTH, 2 * S5_NSTATE), full2),
                  pl.BlockSpec((S5_ROWS, LANES), full2),
                  pl.BlockSpec((2 * S5_NSTATE, MIXER_WIDTH), full2),
                  pl.BlockSpec((1, MIXER_WIDTH), full2),
                  pl.BlockSpec((MIXER_WIDTH, MIXER_WIDTH), full2),
                  pl.BlockSpec((1, MIXER_WIDTH), full2)],
        out_specs=pl.BlockSpec((BATCH, T_SCAN, MIXER_WIDTH), lambda c: (0, c, 0)),
        out_shape=jax.ShapeDtypeStruct((BATCH, SEQ, MIXER_WIDTH), F32),
        scratch_shapes=[pltpu.VMEM((BATCH, T_SCAN * S5_PITCH, LANES), F32),
                        pltpu.VMEM((BATCH, S5_ROWS, LANES), F32)],
        compiler_params=_params(1),
        name="s5",
    )(zmix3, bbd, lam, cbd, d, gw, gb)


LRU_SLOTS = BATCH * MIXER_WIDTH // LANES


def _lru_kernel(z_ref, cw_ref, cb_ref, wa_ref, ba_ref, wx_ref, bx_ref, lam_ref, o_ref,
                tail_ref, a_ref, b_ref, h_ref):
    @pl.when(pl.program_id(0) == 0)
    def _():
        tail_ref[...] = jnp.zeros_like(tail_ref)
        h_ref[...] = jnp.zeros_like(h_ref)

    n_tiles = MIXER_WIDTH // LANES
    decay_rate = LRU_C * _softplus(-lam_ref[...])
    for b in range(BATCH):
        x = z_ref[b, :, 0:MIXER_WIDTH]
        xp = jnp.concatenate([tail_ref[b], x], axis=0)
        tail_ref[b] = x[T_SCAN - SUBLANES:T_SCAN, :]
        xc = cb_ref[...]
        for k in range(LRU_CONV):
            off = SUBLANES - (LRU_CONV - 1) + k
            xc = xc + cw_ref[k:k + 1, :] * xp[off:off + T_SCAN, :]
        xcb = xc.astype(BF16)
        r = jax.nn.sigmoid(jnp.dot(xcb, wa_ref[...], preferred_element_type=F32) + ba_ref[...])
        i = jax.nn.sigmoid(jnp.dot(xcb, wx_ref[...], preferred_element_type=F32) + bx_ref[...])
        log_a = -(r * decay_rate)
        a = jnp.exp(log_a)
        inp = jnp.sqrt(-jnp.tanh(log_a) * (a * a + 1.0)) * (i * xc)
        for j in range(n_tiles):
            slot = b * n_tiles + j
            a_ref[pl.ds(slot, T_SCAN, stride=LRU_SLOTS), :] = a[:, j * LANES:(j + 1) * LANES]
            b_ref[pl.ds(slot, T_SCAN, stride=LRU_SLOTS), :] = inp[:, j * LANES:(j + 1) * LANES]

    def step(t, h):
        base = pl.multiple_of(t * LRU_SLOTS, LRU_SLOTS)
        h = a_ref[pl.ds(base, LRU_SLOTS), :] * h + b_ref[pl.ds(base, LRU_SLOTS), :]
        b_ref[pl.ds(base, LRU_SLOTS), :] = h
        return h

    h_ref[...] = lax.fori_loop(0, T_SCAN, step, h_ref[...], unroll=8)

    for b in range(BATCH):
        h = jnp.concatenate(
            [b_ref[pl.ds(b * n_tiles + j, T_SCAN, stride=LRU_SLOTS), :] for j in range(n_tiles)],
            axis=1)
        o_ref[b] = h * jax.nn.gelu(z_ref[b, :, MIXER_WIDTH:2 * MIXER_WIDTH])


def _lru(zmix3, cw, cb, wa, ba, wx, bx, lam):
    full2 = lambda c: (0, 0)
    vec = pl.BlockSpec((1, MIXER_WIDTH), full2)
    mat = pl.BlockSpec((MIXER_WIDTH, MIXER_WIDTH), full2)
    return pl.pallas_call(
        _lru_kernel,
        grid=(SEQ // T_SCAN,),
        in_specs=[pl.BlockSpec((BATCH, T_SCAN, 2 * MIXER_WIDTH), lambda c: (0, c, 1)),
                  pl.BlockSpec((LRU_CONV, MIXER_WIDTH), full2), vec, mat, vec, mat, vec, vec],
        out_specs=pl.BlockSpec((BATCH, T_SCAN, MIXER_WIDTH), lambda c: (0, c, 0)),
        out_shape=jax.ShapeDtypeStruct((BATCH, SEQ, MIXER_WIDTH), F32),
        scratch_shapes=[pltpu.VMEM((BATCH, SUBLANES, MIXER_WIDTH), F32),
                        pltpu.VMEM((T_SCAN * LRU_SLOTS, LANES), F32),
                        pltpu.VMEM((T_SCAN * LRU_SLOTS, LANES), F32),
                        pltpu.VMEM((LRU_SLOTS, LANES), F32)],
        compiler_params=_params(1),
        name="rglru",
    )(zmix3, cw, cb, wa, ba, wx, bx, lam)


def _attn_kernel(q_ref, qadd_ref, k_ref, kadd_ref, v_ref, o_ref):
    i = pl.program_id(2)
    scale = FOX_HEAD_DIM ** -0.5
    neg = jnp.finfo(F32).min
    denom_lane = lax.broadcasted_iota(jnp.int32, (TK, HEAD_PAD), 1) == FOX_HEAD_DIM
    slots = [slice(n * HEAD_PAD, (n + 1) * HEAD_PAD) for n in range(ATTN_HEADS_PER_STEP)]
    qs = [(q_ref[:, sl].astype(F32) * scale + qadd_ref[:, sl].astype(F32)).astype(BF16) for sl in slots]

    def block(j, carry, masked):
        start = pl.multiple_of(j * TK, TK)
        out = []
        for n, sl in enumerate(slots):
            m, acc = carry[2 * n], carry[2 * n + 1]
            ks = k_ref[pl.ds(start, TK), sl] + kadd_ref[pl.ds(start, TK), sl]
            vs = jnp.where(denom_lane, 1.0, v_ref[pl.ds(start, TK), sl]).astype(BF16)
            s = lax.dot_general(qs[n], ks, (((1,), (1,)), ((), ())), preferred_element_type=F32)
            if masked:
                rr = lax.broadcasted_iota(jnp.int32, (TQ, TK), 0)
                cc = lax.broadcasted_iota(jnp.int32, (TQ, TK), 1)
                s = jnp.where(cc <= rr, s, neg)
            m_new = jnp.maximum(m, jnp.max(s, axis=1, keepdims=True))
            alpha = jnp.exp(m - m_new)
            p = jnp.exp(s - m_new)
            acc = alpha * acc + jnp.dot(p.astype(BF16), vs, preferred_element_type=F32)
            out += [m_new, acc]
        return tuple(out)

    init = (jnp.full((TQ, 1), neg, F32), jnp.zeros((TQ, HEAD_PAD), F32)) * ATTN_HEADS_PER_STEP
    carry = lax.fori_loop(0, i, lambda j, c: block(j, c, False), init)
    fin = block(i, carry, True)
    out_lane = lax.broadcasted_iota(jnp.int32, (TQ, HEAD_PAD), 1) < FOX_HEAD_DIM
    for n, sl in enumerate(slots):
        acc = fin[2 * n + 1]
        o_ref[:, sl] = jnp.where(out_lane, acc / acc[:, FOX_HEAD_DIM:FOX_HEAD_DIM + 1], 0.0)


def _attn(qkv, qadd, kadd):
    n_q = SEQ // TQ
    n_hg = FOX_HEADS // ATTN_HEADS_PER_STEP
    width = ATTN_HEADS_PER_STEP * HEAD_PAD
    qrow = lambda b, h, i: (b * n_q + i, h)
    return pl.pallas_call(
        _attn_kernel,
        grid=(BATCH, n_hg, n_q),
        in_specs=[pl.BlockSpec((TQ, width), qrow),
                  pl.BlockSpec((TQ, width), qrow),
                  pl.BlockSpec((SEQ, width), lambda b, h, i: (b, n_hg + h)),
                  pl.BlockSpec((SEQ, width), lambda b, h, i: (b, h)),
                  pl.BlockSpec((SEQ, width), lambda b, h, i: (b, 2 * n_hg + h))],
        out_specs=pl.BlockSpec((TQ, width), qrow),
        out_shape=jax.ShapeDtypeStruct((N_TOK, FOX_HEADS * HEAD_PAD), F32),
        compiler_params=_params(3),
        name="fox_attention",
    )(qkv, qadd, qkv, kadd, qkv)


def _merge_kernel(ya_ref, yb_ref, yc_ref, yd_ref, x_ref, g_ref, gd_ref, w_ref, o_ref):
    w = MIXER_WIDTH
    parts = [_rms(ya_ref[...], g_ref[:, 0:w], w),
             _rms(yb_ref[...], g_ref[:, w:2 * w], w),
             _rms(yc_ref[...], g_ref[:, 2 * w:3 * w], w),
             _rms(yd_ref[...], gd_ref[...], w)]
    y = jnp.concatenate(parts, axis=1).astype(BF16)
    o_ref[...] = x_ref[...] + jnp.dot(y, w_ref[...], preferred_element_type=F32)


def _merge(ya, yb, yc, yd, x, g, gd, w):
    row = lambda i: (i, 0)
    full = lambda i: (0, 0)
    k_dim = 3 * MIXER_WIDTH + FOX_HEADS * HEAD_PAD
    mix = pl.BlockSpec((TM_MERGE, MIXER_WIDTH), row)
    return pl.pallas_call(
        _merge_kernel,
        grid=(N_TOK // TM_MERGE,),
        in_specs=[mix, mix, mix,
                  pl.BlockSpec((TM_MERGE, FOX_HEADS * HEAD_PAD), row),
                  pl.BlockSpec((TM_MERGE, D_MODEL), row),
                  pl.BlockSpec((1, 3 * MIXER_WIDTH), full),
                  pl.BlockSpec((1, FOX_HEADS * HEAD_PAD), full),
                  pl.BlockSpec((k_dim, D_MODEL), full)],
        out_specs=pl.BlockSpec((TM_MERGE, D_MODEL), row),
        out_shape=jax.ShapeDtypeStruct((N_TOK, D_MODEL), F32),
        compiler_params=_params(1),
        name="merge_out_proj",
    )(ya, yb, yc, yd, x, g, gd, w)


def _mlp_kernel(x_ref, g_ref, w1_ref, w2_ref, fg_ref, o_ref, h_ref, acc_ref, *, final_norm):
    j = pl.program_id(1)

    @pl.when(j == 0)
    def _():
        h_ref[...] = _rms(x_ref[...], g_ref[...], D_MODEL).astype(BF16)
        acc_ref[...] = jnp.zeros_like(acc_ref)

    a = jnp.dot(h_ref[...], w1_ref[...], preferred_element_type=F32)
    a = jnp.square(jnp.maximum(a, 0.0)).astype(BF16)
    acc_ref[...] += jnp.dot(a, w2_ref[...], preferred_element_type=F32)

    @pl.when(j == pl.num_programs(1) - 1)
    def _():
        out = x_ref[...] + acc_ref[...]
        if final_norm:
            out = _rms(out, fg_ref[...], D_MODEL)
        o_ref[...] = out


def _mlp(x, g, w1, w2, fg, final_norm):
    return pl.pallas_call(
        functools.partial(_mlp_kernel, final_norm=final_norm),
        grid=(N_TOK // TM_MLP, D_FF // TF_MLP),
        in_specs=[pl.BlockSpec((TM_MLP, D_MODEL), lambda i, j: (i, 0)),
                  pl.BlockSpec((1, D_MODEL), lambda i, j: (0, 0)),
                  pl.BlockSpec((D_MODEL, TF_MLP), lambda i, j: (0, j)),
                  pl.BlockSpec((TF_MLP, D_MODEL), lambda i, j: (j, 0)),
                  pl.BlockSpec((1, D_MODEL), lambda i, j: (0, 0))],
        out_specs=pl.BlockSpec((TM_MLP, D_MODEL), lambda i, j: (i, 0)),
        out_shape=jax.ShapeDtypeStruct((N_TOK, D_MODEL), F32),
        scratch_shapes=[pltpu.VMEM((TM_MLP, D_MODEL), BF16),
                        pltpu.VMEM((TM_MLP, D_MODEL), F32)],
        compiler_params=_params(2),
        name="mlp",
    )(x, g, w1, w2, fg)


def _pad_heads(w):
    rows = w.shape[0]
    w = w.reshape(rows, FOX_HEADS, FOX_HEAD_DIM)
    w = jnp.pad(w, ((0, 0), (0, 0), (0, HEAD_PAD - FOX_HEAD_DIM)))
    return w.reshape(rows, FOX_HEADS * HEAD_PAD)


def _block_diag(blocks):
    g, r, c = blocks.shape
    eye = jnp.eye(g, dtype=blocks.dtype)
    return jnp.einsum('grc,gh->grhc', blocks, eye).reshape(g * r, g * c)


def _s5_discretize(lam_re, lam_im, log_dt, b_re, b_im):
    dt = jnp.exp(log_dt)[:, None]
    mag = jnp.exp(lam_re * dt)
    abar_re = mag * jnp.cos(lam_im * dt)
    abar_im = mag * jnp.sin(lam_im * dt)
    denom = jnp.square(lam_re) + jnp.square(lam_im)
    num_re = abar_re - 1.0
    num_im = abar_im
    fac_re = (num_re * lam_re + num_im * lam_im) / denom
    fac_im = (num_im * lam_re - num_re * lam_im) / denom
    bbar_re = fac_re[..., None] * b_re - fac_im[..., None] * b_im
    bbar_im = fac_re[..., None] * b_im + fac_im[..., None] * b_re
    return abar_re, abar_im, bbar_re, bbar_im


def kernel(x, norm1_g, w_in, sgu_norm_g, sgu_w, sgu_b, s5_lambda_re, s5_lambda_im, s5_log_dt, s5_b_re, s5_b_im, s5_c_re, s5_c_im, s5_d, s5_glu_w, s5_glu_b, lru_conv_w, lru_conv_b, lru_wa, lru_ba, lru_wx, lru_bx, lru_lambda, fox_fgate_b, mix_norm_g, w_out, norm2_g, w_mlp_in, w_mlp_out, final_g):
    w = MIXER_WIDTH
    xf = x.reshape(N_TOK, D_MODEL)
    row = lambda v: v.reshape(1, -1)
    for l in range(DEPTH):
        wl = w_in[l]
        wmix = jnp.concatenate([wl[:, 0:2 * w], wl[:, 3 * w:5 * w], wl[:, 2 * w:3 * w]], axis=1).astype(BF16)
        wqkv = jnp.concatenate([_pad_heads(wl[:, (5 + m) * w:(6 + m) * w]) for m in range(3)], axis=1).astype(BF16)
        wf = jnp.pad(wl[:, 8 * w:], ((0, 0), (0, LANES - FOX_HEADS))).astype(BF16)
        zmix, qkv, zf = _in_proj(xf, row(norm1_g[l]), wmix, wqkv, wf)
        zmix3 = zmix.reshape(BATCH, SEQ, ZMIX_COLS)

        bf = jnp.pad(fox_fgate_b[l], (0, LANES - FOX_HEADS)).reshape(1, LANES)
        qadd, kadd = _fcum(zf, bf)

        sgu_wcat = jnp.transpose(sgu_w[l], (1, 0, 2)).reshape(SGU_CHUNK, SGU_HEADS * SGU_CHUNK)
        sgu_bias = jnp.repeat(sgu_b[l].T, w // SGU_HEADS, axis=1)
        y_a = _sgu(zmix, row(sgu_norm_g[l]), sgu_wcat, sgu_bias)

        abar_re, abar_im, bbar_re, bbar_im = _s5_discretize(
            s5_lambda_re[l], s5_lambda_im[l], s5_log_dt[l], s5_b_re[l], s5_b_im[l])
        bbd = jnp.concatenate([_block_diag(jnp.transpose(bbar_re, (0, 2, 1))),
                               _block_diag(jnp.transpose(bbar_im, (0, 2, 1)))], axis=1).astype(BF16)
        cbd = jnp.concatenate([_block_diag(jnp.transpose(s5_c_re[l], (0, 2, 1))),
                               -_block_diag(jnp.transpose(s5_c_im[l], (0, 2, 1)))], axis=0).astype(BF16)
        lam = jnp.concatenate([abar_re.reshape(SUBLANES, LANES), abar_im.reshape(SUBLANES, LANES)], axis=0)
        y_b = _s5(zmix3, bbd, lam, cbd, row(s5_d[l]), s5_glu_w[l].astype(BF16), row(s5_glu_b[l]))

        y_c = _lru(zmix3, lru_conv_w[l], row(lru_conv_b[l]),
                   _block_diag(lru_wa[l]).astype(BF16), row(lru_ba[l]),
                   _block_diag(lru_wx[l]).astype(BF16), row(lru_bx[l]), row(lru_lambda[l]))

        y_d = _attn(qkv, qadd, kadd)

        g_mix = mix_norm_g[l]
        w_o = w_out[l]
        w_o_pad = jnp.concatenate(
            [w_o[0:3 * w],
             jnp.pad(w_o[3 * w:].reshape(FOX_HEADS, FOX_HEAD_DIM, D_MODEL),
                     ((0, 0), (0, HEAD_PAD - FOX_HEAD_DIM), (0, 0))).reshape(FOX_HEADS * HEAD_PAD, D_MODEL)],
            axis=0).astype(BF16)
        xf = _merge(y_a, y_b.reshape(N_TOK, w), y_c.reshape(N_TOK, w), y_d, xf,
                    row(g_mix[0:3 * w]), _pad_heads(row(g_mix[3 * w:])), w_o_pad)

        xf = _mlp(xf, row(norm2_g[l]), w_mlp_in[l].astype(BF16), w_mlp_out[l].astype(BF16),
                  row(final_g), final_norm=(l == DEPTH - 1))
    return xf.reshape(BATCH, SEQ, D_MODEL)
```

```python
import functools

import jax
import jax.numpy as jnp
import numpy as np
from jax import lax
from jax.experimental import pallas as pl
from jax.experimental.pallas import tpu as pltpu

D_MODEL = 1024
BATCH = 4
SEQ = 4096
DEPTH = 4
N_TOK = BATCH * SEQ
MIXER_WIDTH = 256
SGU_HEADS = 4
SGU_CHUNK = 128
S5_GROUP = 16
S5_GROUPS = 16
S5_STATE = 64
LRU_HEADS = 4
LRU_CONV = 4
LRU_C = 8.0
FOX_HEADS = 4
FOX_HEAD_DIM = 64
D_FF = 4 * D_MODEL
RMS_EPS = 1e-6
LOG2E = 1.4426950408889634

LANES = 128
SUBLANES = 8
HEAD_PAD = LANES
ZMIX_COLS = 5 * MIXER_WIDTH
QKV_COLS = 3 * FOX_HEADS * HEAD_PAD
S5_NSTATE = S5_GROUPS * S5_STATE
S5_ROWS = 2 * S5_NSTATE // LANES
S5_PITCH = S5_ROWS + SUBLANES

TM_IN = 512
TM_MERGE = 512
TM_MLP = 1024
TF_MLP = 1024
T_SGU = 512
T_SCAN = 128
T_CUM = 512
TQ = 512
TK = 512
ATTN_HEADS_PER_STEP = 4
VMEM_LIMIT = 48 * 1024 * 1024

F32 = jnp.float32
BF16 = jnp.bfloat16


def _params(n_axes):
    return pltpu.CompilerParams(dimension_semantics=("arbitrary",) * n_axes,
                                vmem_limit_bytes=VMEM_LIMIT)


def _layer_block(shape, layer):
    zeros = (0,) * len(shape)
    return pl.BlockSpec((None,) + tuple(shape), lambda *_: (layer,) + zeros)


def _rms(x, g, width):
    ms = jnp.sum(jnp.square(x), axis=-1, keepdims=True) * (1.0 / width)
    return x * lax.rsqrt(ms + RMS_EPS) * g


def _softplus(x):
    return jnp.maximum(x, 0.0) + jnp.log1p(jnp.exp(-jnp.abs(x)))


def _in_proj_kernel(x_ref, g_ref, wmix_ref, wqkv_ref, wf_ref, zmix_ref, qkv_ref, zf_ref):
    h = _rms(x_ref[...], g_ref[...], D_MODEL).astype(BF16)
    zmix_ref[...] = jnp.dot(h, wmix_ref[...], preferred_element_type=F32)
    qkv_ref[...] = jnp.dot(h, wqkv_ref[...], preferred_element_type=F32).astype(BF16)
    zf_ref[...] = jnp.dot(h, wf_ref[...], preferred_element_type=F32)


def _in_proj(x, g, wmix, wqkv, wf, layer):
    row = lambda i: (i, 0)
    return pl.pallas_call(
        _in_proj_kernel,
        grid=(N_TOK // TM_IN,),
        in_specs=[pl.BlockSpec((TM_IN, D_MODEL), row),
                  _layer_block((1, D_MODEL), layer),
                  _layer_block((D_MODEL, ZMIX_COLS), layer),
                  _layer_block((D_MODEL, QKV_COLS), layer),
                  _layer_block((D_MODEL, LANES), layer)],
        out_specs=[pl.BlockSpec((TM_IN, ZMIX_COLS), row),
                   pl.BlockSpec((TM_IN, QKV_COLS), row),
                   pl.BlockSpec((TM_IN, LANES), row)],
        out_shape=[jax.ShapeDtypeStruct((N_TOK, ZMIX_COLS), F32),
                   jax.ShapeDtypeStruct((N_TOK, QKV_COLS), BF16),
                   jax.ShapeDtypeStruct((N_TOK, LANES), F32)],
        compiler_params=_params(1),
        name="in_proj",
    )(x, g, wmix, wqkv, wf)


def _split3(x):
    hi = x.astype(BF16)
    r1 = x - hi.astype(F32)
    mid = r1.astype(BF16)
    lo = (r1 - mid.astype(F32)).astype(BF16)
    return hi, mid, lo


def _bias_placement():
    pq = np.zeros((3 * LANES, FOX_HEADS * HEAD_PAD), np.float32)
    pk = np.zeros_like(pq)
    ones_q = np.zeros((1, FOX_HEADS * HEAD_PAD), np.float32)
    ones_k = np.zeros_like(ones_q)
    for h in range(FOX_HEADS):
        for piece in range(3):
            pq[piece * LANES + h, h * HEAD_PAD + FOX_HEAD_DIM + piece] = 1.0
            pk[piece * LANES + h, h * HEAD_PAD + FOX_HEAD_DIM + 3 + piece] = -1.0
            ones_q[0, h * HEAD_PAD + FOX_HEAD_DIM + 3 + piece] = 1.0
            ones_k[0, h * HEAD_PAD + FOX_HEAD_DIM + piece] = 1.0
    return (jnp.asarray(pq, BF16), jnp.asarray(pk, BF16), jnp.asarray(ones_q), jnp.asarray(ones_k))


def _fcum_kernel(zf_ref, bf_ref, pq_ref, pk_ref, oq_ref, ok_ref, qadd_ref, kadd_ref, carry_ref):
    @pl.when(pl.program_id(1) == 0)
    def _():
        carry_ref[...] = jnp.zeros_like(carry_ref)

    logit = zf_ref[...] + bf_ref[...]
    log_f = -_softplus(-logit)
    r = lax.broadcasted_iota(jnp.int32, (T_CUM, T_CUM), 0)
    c = lax.broadcasted_iota(jnp.int32, (T_CUM, T_CUM), 1)
    tril = jnp.where(c <= r, 1.0, 0.0).astype(BF16)
    hi, mid, lo = _split3(log_f)
    cs = (jnp.dot(tril, hi, preferred_element_type=F32)
          + jnp.dot(tril, mid, preferred_element_type=F32)
          + jnp.dot(tril, lo, preferred_element_type=F32))
    cum = cs + carry_ref[...]
    carry_ref[...] = cum[T_CUM - 1:T_CUM, :]
    pieces = jnp.concatenate(_split3(cum * LOG2E), axis=1)
    qadd_ref[...] = (jnp.dot(pieces, pq_ref[...], preferred_element_type=F32) + oq_ref[...]).astype(BF16)
    kadd_ref[...] = (jnp.dot(pieces, pk_ref[...], preferred_element_type=F32) + ok_ref[...]).astype(BF16)


def _fcum(zf, bf, layer):
    n_c = SEQ // T_CUM
    pq, pk, ones_q, ones_k = _bias_placement()
    full = lambda b, c: (0, 0)
    wide = FOX_HEADS * HEAD_PAD
    return pl.pallas_call(
        _fcum_kernel,
        grid=(BATCH, n_c),
        in_specs=[pl.BlockSpec((T_CUM, LANES), lambda b, c: (b * n_c + c, 0)),
                  _layer_block((1, LANES), layer),
                  pl.BlockSpec((3 * LANES, wide), full),
                  pl.BlockSpec((3 * LANES, wide), full),
                  pl.BlockSpec((1, wide), full),
                  pl.BlockSpec((1, wide), full)],
        out_specs=[pl.BlockSpec((T_CUM, wide), lambda b, c: (b * n_c + c, 0)),
                   pl.BlockSpec((T_CUM, wide), lambda b, c: (b * n_c + c, 0))],
        out_shape=[jax.ShapeDtypeStruct((N_TOK, wide), BF16),
                   jax.ShapeDtypeStruct((N_TOK, wide), BF16)],
        scratch_shapes=[pltpu.VMEM((1, LANES), F32)],
        compiler_params=_params(2),
        name="forget_cumsum",
    )(zf, bf, pq, pk, ones_q, ones_k)


def _sgu_kernel(zu_ref, zv_ref, g_ref, w_ref, b_ref, o_ref):
    u = jax.nn.gelu(zu_ref[...])
    v = _rms(jax.nn.gelu(zv_ref[...]), g_ref[...], MIXER_WIDTH)
    hd = MIXER_WIDTH // SGU_HEADS
    lane_head = lax.broadcasted_iota(jnp.int32, (SGU_CHUNK, MIXER_WIDTH), 1) // hd
    t_idx = lax.broadcasted_iota(jnp.int32, (SGU_CHUNK, SGU_HEADS * SGU_CHUNK), 0)
    s_idx = lax.broadcasted_iota(jnp.int32, (SGU_CHUNK, SGU_HEADS * SGU_CHUNK), 1) % SGU_CHUNK
    wm = jnp.where(s_idx <= t_idx, w_ref[...], 0.0).astype(BF16)
    bias = b_ref[...]
    for c in range(T_SGU // SGU_CHUNK):
        rows = slice(c * SGU_CHUNK, (c + 1) * SGU_CHUNK)
        vc = v[rows]
        vstack = jnp.concatenate(
            [jnp.where(lane_head == h, vc, 0.0) for h in range(SGU_HEADS)], axis=0).astype(BF16)
        mixed = jnp.dot(wm, vstack, preferred_element_type=F32) + bias
        o_ref[rows, :] = u[rows] * mixed


def _sgu(zmix, g, w, b, layer):
    return pl.pallas_call(
        _sgu_kernel,
        grid=(N_TOK // T_SGU,),
        in_specs=[pl.BlockSpec((T_SGU, MIXER_WIDTH), lambda i: (i, 0)),
                  pl.BlockSpec((T_SGU, MIXER_WIDTH), lambda i: (i, 1)),
                  _layer_block((1, MIXER_WIDTH), layer),
                  _layer_block((SGU_CHUNK, SGU_HEADS * SGU_CHUNK), layer),
                  _layer_block((SGU_CHUNK, MIXER_WIDTH), layer)],
        out_specs=pl.BlockSpec((T_SGU, MIXER_WIDTH), lambda i: (i, 0)),
        out_shape=jax.ShapeDtypeStruct((N_TOK, MIXER_WIDTH), F32),
        compiler_params=_params(1),
        name="sgu",
    )(zmix, zmix, g, w, b)


def _s5_kernel(u_ref, bbd_ref, lam_ref, cbd_ref, d_ref, gw_ref, gb_ref, o_ref, s_ref, h_ref):
    @pl.when(pl.program_id(0) == 0)
    def _():
        h_ref[...] = jnp.zeros_like(h_ref)

    for b in range(BATCH):
        bu = jnp.dot(u_ref[b].astype(BF16), bbd_ref[...], preferred_element_type=F32)
        for j in range(S5_ROWS):
            s_ref[b, pl.ds(j, T_SCAN, stride=S5_PITCH), :] = bu[:, j * LANES:(j + 1) * LANES]

    lam_re = lam_ref[0:SUBLANES, :]
    lam_im = lam_ref[SUBLANES:S5_ROWS, :]

    def step(t, carry):
        base = pl.multiple_of(t * S5_PITCH, SUBLANES)
        new = []
        for b in range(BATCH):
            h_re, h_im = carry[2 * b], carry[2 * b + 1]
            n_re = lam_re * h_re - lam_im * h_im + s_ref[b, pl.ds(base, SUBLANES), :]
            n_im = lam_re * h_im + lam_im * h_re + s_ref[b, pl.ds(base + SUBLANES, SUBLANES), :]
            s_ref[b, pl.ds(base, SUBLANES), :] = n_re
            s_ref[b, pl.ds(base + SUBLANES, SUBLANES), :] = n_im
            new += [n_re, n_im]
        return tuple(new)

    init = []
    for b in range(BATCH):
        init += [h_ref[b, 0:SUBLANES, :], h_ref[b, SUBLANES:S5_ROWS, :]]
    fin = lax.fori_loop(0, T_SCAN, step, tuple(init), unroll=8)
    for b in range(BATCH):
        h_ref[b, 0:SUBLANES, :] = fin[2 * b]
        h_ref[b, SUBLANES:S5_ROWS, :] = fin[2 * b + 1]

    for b in range(BATCH):
        states = jnp.concatenate(
            [s_ref[b, pl.ds(j, T_SCAN, stride=S5_PITCH), :] for j in range(S5_ROWS)], axis=1)
        u = u_ref[b]
        y = jnp.dot(states.astype(BF16), cbd_ref[...], preferred_element_type=F32) + d_ref[...] * u
        y = jax.nn.gelu(y)
        gate = jnp.dot(y.astype(BF16), gw_ref[...], preferred_element_type=F32) + gb_ref[...]
        o_ref[b] = y * jax.nn.sigmoid(gate)


def _s5(zmix3, bbd, lam, cbd, d, gw, gb, layer):
    return pl.pallas_call(
        _s5_kernel,
        grid=(SEQ // T_SCAN,),
        in_specs=[pl.BlockSpec((BATCH, T_SCAN, MIXER_WIDTH), lambda c: (0, c, 4)),
                  _layer_block((MIXER_WIDTH, 2 * S5_NSTATE), layer),
                  _layer_block((S5_ROWS, LANES), layer),
                  _layer_block((2 * S5_NSTATE, MIXER_WIDTH), layer),
                  _layer_block((1, MIXER_WIDTH), layer),
                  _layer_block((MIXER_WIDTH, MIXER_WIDTH), layer),
                  _layer_block((1, MIXER_WIDTH), layer)],
        out_specs=pl.BlockSpec((BATCH, T_SCAN, MIXER_WIDTH), lambda c: (0, c, 0)),
        out_shape=jax.ShapeDtypeStruct((BATCH, SEQ, MIXER_WIDTH), F32),
        scratch_shapes=[pltpu.VMEM((BATCH, T_SCAN * S5_PITCH, LANES), F32),
                        pltpu.VMEM((BATCH, S5_ROWS, LANES), F32)],
        compiler_params=_params(1),
        name="s5",
    )(zmix3, bbd, lam, cbd, d, gw, gb)


LRU_SLOTS = BATCH * MIXER_WIDTH // LANES


def _lru_kernel(z_ref, cw_ref, cb_ref, wa_ref, ba_ref, wx_ref, bx_ref, lam_ref, o_ref,
                tail_ref, a_ref, b_ref, h_ref):
    @pl.when(pl.program_id(0) == 0)
    def _():
        tail_ref[...] = jnp.zeros_like(tail_ref)
        h_ref[...] = jnp.zeros_like(h_ref)

    n_tiles = MIXER_WIDTH // LANES
    decay_rate = LRU_C * _softplus(-lam_ref[...])
    for b in range(BATCH):
        x = z_ref[b, :, 0:MIXER_WIDTH]
        xp = jnp.concatenate([tail_ref[b], x], axis=0)
        tail_ref[b] = x[T_SCAN - SUBLANES:T_SCAN, :]
        xc = cb_ref[...]
        for k in range(LRU_CONV):
            off = SUBLANES - (LRU_CONV - 1) + k
            xc = xc + cw_ref[k:k + 1, :] * xp[off:off + T_SCAN, :]
        xcb = xc.astype(BF16)
        r = jax.nn.sigmoid(jnp.dot(xcb, wa_ref[...], preferred_element_type=F32) + ba_ref[...])
        i = jax.nn.sigmoid(jnp.dot(xcb, wx_ref[...], preferred_element_type=F32) + bx_ref[...])
        log_a = -(r * decay_rate)
        a = jnp.exp(log_a)
        inp = jnp.sqrt(-jnp.tanh(log_a) * (a * a + 1.0)) * (i * xc)
        for j in range(n_tiles):
            slot = b * n_tiles + j
            a_ref[pl.ds(slot, T_SCAN, stride=LRU_SLOTS), :] = a[:, j * LANES:(j + 1) * LANES]
            b_ref[pl.ds(slot, T_SCAN, stride=LRU_SLOTS), :] = inp[:, j * LANES:(j + 1) * LANES]

    def step(t, h):
        base = pl.multiple_of(t * LRU_SLOTS, LRU_SLOTS)
        h = a_ref[pl.ds(base, LRU_SLOTS), :] * h + b_ref[pl.ds(base, LRU_SLOTS), :]
        b_ref[pl.ds(base, LRU_SLOTS), :] = h
        return h

    h_ref[...] = lax.fori_loop(0, T_SCAN, step, h_ref[...], unroll=8)

    for b in range(BATCH):
        h = jnp.concatenate(
            [b_ref[pl.ds(b * n_tiles + j, T_SCAN, stride=LRU_SLOTS), :] for j in range(n_tiles)],
            axis=1)
        o_ref[b] = h * jax.nn.gelu(z_ref[b, :, MIXER_WIDTH:2 * MIXER_WIDTH])


def _lru(zmix3, cw, cb, wa, ba, wx, bx, lam, layer):
    vec = _layer_block((1, MIXER_WIDTH), layer)
    mat = _layer_block((MIXER_WIDTH, MIXER_WIDTH), layer)
    return pl.pallas_call(
        _lru_kernel,
        grid=(SEQ // T_SCAN,),
        in_specs=[pl.BlockSpec((BATCH, T_SCAN, 2 * MIXER_WIDTH), lambda c: (0, c, 1)),
                  _layer_block((LRU_CONV, MIXER_WIDTH), layer), vec, mat, vec, mat, vec, vec],
        out_specs=pl.BlockSpec((BATCH, T_SCAN, MIXER_WIDTH), lambda c: (0, c, 0)),
        out_shape=jax.ShapeDtypeStruct((BATCH, SEQ, MIXER_WIDTH), F32),
        scratch_shapes=[pltpu.VMEM((BATCH, SUBLANES, MIXER_WIDTH), F32),
                        pltpu.VMEM((T_SCAN * LRU_SLOTS, LANES), F32),
                        pltpu.VMEM((T_SCAN * LRU_SLOTS, LANES), F32),
                        pltpu.VMEM((LRU_SLOTS, LANES), F32)],
        compiler_params=_params(1),
        name="rglru",
    )(zmix3, cw, cb, wa, ba, wx, bx, lam)


def _attn_kernel(q_ref, qadd_ref, k_ref, kadd_ref, v_ref, o_ref, s_ref, m_ref, acc_ref):
    i = pl.program_id(2)
    scale = FOX_HEAD_DIM ** -0.5 * LOG2E
    neg = jnp.finfo(F32).min
    denom_lane = lax.broadcasted_iota(jnp.int32, (TK, HEAD_PAD), 1) == FOX_HEAD_DIM
    slots = [slice(n * HEAD_PAD, (n + 1) * HEAD_PAD) for n in range(ATTN_HEADS_PER_STEP)]
    qs = [(q_ref[:, sl].astype(F32) * scale + qadd_ref[:, sl].astype(F32)).astype(BF16) for sl in slots]

    def logits(n, j):
        start = pl.multiple_of(j * TK, TK)
        ks = k_ref[pl.ds(start, TK), slots[n]] + kadd_ref[pl.ds(start, TK), slots[n]]
        return lax.dot_general(qs[n], ks, (((1,), (1,)), ((), ())), preferred_element_type=F32)

    def block(j, masked, prefetch):
        start = pl.multiple_of(j * TK, TK)
        for n, sl in enumerate(slots):
            s = s_ref[...] if n == 0 else logits(n, j)
            vs = jnp.where(denom_lane, 1.0, v_ref[pl.ds(start, TK), sl]).astype(BF16)
            if masked:
                rr = lax.broadcasted_iota(jnp.int32, (TQ, TK), 0)
                cc = lax.broadcasted_iota(jnp.int32, (TQ, TK), 1)
                s = jnp.where(cc <= rr, s, neg)
            m = m_ref[n]
            m_new = jnp.maximum(m, jnp.max(s, axis=1, keepdims=True))
            alpha = jnp.exp2(m - m_new)
            p = jnp.exp2(s - jnp.concatenate([m_new] * (TK // LANES), axis=1))
            m_ref[n] = m_new
            acc_ref[n] = alpha * acc_ref[n] + jnp.dot(p.astype(BF16), vs, preferred_element_type=F32)
        if prefetch:
            s_ref[...] = logits(0, j + 1)

    s_ref[...] = logits(0, 0)
    m_ref[...] = jnp.full(m_ref.shape, neg, F32)
    acc_ref[...] = jnp.zeros_like(acc_ref)

    @pl.loop(0, i)
    def _(j):
        block(j, False, True)

    block(i, True, False)
    out_lane = lax.broadcasted_iota(jnp.int32, (TQ, HEAD_PAD), 1) < FOX_HEAD_DIM
    for n, sl in enumerate(slots):
        acc = acc_ref[n]
        o_ref[:, sl] = jnp.where(out_lane, acc / acc[:, FOX_HEAD_DIM:FOX_HEAD_DIM + 1], 0.0)


def _attn(qkv, qadd, kadd):
    n_q = SEQ // TQ
    n_hg = FOX_HEADS // ATTN_HEADS_PER_STEP
    width = ATTN_HEADS_PER_STEP * HEAD_PAD
    qrow = lambda b, h, i: (b * n_q + i, h)
    return pl.pallas_call(
        _attn_kernel,
        grid=(BATCH, n_hg, n_q),
        in_specs=[pl.BlockSpec((TQ, width), qrow),
                  pl.BlockSpec((TQ, width), qrow),
                  pl.BlockSpec((SEQ, width), lambda b, h, i: (b, n_hg + h)),
                  pl.BlockSpec((SEQ, width), lambda b, h, i: (b, h)),
                  pl.BlockSpec((SEQ, width), lambda b, h, i: (b, 2 * n_hg + h))],
        out_specs=pl.BlockSpec((TQ, width), qrow),
        out_shape=jax.ShapeDtypeStruct((N_TOK, FOX_HEADS * HEAD_PAD), F32),
        scratch_shapes=[pltpu.VMEM((TQ, TK), F32),
                        pltpu.VMEM((ATTN_HEADS_PER_STEP, TQ, LANES), F32),
                        pltpu.VMEM((ATTN_HEADS_PER_STEP, TQ, HEAD_PAD), F32)],
        compiler_params=_params(3),
        name="fox_attention",
    )(qkv, qadd, qkv, kadd, qkv)


def _merge_kernel(ya_ref, yb_ref, yc_ref, yd_ref, x_ref, g_ref, gd_ref, w_ref, o_ref):
    w = MIXER_WIDTH
    parts = [_rms(ya_ref[...], g_ref[:, 0:w], w),
             _rms(yb_ref[...], g_ref[:, w:2 * w], w),
             _rms(yc_ref[...], g_ref[:, 2 * w:3 * w], w),
             _rms(yd_ref[...], gd_ref[...], w)]
    y = jnp.concatenate(parts, axis=1).astype(BF16)
    o_ref[...] = x_ref[...] + jnp.dot(y, w_ref[...], preferred_element_type=F32)


def _merge(ya, yb, yc, yd, x, g, gd, w, layer):
    row = lambda i: (i, 0)
    k_dim = 3 * MIXER_WIDTH + FOX_HEADS * HEAD_PAD
    mix = pl.BlockSpec((TM_MERGE, MIXER_WIDTH), row)
    return pl.pallas_call(
        _merge_kernel,
        grid=(N_TOK // TM_MERGE,),
        in_specs=[mix, mix, mix,
                  pl.BlockSpec((TM_MERGE, FOX_HEADS * HEAD_PAD), row),
                  pl.BlockSpec((TM_MERGE, D_MODEL), row),
                  _layer_block((1, 3 * MIXER_WIDTH), layer),
                  _layer_block((1, FOX_HEADS * HEAD_PAD), layer),
                  _layer_block((k_dim, D_MODEL), layer)],
        out_specs=pl.BlockSpec((TM_MERGE, D_MODEL), row),
        out_shape=jax.ShapeDtypeStruct((N_TOK, D_MODEL), F32),
        compiler_params=_params(1),
        name="merge_out_proj",
    )(ya, yb, yc, yd, x, g, gd, w)


def _mlp_kernel(x_ref, g_ref, w1_ref, w2_ref, fg_ref, o_ref, h_ref, acc_ref, *, final_norm):
    j = pl.program_id(1)

    @pl.when(j == 0)
    def _():
        h_ref[...] = _rms(x_ref[...], g_ref[...], D_MODEL).astype(BF16)
        acc_ref[...] = jnp.zeros_like(acc_ref)

    a = jnp.dot(h_ref[...], w1_ref[...], preferred_element_type=F32)
    a = jnp.square(jnp.maximum(a, 0.0)).astype(BF16)
    acc_ref[...] += jnp.dot(a, w2_ref[...], preferred_element_type=F32)

    @pl.when(j == pl.num_programs(1) - 1)
    def _():
        out = x_ref[...] + acc_ref[...]
        if final_norm:
            out = _rms(out, fg_ref[...], D_MODEL)
        o_ref[...] = out


def _mlp(x, g, w1, w2, fg, layer, final_norm):
    return pl.pallas_call(
        functools.partial(_mlp_kernel, final_norm=final_norm),
        grid=(N_TOK // TM_MLP, D_FF // TF_MLP),
        in_specs=[pl.BlockSpec((TM_MLP, D_MODEL), lambda i, j: (i, 0)),
                  _layer_block((1, D_MODEL), layer),
                  pl.BlockSpec((None, D_MODEL, TF_MLP), lambda i, j: (layer, 0, j)),
                  pl.BlockSpec((None, TF_MLP, D_MODEL), lambda i, j: (layer, j, 0)),
                  pl.BlockSpec((1, D_MODEL), lambda i, j: (0, 0))],
        out_specs=pl.BlockSpec((TM_MLP, D_MODEL), lambda i, j: (i, 0)),
        out_shape=jax.ShapeDtypeStruct((N_TOK, D_MODEL), F32),
        scratch_shapes=[pltpu.VMEM((TM_MLP, D_MODEL), BF16),
                        pltpu.VMEM((TM_MLP, D_MODEL), F32)],
        compiler_params=_params(2),
        name="mlp",
    )(x, g, w1, w2, fg)


def _pad_heads(w):
    lead = w.shape[:-1]
    w = w.reshape(*lead, FOX_HEADS, FOX_HEAD_DIM)
    w = jnp.pad(w, [(0, 0)] * (len(lead) + 1) + [(0, HEAD_PAD - FOX_HEAD_DIM)])
    return w.reshape(*lead, FOX_HEADS * HEAD_PAD)


def _block_diag(blocks):
    n, g, r, c = blocks.shape
    eye = jnp.eye(g, dtype=blocks.dtype)
    return jnp.einsum('ngrc,gh->ngrhc', blocks, eye).reshape(n, g * r, g * c)


def _s5_discretize(lam_re, lam_im, log_dt, b_re, b_im):
    dt = jnp.exp(log_dt)[..., None]
    mag = jnp.exp(lam_re * dt)
    abar_re = mag * jnp.cos(lam_im * dt)
    abar_im = mag * jnp.sin(lam_im * dt)
    denom = jnp.square(lam_re) + jnp.square(lam_im)
    num_re = abar_re - 1.0
    num_im = abar_im
    fac_re = (num_re * lam_re + num_im * lam_im) / denom
    fac_im = (num_im * lam_re - num_re * lam_im) / denom
    bbar_re = fac_re[..., None] * b_re - fac_im[..., None] * b_im
    bbar_im = fac_re[..., None] * b_im + fac_im[..., None] * b_re
    return abar_re, abar_im, bbar_re, bbar_im


def kernel(x, norm1_g, w_in, sgu_norm_g, sgu_w, sgu_b, s5_lambda_re, s5_lambda_im, s5_log_dt, s5_b_re, s5_b_im, s5_c_re, s5_c_im, s5_d, s5_glu_w, s5_glu_b, lru_conv_w, lru_conv_b, lru_wa, lru_ba, lru_wx, lru_bx, lru_lambda, fox_fgate_b, mix_norm_g, w_out, norm2_g, w_mlp_in, w_mlp_out, final_g):
    w = MIXER_WIDTH
    row = lambda v: v.reshape(DEPTH, 1, -1)

    wmix = jnp.concatenate([w_in[..., 0:2 * w], w_in[..., 3 * w:5 * w], w_in[..., 2 * w:3 * w]], axis=-1).astype(BF16)
    wqkv = jnp.concatenate([_pad_heads(w_in[..., (5 + m) * w:(6 + m) * w]) for m in range(3)], axis=-1).astype(BF16)
    wf = jnp.pad(w_in[..., 8 * w:], ((0, 0), (0, 0), (0, LANES - FOX_HEADS))).astype(BF16)
    bf = row(jnp.pad(fox_fgate_b, ((0, 0), (0, LANES - FOX_HEADS))))

    sgu_wcat = jnp.transpose(sgu_w, (0, 2, 1, 3)).reshape(DEPTH, SGU_CHUNK, SGU_HEADS * SGU_CHUNK)
    sgu_bias = jnp.repeat(jnp.transpose(sgu_b, (0, 2, 1)), w // SGU_HEADS, axis=2)

    abar_re, abar_im, bbar_re, bbar_im = _s5_discretize(s5_lambda_re, s5_lambda_im, s5_log_dt, s5_b_re, s5_b_im)
    swap = lambda t: jnp.transpose(t, (0, 1, 3, 2))
    bbd = jnp.concatenate([_block_diag(swap(bbar_re)), _block_diag(swap(bbar_im))], axis=2).astype(BF16)
    cbd = jnp.concatenate([_block_diag(swap(s5_c_re)), -_block_diag(swap(s5_c_im))], axis=1).astype(BF16)
    lam = jnp.concatenate([abar_re.reshape(DEPTH, SUBLANES, LANES), abar_im.reshape(DEPTH, SUBLANES, LANES)], axis=1)
    glu_w = s5_glu_w.astype(BF16)

    wa_bd = _block_diag(lru_wa).astype(BF16)
    wx_bd = _block_diag(lru_wx).astype(BF16)

    w_o_pad = jnp.concatenate(
        [w_out[:, 0:3 * w],
         jnp.pad(w_out[:, 3 * w:].reshape(DEPTH, FOX_HEADS, FOX_HEAD_DIM, D_MODEL),
                 ((0, 0), (0, 0), (0, HEAD_PAD - FOX_HEAD_DIM), (0, 0))).reshape(DEPTH, FOX_HEADS * HEAD_PAD, D_MODEL)],
        axis=1).astype(BF16)
    g_mix = row(mix_norm_g[:, 0:3 * w])
    g_mix_d = _pad_heads(row(mix_norm_g[:, 3 * w:]))
    w1 = w_mlp_in.astype(BF16)
    w2 = w_mlp_out.astype(BF16)
    g1, g2 = row(norm1_g), row(norm2_g)
    fg = final_g.reshape(1, D_MODEL)

    xf = x.reshape(N_TOK, D_MODEL)
    for l in range(DEPTH):
        zmix, qkv, zf = _in_proj(xf, g1, wmix, wqkv, wf, l)
        zmix3 = zmix.reshape(BATCH, SEQ, ZMIX_COLS)
        qadd, kadd = _fcum(zf, bf, l)
        y_a = _sgu(zmix, row(sgu_norm_g), sgu_wcat, sgu_bias, l)
        y_b = _s5(zmix3, bbd, lam, cbd, row(s5_d), glu_w, row(s5_glu_b), l)
        y_c = _lru(zmix3, lru_conv_w, row(lru_conv_b), wa_bd, row(lru_ba.reshape(DEPTH, w)),
                   wx_bd, row(lru_bx.reshape(DEPTH, w)), row(lru_lambda), l)
        y_d = _attn(qkv, qadd, kadd)
        xf = _merge(y_a, y_b.reshape(N_TOK, w), y_c.reshape(N_TOK, w), y_d, xf, g_mix, g_mix_d, w_o_pad, l)
        xf = _mlp(xf, g2, w1, w2, fg, l, final_norm=(l == DEPTH - 1))
    return xf.reshape(BATCH, SEQ, D_MODEL)
```

```python
import functools

import jax
import jax.numpy as jnp
import numpy as np
from jax import lax
from jax.experimental import pallas as pl
from jax.experimental.pallas import tpu as pltpu

D_MODEL = 1024
BATCH = 4
SEQ = 4096
DEPTH = 4
N_TOK = BATCH * SEQ
MIXER_WIDTH = 256
SGU_HEADS = 4
SGU_CHUNK = 128
S5_GROUP = 16
S5_GROUPS = 16
S5_STATE = 64
LRU_HEADS = 4
LRU_CONV = 4
LRU_C = 8.0
FOX_HEADS = 4
FOX_HEAD_DIM = 64
D_FF = 4 * D_MODEL
RMS_EPS = 1e-6
LOG2E = 1.4426950408889634

LANES = 128
SUBLANES = 8
HEAD_PAD = LANES
ZMIX_COLS = 5 * MIXER_WIDTH
QKV_COLS = 3 * FOX_HEADS * HEAD_PAD
S5_NSTATE = S5_GROUPS * S5_STATE
S5_ROWS = 2 * S5_NSTATE // LANES
S5_PITCH = S5_ROWS + SUBLANES

TM_IN = 512
TM_MERGE = 512
TM_MLP = 1024
TF_MLP = 1024
T_SGU = 2048
T_SCAN = 256
T_CUM = 512
TQ = 512
TK = 512
ATTN_HEADS_PER_STEP = 4
VMEM_LIMIT = 48 * 1024 * 1024

F32 = jnp.float32
BF16 = jnp.bfloat16


def _params(n_axes):
    return pltpu.CompilerParams(dimension_semantics=("arbitrary",) * n_axes,
                                vmem_limit_bytes=VMEM_LIMIT)


def _layer_block(shape, layer):
    zeros = (0,) * len(shape)
    return pl.BlockSpec((None,) + tuple(shape), lambda *_: (layer,) + zeros)


def _rms(x, g, width):
    ms = jnp.sum(jnp.square(x), axis=-1, keepdims=True) * (1.0 / width)
    return x * lax.rsqrt(ms + RMS_EPS) * g


def _softplus(x):
    return jnp.maximum(x, 0.0) + jnp.log1p(jnp.exp(-jnp.abs(x)))


def _in_proj_kernel(x_ref, g_ref, wmix_ref, wqkv_ref, wf_ref, zmix_ref, qkv_ref, zf_ref):
    h = _rms(x_ref[...], g_ref[...], D_MODEL).astype(BF16)
    zmix_ref[...] = jnp.dot(h, wmix_ref[...], preferred_element_type=F32)
    qkv_ref[...] = jnp.dot(h, wqkv_ref[...], preferred_element_type=F32).astype(BF16)
    zf_ref[...] = jnp.dot(h, wf_ref[...], preferred_element_type=F32)


def _in_proj(x, g, wmix, wqkv, wf, layer):
    row = lambda i: (i, 0)
    return pl.pallas_call(
        _in_proj_kernel,
        grid=(N_TOK // TM_IN,),
        in_specs=[pl.BlockSpec((TM_IN, D_MODEL), row),
                  _layer_block((1, D_MODEL), layer),
                  _layer_block((D_MODEL, ZMIX_COLS), layer),
                  _layer_block((D_MODEL, QKV_COLS), layer),
                  _layer_block((D_MODEL, LANES), layer)],
        out_specs=[pl.BlockSpec((TM_IN, ZMIX_COLS), row),
                   pl.BlockSpec((TM_IN, QKV_COLS), row),
                   pl.BlockSpec((TM_IN, LANES), row)],
        out_shape=[jax.ShapeDtypeStruct((N_TOK, ZMIX_COLS), F32),
                   jax.ShapeDtypeStruct((N_TOK, QKV_COLS), BF16),
                   jax.ShapeDtypeStruct((N_TOK, LANES), F32)],
        compiler_params=_params(1),
        name="in_proj",
    )(x, g, wmix, wqkv, wf)


def _split3(x):
    hi = x.astype(BF16)
    r1 = x - hi.astype(F32)
    mid = r1.astype(BF16)
    lo = (r1 - mid.astype(F32)).astype(BF16)
    return hi, mid, lo


def _bias_placement():
    pq = np.zeros((3 * LANES, FOX_HEADS * HEAD_PAD), np.float32)
    pk = np.zeros_like(pq)
    ones_q = np.zeros((1, FOX_HEADS * HEAD_PAD), np.float32)
    ones_k = np.zeros_like(ones_q)
    for h in range(FOX_HEADS):
        for piece in range(3):
            pq[piece * LANES + h, h * HEAD_PAD + FOX_HEAD_DIM + piece] = 1.0
            pk[piece * LANES + h, h * HEAD_PAD + FOX_HEAD_DIM + 3 + piece] = -1.0
            ones_q[0, h * HEAD_PAD + FOX_HEAD_DIM + 3 + piece] = 1.0
            ones_k[0, h * HEAD_PAD + FOX_HEAD_DIM + piece] = 1.0
    return (jnp.asarray(pq, BF16), jnp.asarray(pk, BF16), jnp.asarray(ones_q), jnp.asarray(ones_k))


def _fcum_kernel(zf_ref, bf_ref, tril_ref, pq_ref, pk_ref, oq_ref, ok_ref, qadd_ref, kadd_ref, carry_ref):
    @pl.when(pl.program_id(1) == 0)
    def _():
        carry_ref[...] = jnp.zeros_like(carry_ref)

    logit = zf_ref[...] + bf_ref[...]
    log_f = -_softplus(-logit)
    tril = tril_ref[...]
    hi, mid, lo = _split3(log_f)
    cs = (jnp.dot(tril, hi, preferred_element_type=F32)
          + jnp.dot(tril, mid, preferred_element_type=F32)
          + jnp.dot(tril, lo, preferred_element_type=F32))
    cum = cs + carry_ref[...]
    carry_ref[...] = cum[T_CUM - 1:T_CUM, :]
    pieces = jnp.concatenate(_split3(cum * LOG2E), axis=1)
    qadd_ref[...] = (jnp.dot(pieces, pq_ref[...], preferred_element_type=F32) + oq_ref[...]).astype(BF16)
    kadd_ref[...] = (jnp.dot(pieces, pk_ref[...], preferred_element_type=F32) + ok_ref[...]).astype(BF16)


def _fcum(zf, bf, layer):
    n_c = SEQ // T_CUM
    pq, pk, ones_q, ones_k = _bias_placement()
    tril = jnp.asarray(np.tril(np.ones((T_CUM, T_CUM), np.float32)), BF16)
    full = lambda b, c: (0, 0)
    wide = FOX_HEADS * HEAD_PAD
    return pl.pallas_call(
        _fcum_kernel,
        grid=(BATCH, n_c),
        in_specs=[pl.BlockSpec((T_CUM, LANES), lambda b, c: (b * n_c + c, 0)),
                  _layer_block((1, LANES), layer),
                  pl.BlockSpec((T_CUM, T_CUM), full),
                  pl.BlockSpec((3 * LANES, wide), full),
                  pl.BlockSpec((3 * LANES, wide), full),
                  pl.BlockSpec((1, wide), full),
                  pl.BlockSpec((1, wide), full)],
        out_specs=[pl.BlockSpec((T_CUM, wide), lambda b, c: (b * n_c + c, 0)),
                   pl.BlockSpec((T_CUM, wide), lambda b, c: (b * n_c + c, 0))],
        out_shape=[jax.ShapeDtypeStruct((N_TOK, wide), BF16),
                   jax.ShapeDtypeStruct((N_TOK, wide), BF16)],
        scratch_shapes=[pltpu.VMEM((1, LANES), F32)],
        compiler_params=_params(2),
        name="forget_cumsum",
    )(zf, bf, tril, pq, pk, ones_q, ones_k)


def _sgu_kernel(zu_ref, zv_ref, g_ref, w_ref, b_ref, o_ref):
    u = jax.nn.gelu(zu_ref[...])
    v = _rms(jax.nn.gelu(zv_ref[...]), g_ref[...], MIXER_WIDTH)
    hd = MIXER_WIDTH // SGU_HEADS
    lane_head = lax.broadcasted_iota(jnp.int32, (SGU_CHUNK, MIXER_WIDTH), 1) // hd
    t_idx = lax.broadcasted_iota(jnp.int32, (SGU_CHUNK, SGU_HEADS * SGU_CHUNK), 0)
    s_idx = lax.broadcasted_iota(jnp.int32, (SGU_CHUNK, SGU_HEADS * SGU_CHUNK), 1) % SGU_CHUNK
    wm = jnp.where(s_idx <= t_idx, w_ref[...], 0.0).astype(BF16)
    bias = b_ref[...]
    for c in range(T_SGU // SGU_CHUNK):
        rows = slice(c * SGU_CHUNK, (c + 1) * SGU_CHUNK)
        vc = v[rows]
        vstack = jnp.concatenate(
            [jnp.where(lane_head == h, vc, 0.0) for h in range(SGU_HEADS)], axis=0).astype(BF16)
        mixed = jnp.dot(wm, vstack, preferred_element_type=F32) + bias
        o_ref[rows, :] = u[rows] * mixed


def _sgu(zmix, g, w, b, layer):
    return pl.pallas_call(
        _sgu_kernel,
        grid=(N_TOK // T_SGU,),
        in_specs=[pl.BlockSpec((T_SGU, MIXER_WIDTH), lambda i: (i, 0)),
                  pl.BlockSpec((T_SGU, MIXER_WIDTH), lambda i: (i, 1)),
                  _layer_block((1, MIXER_WIDTH), layer),
                  _layer_block((SGU_CHUNK, SGU_HEADS * SGU_CHUNK), layer),
                  _layer_block((SGU_CHUNK, MIXER_WIDTH), layer)],
        out_specs=pl.BlockSpec((T_SGU, MIXER_WIDTH), lambda i: (i, 0)),
        out_shape=jax.ShapeDtypeStruct((N_TOK, MIXER_WIDTH), F32),
        compiler_params=_params(1),
        name="sgu",
    )(zmix, zmix, g, w, b)


def _s5_kernel(u_ref, bbd_ref, lam_ref, cbd_ref, d_ref, gw_ref, gb_ref, o_ref, s_ref, h_ref):
    @pl.when(pl.program_id(0) == 0)
    def _():
        h_ref[...] = jnp.zeros_like(h_ref)

    for b in range(BATCH):
        bu = jnp.dot(u_ref[b].astype(BF16), bbd_ref[...], preferred_element_type=F32)
        for j in range(S5_ROWS):
            s_ref[b, pl.ds(j, T_SCAN, stride=S5_PITCH), :] = bu[:, j * LANES:(j + 1) * LANES]

    lam_re = lam_ref[0:SUBLANES, :]
    lam_im = lam_ref[SUBLANES:S5_ROWS, :]

    def step(t, carry):
        base = pl.multiple_of(t * S5_PITCH, SUBLANES)
        new = []
        for b in range(BATCH):
            h_re, h_im = carry[2 * b], carry[2 * b + 1]
            n_re = lam_re * h_re - lam_im * h_im + s_ref[b, pl.ds(base, SUBLANES), :]
            n_im = lam_re * h_im + lam_im * h_re + s_ref[b, pl.ds(base + SUBLANES, SUBLANES), :]
            s_ref[b, pl.ds(base, SUBLANES), :] = n_re
            s_ref[b, pl.ds(base + SUBLANES, SUBLANES), :] = n_im
            new += [n_re, n_im]
        return tuple(new)

    init = []
    for b in range(BATCH):
        init += [h_ref[b, 0:SUBLANES, :], h_ref[b, SUBLANES:S5_ROWS, :]]
    fin = lax.fori_loop(0, T_SCAN, step, tuple(init), unroll=8)
    for b in range(BATCH):
        h_ref[b, 0:SUBLANES, :] = fin[2 * b]
        h_ref[b, SUBLANES:S5_ROWS, :] = fin[2 * b + 1]

    for b in range(BATCH):
        states = jnp.concatenate(
            [s_ref[b, pl.ds(j, T_SCAN, stride=S5_PITCH), :] for j in range(S5_ROWS)], axis=1)
        u = u_ref[b]
        y = jnp.dot(states.astype(BF16), cbd_ref[...], preferred_element_type=F32) + d_ref[...] * u
        y = jax.nn.gelu(y)
        gate = jnp.dot(y.astype(BF16), gw_ref[...], preferred_element_type=F32) + gb_ref[...]
        o_ref[b] = y * jax.nn.sigmoid(gate)


def _s5(zmix3, bbd, lam, cbd, d, gw, gb, layer):
    return pl.pallas_call(
        _s5_kernel,
        grid=(SEQ // T_SCAN,),
        in_specs=[pl.BlockSpec((BATCH, T_SCAN, MIXER_WIDTH), lambda c: (0, c, 4)),
                  _layer_block((MIXER_WIDTH, 2 * S5_NSTATE), layer),
                  _layer_block((S5_ROWS, LANES), layer),
                  _layer_block((2 * S5_NSTATE, MIXER_WIDTH), layer),
                  _layer_block((1, MIXER_WIDTH), layer),
                  _layer_block((MIXER_WIDTH, MIXER_WIDTH), layer),
                  _layer_block((1, MIXER_WIDTH), layer)],
        out_specs=pl.BlockSpec((BATCH, T_SCAN, MIXER_WIDTH), lambda c: (0, c, 0)),
        out_shape=jax.ShapeDtypeStruct((BATCH, SEQ, MIXER_WIDTH), F32),
        scratch_shapes=[pltpu.VMEM((BATCH, T_SCAN * S5_PITCH, LANES), F32),
                        pltpu.VMEM((BATCH, S5_ROWS, LANES), F32)],
        compiler_params=_params(1),
        name="s5",
    )(zmix3, bbd, lam, cbd, d, gw, gb)


LRU_SLOTS = BATCH * MIXER_WIDTH // LANES


def _lru_kernel(z_ref, cw_ref, cb_ref, wa_ref, ba_ref, wx_ref, bx_ref, lam_ref, o_ref,
                tail_ref, a_ref, b_ref, h_ref):
    @pl.when(pl.program_id(0) == 0)
    def _():
        tail_ref[...] = jnp.zeros_like(tail_ref)
        h_ref[...] = jnp.zeros_like(h_ref)

    n_tiles = MIXER_WIDTH // LANES
    decay_rate = LRU_C * _softplus(-lam_ref[...])
    for b in range(BATCH):
        x = z_ref[b, :, 0:MIXER_WIDTH]
        xp = jnp.concatenate([tail_ref[b], x], axis=0)
        tail_ref[b] = x[T_SCAN - SUBLANES:T_SCAN, :]
        xc = cb_ref[...]
        for k in range(LRU_CONV):
            off = SUBLANES - (LRU_CONV - 1) + k
            xc = xc + cw_ref[k:k + 1, :] * xp[off:off + T_SCAN, :]
        xcb = xc.astype(BF16)
        r = jax.nn.sigmoid(jnp.dot(xcb, wa_ref[...], preferred_element_type=F32) + ba_ref[...])
        i = jax.nn.sigmoid(jnp.dot(xcb, wx_ref[...], preferred_element_type=F32) + bx_ref[...])
        log_a = -(r * decay_rate)
        a = jnp.exp(log_a)
        inp = jnp.sqrt(-jnp.tanh(log_a) * (a * a + 1.0)) * (i * xc)
        for j in range(n_tiles):
            slot = b * n_tiles + j
            a_ref[pl.ds(slot, T_SCAN, stride=LRU_SLOTS), :] = a[:, j * LANES:(j + 1) * LANES]
            b_ref[pl.ds(slot, T_SCAN, stride=LRU_SLOTS), :] = inp[:, j * LANES:(j + 1) * LANES]

    def step(t, h):
        base = pl.multiple_of(t * LRU_SLOTS, LRU_SLOTS)
        h = a_ref[pl.ds(base, LRU_SLOTS), :] * h + b_ref[pl.ds(base, LRU_SLOTS), :]
        b_ref[pl.ds(base, LRU_SLOTS), :] = h
        return h

    h_ref[...] = lax.fori_loop(0, T_SCAN, step, h_ref[...], unroll=8)

    for b in range(BATCH):
        h = jnp.concatenate(
            [b_ref[pl.ds(b * n_tiles + j, T_SCAN, stride=LRU_SLOTS), :] for j in range(n_tiles)],
            axis=1)
        o_ref[b] = h * jax.nn.gelu(z_ref[b, :, MIXER_WIDTH:2 * MIXER_WIDTH])


def _lru(zmix3, cw, cb, wa, ba, wx, bx, lam, layer):
    vec = _layer_block((1, MIXER_WIDTH), layer)
    mat = _layer_block((MIXER_WIDTH, MIXER_WIDTH), layer)
    return pl.pallas_call(
        _lru_kernel,
        grid=(SEQ // T_SCAN,),
        in_specs=[pl.BlockSpec((BATCH, T_SCAN, 2 * MIXER_WIDTH), lambda c: (0, c, 1)),
                  _layer_block((LRU_CONV, MIXER_WIDTH), layer), vec, mat, vec, mat, vec, vec],
        out_specs=pl.BlockSpec((BATCH, T_SCAN, MIXER_WIDTH), lambda c: (0, c, 0)),
        out_shape=jax.ShapeDtypeStruct((BATCH, SEQ, MIXER_WIDTH), F32),
        scratch_shapes=[pltpu.VMEM((BATCH, SUBLANES, MIXER_WIDTH), F32),
                        pltpu.VMEM((T_SCAN * LRU_SLOTS, LANES), F32),
                        pltpu.VMEM((T_SCAN * LRU_SLOTS, LANES), F32),
                        pltpu.VMEM((LRU_SLOTS, LANES), F32)],
        compiler_params=_params(1),
        name="rglru",
    )(zmix3, cw, cb, wa, ba, wx, bx, lam)


def _attn_kernel(q_ref, qadd_ref, k_ref, kadd_ref, v_ref, o_ref, s_ref, m_ref, acc_ref):
    i = pl.program_id(2)
    scale = FOX_HEAD_DIM ** -0.5 * LOG2E
    neg = jnp.finfo(F32).min
    denom_lane = lax.broadcasted_iota(jnp.int32, (TK, HEAD_PAD), 1) == FOX_HEAD_DIM
    slots = [slice(n * HEAD_PAD, (n + 1) * HEAD_PAD) for n in range(ATTN_HEADS_PER_STEP)]
    qs = [(q_ref[:, sl].astype(F32) * scale + qadd_ref[:, sl].astype(F32)).astype(BF16) for sl in slots]

    def logits(n, j):
        start = pl.multiple_of(j * TK, TK)
        ks = k_ref[pl.ds(start, TK), slots[n]] + kadd_ref[pl.ds(start, TK), slots[n]]
        return lax.dot_general(qs[n], ks, (((1,), (1,)), ((), ())), preferred_element_type=F32)

    def block(j, masked, prefetch):
        start = pl.multiple_of(j * TK, TK)
        for n, sl in enumerate(slots):
            s = s_ref[n]
            vs = jnp.where(denom_lane, 1.0, v_ref[pl.ds(start, TK), sl]).astype(BF16)
            if masked:
                rr = lax.broadcasted_iota(jnp.int32, (TQ, TK), 0)
                cc = lax.broadcasted_iota(jnp.int32, (TQ, TK), 1)
                s = jnp.where(cc <= rr, s, neg)
            m = m_ref[n]
            m_new = jnp.maximum(m, jnp.max(s, axis=1, keepdims=True))
            alpha = jnp.exp2(m - m_new)
            p = jnp.exp2(s - jnp.concatenate([m_new] * (TK // LANES), axis=1))
            m_ref[n] = m_new
            acc_ref[n] = alpha * acc_ref[n] + jnp.dot(p.astype(BF16), vs, preferred_element_type=F32)
            if prefetch:
                s_ref[n] = logits(n, j + 1)

    for n in range(ATTN_HEADS_PER_STEP):
        s_ref[n] = logits(n, 0)
    m_ref[...] = jnp.full(m_ref.shape, neg, F32)
    acc_ref[...] = jnp.zeros_like(acc_ref)

    @pl.loop(0, i)
    def _(j):
        block(j, False, True)

    block(i, True, False)
    out_lane = lax.broadcasted_iota(jnp.int32, (TQ, HEAD_PAD), 1) < FOX_HEAD_DIM
    for n, sl in enumerate(slots):
        acc = acc_ref[n]
        o_ref[:, sl] = jnp.where(out_lane, acc / acc[:, FOX_HEAD_DIM:FOX_HEAD_DIM + 1], 0.0)


def _attn(qkv, qadd, kadd):
    n_q = SEQ // TQ
    n_hg = FOX_HEADS // ATTN_HEADS_PER_STEP
    width = ATTN_HEADS_PER_STEP * HEAD_PAD
    qrow = lambda b, h, i: (b * n_q + i, h)
    return pl.pallas_call(
        _attn_kernel,
        grid=(BATCH, n_hg, n_q),
        in_specs=[pl.BlockSpec((TQ, width), qrow),
                  pl.BlockSpec((TQ, width), qrow),
                  pl.BlockSpec((SEQ, width), lambda b, h, i: (b, n_hg + h)),
                  pl.BlockSpec((SEQ, width), lambda b, h, i: (b, h)),
                  pl.BlockSpec((SEQ, width), lambda b, h, i: (b, 2 * n_hg + h))],
        out_specs=pl.BlockSpec((TQ, width), qrow),
        out_shape=jax.ShapeDtypeStruct((N_TOK, FOX_HEADS * HEAD_PAD), F32),
        scratch_shapes=[pltpu.VMEM((ATTN_HEADS_PER_STEP, TQ, TK), F32),
                        pltpu.VMEM((ATTN_HEADS_PER_STEP, TQ, LANES), F32),
                        pltpu.VMEM((ATTN_HEADS_PER_STEP, TQ, HEAD_PAD), F32)],
        compiler_params=_params(3),
        name="fox_attention",
    )(qkv, qadd, qkv, kadd, qkv)


def _merge_kernel(ya_ref, yb_ref, yc_ref, yd_ref, x_ref, g_ref, gd_ref, w_ref, o_ref):
    w = MIXER_WIDTH
    parts = [_rms(ya_ref[...], g_ref[:, 0:w], w),
             _rms(yb_ref[...], g_ref[:, w:2 * w], w),
             _rms(yc_ref[...], g_ref[:, 2 * w:3 * w], w),
             _rms(yd_ref[...], gd_ref[...], w)]
    y = jnp.concatenate(parts, axis=1).astype(BF16)
    o_ref[...] = x_ref[...] + jnp.dot(y, w_ref[...], preferred_element_type=F32)


def _merge(ya, yb, yc, yd, x, g, gd, w, layer):
    row = lambda i: (i, 0)
    k_dim = 3 * MIXER_WIDTH + FOX_HEADS * HEAD_PAD
    mix = pl.BlockSpec((TM_MERGE, MIXER_WIDTH), row)
    return pl.pallas_call(
        _merge_kernel,
        grid=(N_TOK // TM_MERGE,),
        in_specs=[mix, mix, mix,
                  pl.BlockSpec((TM_MERGE, FOX_HEADS * HEAD_PAD), row),
                  pl.BlockSpec((TM_MERGE, D_MODEL), row),
                  _layer_block((1, 3 * MIXER_WIDTH), layer),
                  _layer_block((1, FOX_HEADS * HEAD_PAD), layer),
                  _layer_block((k_dim, D_MODEL), layer)],
        out_specs=pl.BlockSpec((TM_MERGE, D_MODEL), row),
        out_shape=jax.ShapeDtypeStruct((N_TOK, D_MODEL), F32),
        compiler_params=_params(1),
        name="merge_out_proj",
    )(ya, yb, yc, yd, x, g, gd, w)


def _mlp_kernel(x_ref, g_ref, w1_ref, w2_ref, fg_ref, o_ref, h_ref, *, final_norm):
    j = pl.program_id(1)

    @pl.when(j == 0)
    def _():
        x = x_ref[...]
        h_ref[...] = _rms(x, g_ref[...], D_MODEL).astype(BF16)
        o_ref[...] = x

    a = jnp.dot(h_ref[...], w1_ref[...], preferred_element_type=F32)
    a = jnp.square(jnp.maximum(a, 0.0)).astype(BF16)
    o_ref[...] += jnp.dot(a, w2_ref[...], preferred_element_type=F32)

    if final_norm:
        @pl.when(j == pl.num_programs(1) - 1)
        def _():
            o_ref[...] = _rms(o_ref[...], fg_ref[...], D_MODEL)


def _mlp(x, g, w1, w2, fg, layer, final_norm):
    return pl.pallas_call(
        functools.partial(_mlp_kernel, final_norm=final_norm),
        grid=(N_TOK // TM_MLP, D_FF // TF_MLP),
        in_specs=[pl.BlockSpec((TM_MLP, D_MODEL), lambda i, j: (i, 0)),
                  _layer_block((1, D_MODEL), layer),
                  pl.BlockSpec((None, D_MODEL, TF_MLP), lambda i, j: (layer, 0, j)),
                  pl.BlockSpec((None, TF_MLP, D_MODEL), lambda i, j: (layer, j, 0)),
                  pl.BlockSpec((1, D_MODEL), lambda i, j: (0, 0))],
        out_specs=pl.BlockSpec((TM_MLP, D_MODEL), lambda i, j: (i, 0)),
        out_shape=jax.ShapeDtypeStruct((N_TOK, D_MODEL), F32),
        scratch_shapes=[pltpu.VMEM((TM_MLP, D_MODEL), BF16)],
        compiler_params=_params(2),
        name="mlp",
    )(x, g, w1, w2, fg)


def _pad_heads(w):
    lead = w.shape[:-1]
    w = w.reshape(*lead, FOX_HEADS, FOX_HEAD_DIM)
    w = jnp.pad(w, [(0, 0)] * (len(lead) + 1) + [(0, HEAD_PAD - FOX_HEAD_DIM)])
    return w.reshape(*lead, FOX_HEADS * HEAD_PAD)


def _block_diag(blocks):
    n, g, r, c = blocks.shape
    eye = jnp.eye(g, dtype=blocks.dtype)
    return jnp.einsum('ngrc,gh->ngrhc', blocks, eye).reshape(n, g * r, g * c)


def _s5_discretize(lam_re, lam_im, log_dt, b_re, b_im):
    dt = jnp.exp(log_dt)[..., None]
    mag = jnp.exp(lam_re * dt)
    abar_re = mag * jnp.cos(lam_im * dt)
    abar_im = mag * jnp.sin(lam_im * dt)
    denom = jnp.square(lam_re) + jnp.square(lam_im)
    num_re = abar_re - 1.0
    num_im = abar_im
    fac_re = (num_re * lam_re + num_im * lam_im) / denom
    fac_im = (num_im * lam_re - num_re * lam_im) / denom
    bbar_re = fac_re[..., None] * b_re - fac_im[..., None] * b_im
    bbar_im = fac_re[..., None] * b_im + fac_im[..., None] * b_re
    return abar_re, abar_im, bbar_re, bbar_im


def kernel(x, norm1_g, w_in, sgu_norm_g, sgu_w, sgu_b, s5_lambda_re, s5_lambda_im, s5_log_dt, s5_b_re, s5_b_im, s5_c_re, s5_c_im, s5_d, s5_glu_w, s5_glu_b, lru_conv_w, lru_conv_b, lru_wa, lru_ba, lru_wx, lru_bx, lru_lambda, fox_fgate_b, mix_norm_g, w_out, norm2_g, w_mlp_in, w_mlp_out, final_g):
    w = MIXER_WIDTH
    row = lambda v: v.reshape(DEPTH, 1, -1)

    wmix = jnp.concatenate([w_in[..., 0:2 * w], w_in[..., 3 * w:5 * w], w_in[..., 2 * w:3 * w]], axis=-1).astype(BF16)
    wqkv = jnp.concatenate([_pad_heads(w_in[..., (5 + m) * w:(6 + m) * w]) for m in range(3)], axis=-1).astype(BF16)
    wf = jnp.pad(w_in[..., 8 * w:], ((0, 0), (0, 0), (0, LANES - FOX_HEADS))).astype(BF16)
    bf = row(jnp.pad(fox_fgate_b, ((0, 0), (0, LANES - FOX_HEADS))))

    sgu_wcat = jnp.transpose(sgu_w, (0, 2, 1, 3)).reshape(DEPTH, SGU_CHUNK, SGU_HEADS * SGU_CHUNK)
    sgu_bias = jnp.repeat(jnp.transpose(sgu_b, (0, 2, 1)), w // SGU_HEADS, axis=2)

    abar_re, abar_im, bbar_re, bbar_im = _s5_discretize(s5_lambda_re, s5_lambda_im, s5_log_dt, s5_b_re, s5_b_im)
    swap = lambda t: jnp.transpose(t, (0, 1, 3, 2))
    bbd = jnp.concatenate([_block_diag(swap(bbar_re)), _block_diag(swap(bbar_im))], axis=2).astype(BF16)
    cbd = jnp.concatenate([_block_diag(swap(s5_c_re)), -_block_diag(swap(s5_c_im))], axis=1).astype(BF16)
    lam = jnp.concatenate([abar_re.reshape(DEPTH, SUBLANES, LANES), abar_im.reshape(DEPTH, SUBLANES, LANES)], axis=1)
    glu_w = s5_glu_w.astype(BF16)

    wa_bd = _block_diag(lru_wa).astype(BF16)
    wx_bd = _block_diag(lru_wx).astype(BF16)

    w_o_pad = jnp.concatenate(
        [w_out[:, 0:3 * w],
         jnp.pad(w_out[:, 3 * w:].reshape(DEPTH, FOX_HEADS, FOX_HEAD_DIM, D_MODEL),
                 ((0, 0), (0, 0), (0, HEAD_PAD - FOX_HEAD_DIM), (0, 0))).reshape(DEPTH, FOX_HEADS * HEAD_PAD, D_MODEL)],
        axis=1).astype(BF16)
    g_mix = row(mix_norm_g[:, 0:3 * w])
    g_mix_d = _pad_heads(row(mix_norm_g[:, 3 * w:]))
    w1 = w_mlp_in.astype(BF16)
    w2 = w_mlp_out.astype(BF16)
    g1, g2 = row(norm1_g), row(norm2_g)
    fg = final_g.reshape(1, D_MODEL)

    xf = x.reshape(N_TOK, D_MODEL)
    for l in range(DEPTH):
        zmix, qkv, zf = _in_proj(xf, g1, wmix, wqkv, wf, l)
        zmix3 = zmix.reshape(BATCH, SEQ, ZMIX_COLS)
        qadd, kadd = _fcum(zf, bf, l)
        y_a = _sgu(zmix, row(sgu_norm_g), sgu_wcat, sgu_bias, l)
        y_b = _s5(zmix3, bbd, lam, cbd, row(s5_d), glu_w, row(s5_glu_b), l)
        y_c = _lru(zmix3, lru_conv_w, row(lru_conv_b), wa_bd, row(lru_ba.reshape(DEPTH, w)),
                   wx_bd, row(lru_bx.reshape(DEPTH, w)), row(lru_lambda), l)
        y_d = _attn(qkv, qadd, kadd)
        xf = _merge(y_a, y_b.reshape(N_TOK, w), y_c.reshape(N_TOK, w), y_d, xf, g_mix, g_mix_d, w_o_pad, l)
        xf = _mlp(xf, g2, w1, w2, fg, l, final_norm=(l == DEPTH - 1))
    return xf.reshape(BATCH, SEQ, D_MODEL)
```

```python
import functools

import jax
import jax.numpy as jnp
import numpy as np
from jax import lax
from jax.experimental import pallas as pl
from jax.experimental.pallas import tpu as pltpu

D_MODEL = 1024
BATCH = 4
SEQ = 4096
DEPTH = 4
N_TOK = BATCH * SEQ
MIXER_WIDTH = 256
SGU_HEADS = 4
SGU_CHUNK = 128
S5_GROUP = 16
S5_GROUPS = 16
S5_STATE = 64
LRU_HEADS = 4
LRU_CONV = 4
LRU_C = 8.0
FOX_HEADS = 4
FOX_HEAD_DIM = 64
D_FF = 4 * D_MODEL
RMS_EPS = 1e-6
LOG2E = 1.4426950408889634

LANES = 128
SUBLANES = 8
HEAD_PAD = LANES
ZMIX_COLS = 5 * MIXER_WIDTH
W_IN_COLS = 8 * MIXER_WIDTH + LANES
QKV_COLS = 3 * FOX_HEADS * HEAD_PAD
S5_NSTATE = S5_GROUPS * S5_STATE
S5_ROWS = 2 * S5_NSTATE // LANES
S5_PITCH = S5_ROWS + SUBLANES

TM_IN = 512
TM_MERGE = 512
TM_MLP = 1024
TF_MLP = 1024
T_SGU = 2048
T_SCAN = 256
T_CUM = 512
TQ = 512
TK = 512
ATTN_HEADS_PER_STEP = 4
VMEM_LIMIT = 48 * 1024 * 1024

F32 = jnp.float32
BF16 = jnp.bfloat16
MIX_OUT = BF16


def _params(n_axes):
    return pltpu.CompilerParams(dimension_semantics=("arbitrary",) * n_axes,
                                vmem_limit_bytes=VMEM_LIMIT)


def _layer_block(shape, layer):
    zeros = (0,) * len(shape)
    return pl.BlockSpec((None,) + tuple(shape), lambda *_: (layer,) + zeros)


def _rms(x, g, width):
    ms = jnp.sum(jnp.square(x), axis=-1, keepdims=True) * (1.0 / width)
    return x * lax.rsqrt(ms + RMS_EPS) * g


def _softplus(x):
    return jnp.maximum(x, 0.0) + jnp.log1p(jnp.exp(-jnp.abs(x)))


def _in_proj_kernel(x_ref, g_ref, w_ref, zmix_ref, qkv_ref, zf_ref):
    h = _rms(x_ref[...], g_ref[...], D_MODEL).astype(BF16)
    zmix_ref[...] = jnp.dot(h, w_ref[:, 0:ZMIX_COLS], preferred_element_type=F32)
    qkv_lo = ZMIX_COLS
    qkv_hi = ZMIX_COLS + 3 * MIXER_WIDTH
    zqkv = jnp.dot(h, w_ref[:, qkv_lo:qkv_hi], preferred_element_type=F32)
    pad = jnp.zeros((TM_IN, HEAD_PAD - FOX_HEAD_DIM), F32)
    for slot in range(3 * FOX_HEADS):
        head = zqkv[:, slot * FOX_HEAD_DIM:(slot + 1) * FOX_HEAD_DIM]
        qkv_ref[:, slot * HEAD_PAD:(slot + 1) * HEAD_PAD] = jnp.concatenate([head, pad], axis=1).astype(BF16)
    zf_ref[...] = jnp.dot(h, w_ref[:, qkv_hi:W_IN_COLS], preferred_element_type=F32)


def _in_proj(x, g, w, layer):
    row = lambda i: (i, 0)
    return pl.pallas_call(
        _in_proj_kernel,
        grid=(N_TOK // TM_IN,),
        in_specs=[pl.BlockSpec((TM_IN, D_MODEL), row),
                  _layer_block((1, D_MODEL), layer),
                  _layer_block((D_MODEL, W_IN_COLS), layer)],
        out_specs=[pl.BlockSpec((TM_IN, ZMIX_COLS), row),
                   pl.BlockSpec((TM_IN, QKV_COLS), row),
                   pl.BlockSpec((TM_IN, LANES), row)],
        out_shape=[jax.ShapeDtypeStruct((N_TOK, ZMIX_COLS), F32),
                   jax.ShapeDtypeStruct((N_TOK, QKV_COLS), BF16),
                   jax.ShapeDtypeStruct((N_TOK, LANES), F32)],
        compiler_params=_params(1),
        name="in_proj",
    )(x, g, w)


def _split3(x):
    hi = x.astype(BF16)
    r1 = x - hi.astype(F32)
    mid = r1.astype(BF16)
    lo = (r1 - mid.astype(F32)).astype(BF16)
    return hi, mid, lo


def _bias_placement():
    pq = np.zeros((3 * LANES, FOX_HEADS * HEAD_PAD), np.float32)
    pk = np.zeros_like(pq)
    ones_q = np.zeros((1, FOX_HEADS * HEAD_PAD), np.float32)
    ones_k = np.zeros_like(ones_q)
    for h in range(FOX_HEADS):
        for piece in range(3):
            pq[piece * LANES + h, h * HEAD_PAD + FOX_HEAD_DIM + piece] = 1.0
            pk[piece * LANES + h, h * HEAD_PAD + FOX_HEAD_DIM + 3 + piece] = -1.0
            ones_q[0, h * HEAD_PAD + FOX_HEAD_DIM + 3 + piece] = 1.0
            ones_k[0, h * HEAD_PAD + FOX_HEAD_DIM + piece] = 1.0
    return (jnp.asarray(pq, BF16), jnp.asarray(pk, BF16), jnp.asarray(ones_q), jnp.asarray(ones_k))


def _fcum_kernel(zf_ref, bf_ref, tril_ref, pq_ref, pk_ref, oq_ref, ok_ref, qadd_ref, kadd_ref, carry_ref):
    @pl.when(pl.program_id(1) == 0)
    def _():
        carry_ref[...] = jnp.zeros_like(carry_ref)

    logit = zf_ref[...] + bf_ref[...]
    log_f = -_softplus(-logit)
    tril = tril_ref[...]
    hi, mid, lo = _split3(log_f)
    cs = (jnp.dot(tril, hi, preferred_element_type=F32)
          + jnp.dot(tril, mid, preferred_element_type=F32)
          + jnp.dot(tril, lo, preferred_element_type=F32))
    cum = cs + carry_ref[...]
    carry_ref[...] = cum[T_CUM - 1:T_CUM, :]
    pieces = jnp.concatenate(_split3(cum * LOG2E), axis=1)
    qadd_ref[...] = (jnp.dot(pieces, pq_ref[...], preferred_element_type=F32) + oq_ref[...]).astype(BF16)
    kadd_ref[...] = (jnp.dot(pieces, pk_ref[...], preferred_element_type=F32) + ok_ref[...]).astype(BF16)


def _fcum(zf, bf, layer):
    n_c = SEQ // T_CUM
    pq, pk, ones_q, ones_k = _bias_placement()
    tril = jnp.asarray(np.tril(np.ones((T_CUM, T_CUM), np.float32)), BF16)
    full = lambda b, c: (0, 0)
    wide = FOX_HEADS * HEAD_PAD
    return pl.pallas_call(
        _fcum_kernel,
        grid=(BATCH, n_c),
        in_specs=[pl.BlockSpec((T_CUM, LANES), lambda b, c: (b * n_c + c, 0)),
                  _layer_block((1, LANES), layer),
                  pl.BlockSpec((T_CUM, T_CUM), full),
                  pl.BlockSpec((3 * LANES, wide), full),
                  pl.BlockSpec((3 * LANES, wide), full),
                  pl.BlockSpec((1, wide), full),
                  pl.BlockSpec((1, wide), full)],
        out_specs=[pl.BlockSpec((T_CUM, wide), lambda b, c: (b * n_c + c, 0)),
                   pl.BlockSpec((T_CUM, wide), lambda b, c: (b * n_c + c, 0))],
        out_shape=[jax.ShapeDtypeStruct((N_TOK, wide), BF16),
                   jax.ShapeDtypeStruct((N_TOK, wide), BF16)],
        scratch_shapes=[pltpu.VMEM((1, LANES), F32)],
        compiler_params=_params(2),
        name="forget_cumsum",
    )(zf, bf, tril, pq, pk, ones_q, ones_k)


def _sgu_kernel(zu_ref, zv_ref, g_ref, w_ref, b_ref, o_ref):
    u = jax.nn.gelu(zu_ref[...])
    v = _rms(jax.nn.gelu(zv_ref[...]), g_ref[...], MIXER_WIDTH)
    hd = MIXER_WIDTH // SGU_HEADS
    lane_head = lax.broadcasted_iota(jnp.int32, (SGU_CHUNK, MIXER_WIDTH), 1) // hd
    t_idx = lax.broadcasted_iota(jnp.int32, (SGU_CHUNK, SGU_HEADS * SGU_CHUNK), 0)
    s_idx = lax.broadcasted_iota(jnp.int32, (SGU_CHUNK, SGU_HEADS * SGU_CHUNK), 1) % SGU_CHUNK
    wm = jnp.where(s_idx <= t_idx, w_ref[...], 0.0).astype(BF16)
    bias = b_ref[...]
    for c in range(T_SGU // SGU_CHUNK):
        rows = slice(c * SGU_CHUNK, (c + 1) * SGU_CHUNK)
        vc = v[rows]
        vstack = jnp.concatenate(
            [jnp.where(lane_head == h, vc, 0.0) for h in range(SGU_HEADS)], axis=0).astype(BF16)
        mixed = jnp.dot(wm, vstack, preferred_element_type=F32) + bias
        o_ref[rows, :] = (u[rows] * mixed).astype(MIX_OUT)


def _sgu(zmix, g, w, b, layer):
    return pl.pallas_call(
        _sgu_kernel,
        grid=(N_TOK // T_SGU,),
        in_specs=[pl.BlockSpec((T_SGU, MIXER_WIDTH), lambda i: (i, 0)),
                  pl.BlockSpec((T_SGU, MIXER_WIDTH), lambda i: (i, 1)),
                  _layer_block((1, MIXER_WIDTH), layer),
                  _layer_block((SGU_CHUNK, SGU_HEADS * SGU_CHUNK), layer),
                  _layer_block((SGU_CHUNK, MIXER_WIDTH), layer)],
        out_specs=pl.BlockSpec((T_SGU, MIXER_WIDTH), lambda i: (i, 0)),
        out_shape=jax.ShapeDtypeStruct((N_TOK, MIXER_WIDTH), MIX_OUT),
        compiler_params=_params(1),
        name="sgu",
    )(zmix, zmix, g, w, b)


def _s5_kernel(u_ref, bbd_ref, lam_ref, cbd_ref, d_ref, gw_ref, gb_ref, o_ref, s_ref, h_ref):
    @pl.when(pl.program_id(0) == 0)
    def _():
        h_ref[...] = jnp.zeros_like(h_ref)

    for b in range(BATCH):
        bu = jnp.dot(u_ref[b].astype(BF16), bbd_ref[...], preferred_element_type=F32)
        for j in range(S5_ROWS):
            s_ref[b, pl.ds(j, T_SCAN, stride=S5_PITCH), :] = bu[:, j * LANES:(j + 1) * LANES]

    lam_re = lam_ref[0:SUBLANES, :]
    lam_im = lam_ref[SUBLANES:S5_ROWS, :]

    def step(t, carry):
        base = pl.multiple_of(t * S5_PITCH, SUBLANES)
        new = []
        for b in range(BATCH):
            h_re, h_im = carry[2 * b], carry[2 * b + 1]
            n_re = lam_re * h_re - lam_im * h_im + s_ref[b, pl.ds(base, SUBLANES), :]
            n_im = lam_re * h_im + lam_im * h_re + s_ref[b, pl.ds(base + SUBLANES, SUBLANES), :]
            s_ref[b, pl.ds(base, SUBLANES), :] = n_re
            s_ref[b, pl.ds(base + SUBLANES, SUBLANES), :] = n_im
            new += [n_re, n_im]
        return tuple(new)

    init = []
    for b in range(BATCH):
        init += [h_ref[b, 0:SUBLANES, :], h_ref[b, SUBLANES:S5_ROWS, :]]
    fin = lax.fori_loop(0, T_SCAN, step, tuple(init), unroll=8)
    for b in range(BATCH):
        h_ref[b, 0:SUBLANES, :] = fin[2 * b]
        h_ref[b, SUBLANES:S5_ROWS, :] = fin[2 * b + 1]

    for b in range(BATCH):
        states = jnp.concatenate(
            [s_ref[b, pl.ds(j, T_SCAN, stride=S5_PITCH), :] for j in range(S5_ROWS)], axis=1)
        u = u_ref[b]
        y = jnp.dot(states.astype(BF16), cbd_ref[...], preferred_element_type=F32) + d_ref[...] * u
        y = jax.nn.gelu(y)
        gate = jnp.dot(y.astype(BF16), gw_ref[...], preferred_element_type=F32) + gb_ref[...]
        o_ref[b] = (y * jax.nn.sigmoid(gate)).astype(MIX_OUT)


def _s5(zmix3, bbd, lam, cbd, d, gw, gb, layer):
    return pl.pallas_call(
        _s5_kernel,
        grid=(SEQ // T_SCAN,),
        in_specs=[pl.BlockSpec((BATCH, T_SCAN, MIXER_WIDTH), lambda c: (0, c, 2)),
                  _layer_block((MIXER_WIDTH, 2 * S5_NSTATE), layer),
                  _layer_block((S5_ROWS, LANES), layer),
                  _layer_block((2 * S5_NSTATE, MIXER_WIDTH), layer),
                  _layer_block((1, MIXER_WIDTH), layer),
                  _layer_block((MIXER_WIDTH, MIXER_WIDTH), layer),
                  _layer_block((1, MIXER_WIDTH), layer)],
        out_specs=pl.BlockSpec((BATCH, T_SCAN, MIXER_WIDTH), lambda c: (0, c, 0)),
        out_shape=jax.ShapeDtypeStruct((BATCH, SEQ, MIXER_WIDTH), MIX_OUT),
        scratch_shapes=[pltpu.VMEM((BATCH, T_SCAN * S5_PITCH, LANES), F32),
                        pltpu.VMEM((BATCH, S5_ROWS, LANES), F32)],
        compiler_params=_params(1),
        name="s5",
    )(zmix3, bbd, lam, cbd, d, gw, gb)


LRU_SLOTS = BATCH * MIXER_WIDTH // LANES


def _lru_kernel(x_ref, gate_ref, cw_ref, cb_ref, wa_ref, ba_ref, wx_ref, bx_ref, lam_ref, o_ref,
                tail_ref, a_ref, b_ref, h_ref):
    @pl.when(pl.program_id(0) == 0)
    def _():
        tail_ref[...] = jnp.zeros_like(tail_ref)
        h_ref[...] = jnp.zeros_like(h_ref)

    n_tiles = MIXER_WIDTH // LANES
    decay_rate = LRU_C * _softplus(-lam_ref[...])
    for b in range(BATCH):
        x = x_ref[b]
        xp = jnp.concatenate([tail_ref[b], x], axis=0)
        tail_ref[b] = x[T_SCAN - SUBLANES:T_SCAN, :]
        xc = cb_ref[...]
        for k in range(LRU_CONV):
            off = SUBLANES - (LRU_CONV - 1) + k
            xc = xc + cw_ref[k:k + 1, :] * xp[off:off + T_SCAN, :]
        xcb = xc.astype(BF16)
        r = jax.nn.sigmoid(jnp.dot(xcb, wa_ref[...], preferred_element_type=F32) + ba_ref[...])
        i = jax.nn.sigmoid(jnp.dot(xcb, wx_ref[...], preferred_element_type=F32) + bx_ref[...])
        log_a = -(r * decay_rate)
        a = jnp.exp(log_a)
        inp = jnp.sqrt(-jnp.tanh(log_a) * (a * a + 1.0)) * (i * xc)
        for j in range(n_tiles):
            slot = b * n_tiles + j
            a_ref[pl.ds(slot, T_SCAN, stride=LRU_SLOTS), :] = a[:, j * LANES:(j + 1) * LANES]
            b_ref[pl.ds(slot, T_SCAN, stride=LRU_SLOTS), :] = inp[:, j * LANES:(j + 1) * LANES]

    def step(t, h):
        base = pl.multiple_of(t * LRU_SLOTS, LRU_SLOTS)
        h = a_ref[pl.ds(base, LRU_SLOTS), :] * h + b_ref[pl.ds(base, LRU_SLOTS), :]
        b_ref[pl.ds(base, LRU_SLOTS), :] = h
        return h

    h_ref[...] = lax.fori_loop(0, T_SCAN, step, h_ref[...], unroll=8)

    for b in range(BATCH):
        h = jnp.concatenate(
            [b_ref[pl.ds(b * n_tiles + j, T_SCAN, stride=LRU_SLOTS), :] for j in range(n_tiles)],
            axis=1)
        o_ref[b] = (h * jax.nn.gelu(gate_ref[b])).astype(MIX_OUT)


def _lru(zmix3, cw, cb, wa, ba, wx, bx, lam, layer):
    vec = _layer_block((1, MIXER_WIDTH), layer)
    mat = _layer_block((MIXER_WIDTH, MIXER_WIDTH), layer)
    return pl.pallas_call(
        _lru_kernel,
        grid=(SEQ // T_SCAN,),
        in_specs=[pl.BlockSpec((BATCH, T_SCAN, MIXER_WIDTH), lambda c: (0, c, 3)),
                  pl.BlockSpec((BATCH, T_SCAN, MIXER_WIDTH), lambda c: (0, c, 4)),
                  _layer_block((LRU_CONV, MIXER_WIDTH), layer), vec, mat, vec, mat, vec, vec],
        out_specs=pl.BlockSpec((BATCH, T_SCAN, MIXER_WIDTH), lambda c: (0, c, 0)),
        out_shape=jax.ShapeDtypeStruct((BATCH, SEQ, MIXER_WIDTH), MIX_OUT),
        scratch_shapes=[pltpu.VMEM((BATCH, SUBLANES, MIXER_WIDTH), F32),
                        pltpu.VMEM((T_SCAN * LRU_SLOTS, LANES), F32),
                        pltpu.VMEM((T_SCAN * LRU_SLOTS, LANES), F32),
                        pltpu.VMEM((LRU_SLOTS, LANES), F32)],
        compiler_params=_params(1),
        name="rglru",
    )(zmix3, zmix3, cw, cb, wa, ba, wx, bx, lam)


def _attn_kernel(q_ref, qadd_ref, k_ref, kadd_ref, v_ref, o_ref, s_ref, m_ref, acc_ref):
    i = pl.program_id(2)
    scale = FOX_HEAD_DIM ** -0.5 * LOG2E
    neg = jnp.finfo(F32).min
    denom_lane = lax.broadcasted_iota(jnp.int32, (TK, HEAD_PAD), 1) == FOX_HEAD_DIM
    slots = [slice(n * HEAD_PAD, (n + 1) * HEAD_PAD) for n in range(ATTN_HEADS_PER_STEP)]
    qs = [(q_ref[:, sl].astype(F32) * scale + qadd_ref[:, sl].astype(F32)).astype(BF16) for sl in slots]

    def logits(n, j):
        start = pl.multiple_of(j * TK, TK)
        ks = k_ref[pl.ds(start, TK), slots[n]] + kadd_ref[pl.ds(start, TK), slots[n]]
        return lax.dot_general(qs[n], ks, (((1,), (1,)), ((), ())), preferred_element_type=F32)

    def block(j, masked, prefetch):
        start = pl.multiple_of(j * TK, TK)
        for n, sl in enumerate(slots):
            s = s_ref[n]
            vs = jnp.where(denom_lane, 1.0, v_ref[pl.ds(start, TK), sl]).astype(BF16)
            if masked:
                rr = lax.broadcasted_iota(jnp.int32, (TQ, TK), 0)
                cc = lax.broadcasted_iota(jnp.int32, (TQ, TK), 1)
                s = jnp.where(cc <= rr, s, neg)
            m = m_ref[n]
            m_new = jnp.maximum(m, jnp.max(s, axis=1, keepdims=True))
            alpha = jnp.exp2(m - m_new)
            p = jnp.exp2(s - jnp.concatenate([m_new] * (TK // LANES), axis=1))
            m_ref[n] = m_new
            acc_ref[n] = alpha * acc_ref[n] + jnp.dot(p.astype(BF16), vs, preferred_element_type=F32)
            if prefetch:
                s_ref[n] = logits(n, j + 1)

    for n in range(ATTN_HEADS_PER_STEP):
        s_ref[n] = logits(n, 0)
    m_ref[...] = jnp.full(m_ref.shape, neg, F32)
    acc_ref[...] = jnp.zeros_like(acc_ref)

    @pl.loop(0, i // 2)
    def _(jj):
        block(2 * jj, False, True)
        block(2 * jj + 1, False, True)

    @pl.when(i % 2 == 1)
    def _():
        block(i - 1, False, True)

    block(i, True, False)
    out_lane = lax.broadcasted_iota(jnp.int32, (TQ, HEAD_PAD), 1) < FOX_HEAD_DIM
    for n, sl in enumerate(slots):
        acc = acc_ref[n]
        o_ref[:, sl] = jnp.where(out_lane, acc / acc[:, FOX_HEAD_DIM:FOX_HEAD_DIM + 1], 0.0).astype(MIX_OUT)


def _attn(qkv, qadd, kadd):
    n_q = SEQ // TQ
    n_hg = FOX_HEADS // ATTN_HEADS_PER_STEP
    width = ATTN_HEADS_PER_STEP * HEAD_PAD
    qrow = lambda b, h, i: (b * n_q + i, h)
    return pl.pallas_call(
        _attn_kernel,
        grid=(BATCH, n_hg, n_q),
        in_specs=[pl.BlockSpec((TQ, width), qrow),
                  pl.BlockSpec((TQ, width), qrow),
                  pl.BlockSpec((SEQ, width), lambda b, h, i: (b, n_hg + h)),
                  pl.BlockSpec((SEQ, width), lambda b, h, i: (b, h)),
                  pl.BlockSpec((SEQ, width), lambda b, h, i: (b, 2 * n_hg + h))],
        out_specs=pl.BlockSpec((TQ, width), qrow),
        out_shape=jax.ShapeDtypeStruct((N_TOK, FOX_HEADS * HEAD_PAD), MIX_OUT),
        scratch_shapes=[pltpu.VMEM((ATTN_HEADS_PER_STEP, TQ, TK), F32),
                        pltpu.VMEM((ATTN_HEADS_PER_STEP, TQ, LANES), F32),
                        pltpu.VMEM((ATTN_HEADS_PER_STEP, TQ, HEAD_PAD), F32)],
        compiler_params=_params(3),
        name="fox_attention",
    )(qkv, qadd, qkv, kadd, qkv)


def _merge_kernel(ya_ref, yb_ref, yc_ref, yd_ref, x_ref, g_ref, gd_ref, w_ref, o_ref):
    w = MIXER_WIDTH
    parts = [_rms(ya_ref[...].astype(F32), g_ref[:, 0:w], w),
             _rms(yb_ref[...].astype(F32), g_ref[:, w:2 * w], w),
             _rms(yc_ref[...].astype(F32), g_ref[:, 2 * w:3 * w], w),
             _rms(yd_ref[...].astype(F32), gd_ref[...], w)]
    y = jnp.concatenate(parts, axis=1).astype(BF16)
    o_ref[...] = x_ref[...] + jnp.dot(y, w_ref[...], preferred_element_type=F32)


def _merge(ya, yb, yc, yd, x, g, gd, w, layer):
    row = lambda i: (i, 0)
    k_dim = 3 * MIXER_WIDTH + FOX_HEADS * HEAD_PAD
    mix = pl.BlockSpec((TM_MERGE, MIXER_WIDTH), row)
    return pl.pallas_call(
        _merge_kernel,
        grid=(N_TOK // TM_MERGE,),
        in_specs=[mix, mix, mix,
                  pl.BlockSpec((TM_MERGE, FOX_HEADS * HEAD_PAD), row),
                  pl.BlockSpec((TM_MERGE, D_MODEL), row),
                  _layer_block((1, 3 * MIXER_WIDTH), layer),
                  _layer_block((1, FOX_HEADS * HEAD_PAD), layer),
                  _layer_block((k_dim, D_MODEL), layer)],
        out_specs=pl.BlockSpec((TM_MERGE, D_MODEL), row),
        out_shape=jax.ShapeDtypeStruct((N_TOK, D_MODEL), F32),
        compiler_params=_params(1),
        name="merge_out_proj",
    )(ya, yb, yc, yd, x, g, gd, w)


def _mlp_kernel(x_ref, g_ref, w1_ref, w2_ref, fg_ref, o_ref, h_ref, *, final_norm):
    j = pl.program_id(1)

    @pl.when(j == 0)
    def _():
        x = x_ref[...]
        h_ref[...] = _rms(x, g_ref[...], D_MODEL).astype(BF16)
        o_ref[...] = x

    a = jnp.dot(h_ref[...], w1_ref[...], preferred_element_type=F32)
    a = jnp.square(jnp.maximum(a, 0.0)).astype(BF16)
    o_ref[...] += jnp.dot(a, w2_ref[...], preferred_element_type=F32)

    if final_norm:
        @pl.when(j == pl.num_programs(1) - 1)
        def _():
            o_ref[...] = _rms(o_ref[...], fg_ref[...], D_MODEL)


def _mlp(x, g, w1, w2, fg, layer, final_norm):
    return pl.pallas_call(
        functools.partial(_mlp_kernel, final_norm=final_norm),
        grid=(N_TOK // TM_MLP, D_FF // TF_MLP),
        in_specs=[pl.BlockSpec((TM_MLP, D_MODEL), lambda i, j: (i, 0)),
                  _layer_block((1, D_MODEL), layer),
                  pl.BlockSpec((None, D_MODEL, TF_MLP), lambda i, j: (layer, 0, j)),
                  pl.BlockSpec((None, TF_MLP, D_MODEL), lambda i, j: (layer, j, 0)),
                  pl.BlockSpec((1, D_MODEL), lambda i, j: (0, 0))],
        out_specs=pl.BlockSpec((TM_MLP, D_MODEL), lambda i, j: (i, 0)),
        out_shape=jax.ShapeDtypeStruct((N_TOK, D_MODEL), F32),
        scratch_shapes=[pltpu.VMEM((TM_MLP, D_MODEL), BF16)],
        compiler_params=_params(2),
        name="mlp",
    )(x, g, w1, w2, fg)


def _pad_heads(w):
    lead = w.shape[:-1]
    w = w.reshape(*lead, FOX_HEADS, FOX_HEAD_DIM)
    w = jnp.pad(w, [(0, 0)] * (len(lead) + 1) + [(0, HEAD_PAD - FOX_HEAD_DIM)])
    return w.reshape(*lead, FOX_HEADS * HEAD_PAD)


def _block_diag(blocks):
    n, g, r, c = blocks.shape
    eye = jnp.eye(g, dtype=blocks.dtype)
    return jnp.einsum('ngrc,gh->ngrhc', blocks, eye).reshape(n, g * r, g * c)


def _s5_discretize(lam_re, lam_im, log_dt, b_re, b_im):
    dt = jnp.exp(log_dt)[..., None]
    mag = jnp.exp(lam_re * dt)
    abar_re = mag * jnp.cos(lam_im * dt)
    abar_im = mag * jnp.sin(lam_im * dt)
    denom = jnp.square(lam_re) + jnp.square(lam_im)
    num_re = abar_re - 1.0
    num_im = abar_im
    fac_re = (num_re * lam_re + num_im * lam_im) / denom
    fac_im = (num_im * lam_re - num_re * lam_im) / denom
    bbar_re = fac_re[..., None] * b_re - fac_im[..., None] * b_im
    bbar_im = fac_re[..., None] * b_im + fac_im[..., None] * b_re
    return abar_re, abar_im, bbar_re, bbar_im


def kernel(x, norm1_g, w_in, sgu_norm_g, sgu_w, sgu_b, s5_lambda_re, s5_lambda_im, s5_log_dt, s5_b_re, s5_b_im, s5_c_re, s5_c_im, s5_d, s5_glu_w, s5_glu_b, lru_conv_w, lru_conv_b, lru_wa, lru_ba, lru_wx, lru_bx, lru_lambda, fox_fgate_b, mix_norm_g, w_out, norm2_g, w_mlp_in, w_mlp_out, final_g):
    w = MIXER_WIDTH
    row = lambda v: v.reshape(DEPTH, 1, -1)

    w_in_p = jnp.pad(w_in.astype(BF16), ((0, 0), (0, 0), (0, W_IN_COLS - w_in.shape[-1])))
    bf = row(jnp.pad(fox_fgate_b, ((0, 0), (0, LANES - FOX_HEADS))))

    sgu_wcat = jnp.transpose(sgu_w, (0, 2, 1, 3)).reshape(DEPTH, SGU_CHUNK, SGU_HEADS * SGU_CHUNK)
    sgu_bias = jnp.repeat(jnp.transpose(sgu_b, (0, 2, 1)), w // SGU_HEADS, axis=2)

    abar_re, abar_im, bbar_re, bbar_im = _s5_discretize(s5_lambda_re, s5_lambda_im, s5_log_dt, s5_b_re, s5_b_im)
    swap = lambda t: jnp.transpose(t, (0, 1, 3, 2))
    bbd = jnp.concatenate([_block_diag(swap(bbar_re)), _block_diag(swap(bbar_im))], axis=2).astype(BF16)
    cbd = jnp.concatenate([_block_diag(swap(s5_c_re)), -_block_diag(swap(s5_c_im))], axis=1).astype(BF16)
    lam = jnp.concatenate([abar_re.reshape(DEPTH, SUBLANES, LANES), abar_im.reshape(DEPTH, SUBLANES, LANES)], axis=1)
    glu_w = s5_glu_w.astype(BF16)

    wa_bd = _block_diag(lru_wa).astype(BF16)
    wx_bd = _block_diag(lru_wx).astype(BF16)

    w_o_pad = jnp.concatenate(
        [w_out[:, 0:3 * w],
         jnp.pad(w_out[:, 3 * w:].reshape(DEPTH, FOX_HEADS, FOX_HEAD_DIM, D_MODEL),
                 ((0, 0), (0, 0), (0, HEAD_PAD - FOX_HEAD_DIM), (0, 0))).reshape(DEPTH, FOX_HEADS * HEAD_PAD, D_MODEL)],
        axis=1).astype(BF16)
    g_mix = row(mix_norm_g[:, 0:3 * w])
    g_mix_d = _pad_heads(row(mix_norm_g[:, 3 * w:]))
    w1 = w_mlp_in.astype(BF16)
    w2 = w_mlp_out.astype(BF16)
    g1, g2 = row(norm1_g), row(norm2_g)
    fg = final_g.reshape(1, D_MODEL)

    xf = x.reshape(N_TOK, D_MODEL)
    for l in range(DEPTH):
        zmix, qkv, zf = _in_proj(xf, g1, w_in_p, l)
        zmix3 = zmix.reshape(BATCH, SEQ, ZMIX_COLS)
        qadd, kadd = _fcum(zf, bf, l)
        y_a = _sgu(zmix, row(sgu_norm_g), sgu_wcat, sgu_bias, l)
        y_b = _s5(zmix3, bbd, lam, cbd, row(s5_d), glu_w, row(s5_glu_b), l)
        y_c = _lru(zmix3, lru_conv_w, row(lru_conv_b), wa_bd, row(lru_ba.reshape(DEPTH, w)),
                   wx_bd, row(lru_bx.reshape(DEPTH, w)), row(lru_lambda), l)
        y_d = _attn(qkv, qadd, kadd)
        xf = _merge(y_a, y_b.reshape(N_TOK, w), y_c.reshape(N_TOK, w), y_d, xf, g_mix, g_mix_d, w_o_pad, l)
        xf = _mlp(xf, g2, w1, w2, fg, l, final_norm=(l == DEPTH - 1))
    return xf.reshape(BATCH, SEQ, D_MODEL)
```

```python
import functools

import jax
import jax.numpy as jnp
import numpy as np
from jax import lax
from jax.experimental import pallas as pl
from jax.experimental.pallas import tpu as pltpu

D_MODEL = 1024
BATCH = 4
SEQ = 4096
DEPTH = 4
N_TOK = BATCH * SEQ
MIXER_WIDTH = 256
SGU_HEADS = 4
SGU_CHUNK = 128
S5_GROUP = 16
S5_GROUPS = 16
S5_STATE = 64
LRU_HEADS = 4
LRU_CONV = 4
LRU_C = 8.0
FOX_HEADS = 4
FOX_HEAD_DIM = 64
D_FF = 4 * D_MODEL
RMS_EPS = 1e-6
LOG2E = 1.4426950408889634

LANES = 128
SUBLANES = 8
HEAD_PAD = LANES
ZMIX_COLS = 5 * MIXER_WIDTH
W_IN_COLS = 8 * MIXER_WIDTH + LANES
QKV_COLS = 3 * FOX_HEADS * HEAD_PAD
S5_NSTATE = S5_GROUPS * S5_STATE
S5_ROWS = 2 * S5_NSTATE // LANES
SCAN_ROW_GAP = 4
S5_PITCH = S5_ROWS + SCAN_ROW_GAP

TM_IN = 512
TM_MERGE = 512
TM_MLP = 1024
TF_MLP = 1024
T_SGU = 2048
T_SCAN = 256
T_CUM = 256
TQ = 512
TK = 512
ATTN_HEADS_PER_STEP = 4
VMEM_LIMIT = 48 * 1024 * 1024

F32 = jnp.float32
BF16 = jnp.bfloat16
MIX_OUT = BF16


def _params(n_axes):
    return pltpu.CompilerParams(dimension_semantics=("arbitrary",) * n_axes,
                                vmem_limit_bytes=VMEM_LIMIT)


def _layer_block(shape, layer):
    zeros = (0,) * len(shape)
    return pl.BlockSpec((None,) + tuple(shape), lambda *_: (layer,) + zeros)


def _rms(x, g, width):
    ms = jnp.sum(jnp.square(x), axis=-1, keepdims=True) * (1.0 / width)
    return x * lax.rsqrt(ms + RMS_EPS) * g


def _softplus(x):
    return jnp.maximum(x, 0.0) + jnp.log1p(jnp.exp(-jnp.abs(x)))


def _in_proj_kernel(x_ref, g_ref, w_ref, zmix_ref, qkv_ref, zf_ref):
    h = _rms(x_ref[...], g_ref[...], D_MODEL).astype(BF16)
    zmix_ref[...] = jnp.dot(h, w_ref[:, 0:ZMIX_COLS], preferred_element_type=F32)
    qkv_lo = ZMIX_COLS
    qkv_hi = ZMIX_COLS + 3 * MIXER_WIDTH
    zqkv = jnp.dot(h, w_ref[:, qkv_lo:qkv_hi], preferred_element_type=F32)
    pad = jnp.zeros((TM_IN, HEAD_PAD - FOX_HEAD_DIM), F32)
    for slot in range(3 * FOX_HEADS):
        head = zqkv[:, slot * FOX_HEAD_DIM:(slot + 1) * FOX_HEAD_DIM]
        qkv_ref[:, slot * HEAD_PAD:(slot + 1) * HEAD_PAD] = jnp.concatenate([head, pad], axis=1).astype(BF16)
    zf_ref[...] = jnp.dot(h, w_ref[:, qkv_hi:W_IN_COLS], preferred_element_type=F32)


def _in_proj(x, g, w, layer):
    row = lambda i: (i, 0)
    return pl.pallas_call(
        _in_proj_kernel,
        grid=(N_TOK // TM_IN,),
        in_specs=[pl.BlockSpec((TM_IN, D_MODEL), row),
                  _layer_block((1, D_MODEL), layer),
                  _layer_block((D_MODEL, W_IN_COLS), layer)],
        out_specs=[pl.BlockSpec((TM_IN, ZMIX_COLS), row),
                   pl.BlockSpec((TM_IN, QKV_COLS), row),
                   pl.BlockSpec((TM_IN, LANES), row)],
        out_shape=[jax.ShapeDtypeStruct((N_TOK, ZMIX_COLS), F32),
                   jax.ShapeDtypeStruct((N_TOK, QKV_COLS), BF16),
                   jax.ShapeDtypeStruct((N_TOK, LANES), F32)],
        compiler_params=_params(1),
        name="in_proj",
    )(x, g, w)


def _split3(x):
    hi = x.astype(BF16)
    r1 = x - hi.astype(F32)
    mid = r1.astype(BF16)
    lo = (r1 - mid.astype(F32)).astype(BF16)
    return hi, mid, lo


def _bias_placement():
    pq = np.zeros((LANES, FOX_HEADS * HEAD_PAD), np.float32)
    pk = np.zeros_like(pq)
    ones_q = np.zeros((1, FOX_HEADS * HEAD_PAD), np.float32)
    ones_k = np.zeros_like(ones_q)
    for h in range(FOX_HEADS):
        for piece in range(3):
            pq[piece * FOX_HEADS + h, h * HEAD_PAD + FOX_HEAD_DIM + piece] = 1.0
            pk[piece * FOX_HEADS + h, h * HEAD_PAD + FOX_HEAD_DIM + 3 + piece] = -1.0
            ones_q[0, h * HEAD_PAD + FOX_HEAD_DIM + 3 + piece] = 1.0
            ones_k[0, h * HEAD_PAD + FOX_HEAD_DIM + piece] = 1.0
    return (jnp.asarray(pq, BF16), jnp.asarray(pk, BF16), jnp.asarray(ones_q), jnp.asarray(ones_k))


def _fcum_kernel(zf_ref, bf_ref, tril_ref, pq_ref, pk_ref, oq_ref, ok_ref, qadd_ref, kadd_ref, carry_ref):
    @pl.when(pl.program_id(0) == 0)
    def _():
        carry_ref[...] = jnp.zeros_like(carry_ref)

    tril = tril_ref[...]
    lane = lax.broadcasted_iota(jnp.int32, (T_CUM, LANES), 1)
    for b in range(BATCH):
        logit = zf_ref[b] + bf_ref[...]
        log_f = -_softplus(-logit)
        hi, mid, lo = _split3(log_f)
        cs = (jnp.dot(tril, hi, preferred_element_type=F32)
              + jnp.dot(tril, mid, preferred_element_type=F32)
              + jnp.dot(tril, lo, preferred_element_type=F32))
        cum = cs + carry_ref[b]
        carry_ref[b] = cum[T_CUM - 1:T_CUM, :]
        hi, mid, lo = [p.astype(F32) for p in _split3(cum * LOG2E)]
        pieces = jnp.where(lane < FOX_HEADS, hi,
                           jnp.where(lane < 2 * FOX_HEADS, pltpu.roll(mid, FOX_HEADS, axis=1),
                                     jnp.where(lane < 3 * FOX_HEADS, pltpu.roll(lo, 2 * FOX_HEADS, axis=1), 0.0))
                           ).astype(BF16)
        qadd_ref[b] = (jnp.dot(pieces, pq_ref[...], preferred_element_type=F32) + oq_ref[...]).astype(BF16)
        kadd_ref[b] = (jnp.dot(pieces, pk_ref[...], preferred_element_type=F32) + ok_ref[...]).astype(BF16)


def _fcum(zf3, bf, layer):
    pq, pk, ones_q, ones_k = _bias_placement()
    tril = jnp.asarray(np.tril(np.ones((T_CUM, T_CUM), np.float32)), BF16)
    full = lambda c: (0, 0)
    wide = FOX_HEADS * HEAD_PAD
    chunk = lambda c: (0, c, 0)
    return pl.pallas_call(
        _fcum_kernel,
        grid=(SEQ // T_CUM,),
        in_specs=[pl.BlockSpec((BATCH, T_CUM, LANES), chunk),
                  _layer_block((1, LANES), layer),
                  pl.BlockSpec((T_CUM, T_CUM), full),
                  pl.BlockSpec((LANES, wide), full),
                  pl.BlockSpec((LANES, wide), full),
                  pl.BlockSpec((1, wide), full),
                  pl.BlockSpec((1, wide), full)],
        out_specs=[pl.BlockSpec((BATCH, T_CUM, wide), chunk),
                   pl.BlockSpec((BATCH, T_CUM, wide), chunk)],
        out_shape=[jax.ShapeDtypeStruct((BATCH, SEQ, wide), BF16),
                   jax.ShapeDtypeStruct((BATCH, SEQ, wide), BF16)],
        scratch_shapes=[pltpu.VMEM((BATCH, 1, LANES), F32)],
        compiler_params=_params(1),
        name="forget_cumsum",
    )(zf3, bf, tril, pq, pk, ones_q, ones_k)


def _sgu_kernel(zu_ref, zv_ref, g_ref, w_ref, b_ref, o_ref):
    u = jax.nn.gelu(zu_ref[...])
    v = _rms(jax.nn.gelu(zv_ref[...]), g_ref[...], MIXER_WIDTH)
    hd = MIXER_WIDTH // SGU_HEADS
    lane_head = lax.broadcasted_iota(jnp.int32, (SGU_CHUNK, MIXER_WIDTH), 1) // hd
    t_idx = lax.broadcasted_iota(jnp.int32, (SGU_CHUNK, SGU_HEADS * SGU_CHUNK), 0)
    s_idx = lax.broadcasted_iota(jnp.int32, (SGU_CHUNK, SGU_HEADS * SGU_CHUNK), 1) % SGU_CHUNK
    wm = jnp.where(s_idx <= t_idx, w_ref[...], 0.0).astype(BF16)
    bias = b_ref[...]
    for c in range(T_SGU // SGU_CHUNK):
        rows = slice(c * SGU_CHUNK, (c + 1) * SGU_CHUNK)
        vc = v[rows]
        vstack = jnp.concatenate(
            [jnp.where(lane_head == h, vc, 0.0) for h in range(SGU_HEADS)], axis=0).astype(BF16)
        mixed = jnp.dot(wm, vstack, preferred_element_type=F32) + bias
        o_ref[rows, :] = (u[rows] * mixed).astype(MIX_OUT)


def _sgu(zmix, g, w, b, layer):
    return pl.pallas_call(
        _sgu_kernel,
        grid=(N_TOK // T_SGU,),
        in_specs=[pl.BlockSpec((T_SGU, MIXER_WIDTH), lambda i: (i, 0)),
                  pl.BlockSpec((T_SGU, MIXER_WIDTH), lambda i: (i, 1)),
                  _layer_block((1, MIXER_WIDTH), layer),
                  _layer_block((SGU_CHUNK, SGU_HEADS * SGU_CHUNK), layer),
                  _layer_block((SGU_CHUNK, MIXER_WIDTH), layer)],
        out_specs=pl.BlockSpec((T_SGU, MIXER_WIDTH), lambda i: (i, 0)),
        out_shape=jax.ShapeDtypeStruct((N_TOK, MIXER_WIDTH), MIX_OUT),
        compiler_params=_params(1),
        name="sgu",
    )(zmix, zmix, g, w, b)


def _s5_kernel(u_ref, bbd_ref, lam_ref, cbd_ref, d_ref, gw_ref, gb_ref, o_ref, s_ref, h_ref):
    @pl.when(pl.program_id(0) == 0)
    def _():
        h_ref[...] = jnp.zeros_like(h_ref)

    for b in range(BATCH):
        bu = jnp.dot(u_ref[b].astype(BF16), bbd_ref[...], preferred_element_type=F32)
        for j in range(S5_ROWS):
            s_ref[b, pl.ds(j, T_SCAN, stride=S5_PITCH), :] = bu[:, j * LANES:(j + 1) * LANES]

    lam_re = lam_ref[0:SUBLANES, :]
    lam_im = lam_ref[SUBLANES:S5_ROWS, :]

    def step(t, carry):
        base = t * S5_PITCH
        new = []
        for b in range(BATCH):
            h_re, h_im = carry[2 * b], carry[2 * b + 1]
            n_re = lam_re * h_re - lam_im * h_im + s_ref[b, pl.ds(base, SUBLANES), :]
            n_im = lam_re * h_im + lam_im * h_re + s_ref[b, pl.ds(base + SUBLANES, SUBLANES), :]
            s_ref[b, pl.ds(base, SUBLANES), :] = n_re
            s_ref[b, pl.ds(base + SUBLANES, SUBLANES), :] = n_im
            new += [n_re, n_im]
        return tuple(new)

    init = []
    for b in range(BATCH):
        init += [h_ref[b, 0:SUBLANES, :], h_ref[b, SUBLANES:S5_ROWS, :]]
    fin = lax.fori_loop(0, T_SCAN, step, tuple(init), unroll=8)
    for b in range(BATCH):
        h_ref[b, 0:SUBLANES, :] = fin[2 * b]
        h_ref[b, SUBLANES:S5_ROWS, :] = fin[2 * b + 1]

    for b in range(BATCH):
        states = jnp.concatenate(
            [s_ref[b, pl.ds(j, T_SCAN, stride=S5_PITCH), :] for j in range(S5_ROWS)], axis=1)
        u = u_ref[b]
        y = jnp.dot(states.astype(BF16), cbd_ref[...], preferred_element_type=F32) + d_ref[...] * u
        y = jax.nn.gelu(y)
        gate = jnp.dot(y.astype(BF16), gw_ref[...], preferred_element_type=F32) + gb_ref[...]
        o_ref[b] = (y * jax.nn.sigmoid(gate)).astype(MIX_OUT)


def _s5(zmix3, bbd, lam, cbd, d, gw, gb, layer):
    return pl.pallas_call(
        _s5_kernel,
        grid=(SEQ // T_SCAN,),
        in_specs=[pl.BlockSpec((BATCH, T_SCAN, MIXER_WIDTH), lambda c: (0, c, 2)),
                  _layer_block((MIXER_WIDTH, 2 * S5_NSTATE), layer),
                  _layer_block((S5_ROWS, LANES), layer),
                  _layer_block((2 * S5_NSTATE, MIXER_WIDTH), layer),
                  _layer_block((1, MIXER_WIDTH), layer),
                  _layer_block((MIXER_WIDTH, MIXER_WIDTH), layer),
                  _layer_block((1, MIXER_WIDTH), layer)],
        out_specs=pl.BlockSpec((BATCH, T_SCAN, MIXER_WIDTH), lambda c: (0, c, 0)),
        out_shape=jax.ShapeDtypeStruct((BATCH, SEQ, MIXER_WIDTH), MIX_OUT),
        scratch_shapes=[pltpu.VMEM((BATCH, T_SCAN * S5_PITCH, LANES), F32),
                        pltpu.VMEM((BATCH, S5_ROWS, LANES), F32)],
        compiler_params=_params(1),
        name="s5",
    )(zmix3, bbd, lam, cbd, d, gw, gb)


LRU_SLOTS = BATCH * MIXER_WIDTH // LANES
LRU_PITCH = LRU_SLOTS + SCAN_ROW_GAP


def _lru_kernel(x_ref, gate_ref, cw_ref, cb_ref, wa_ref, ba_ref, wx_ref, bx_ref, lam_ref, o_ref,
                tail_ref, a_ref, b_ref, h_ref):
    @pl.when(pl.program_id(0) == 0)
    def _():
        tail_ref[...] = jnp.zeros_like(tail_ref)
        h_ref[...] = jnp.zeros_like(h_ref)

    n_tiles = MIXER_WIDTH // LANES
    decay_rate = LRU_C * _softplus(-lam_ref[...])
    for b in range(BATCH):
        x = x_ref[b]
        xp = jnp.concatenate([tail_ref[b], x], axis=0)
        tail_ref[b] = x[T_SCAN - SUBLANES:T_SCAN, :]
        xc = cb_ref[...]
        for k in range(LRU_CONV):
            off = SUBLANES - (LRU_CONV - 1) + k
            xc = xc + cw_ref[k:k + 1, :] * xp[off:off + T_SCAN, :]
        xcb = xc.astype(BF16)
        r = jax.nn.sigmoid(jnp.dot(xcb, wa_ref[...], preferred_element_type=F32) + ba_ref[...])
        i = jax.nn.sigmoid(jnp.dot(xcb, wx_ref[...], preferred_element_type=F32) + bx_ref[...])
        log_a = -(r * decay_rate)
        a = jnp.exp(log_a)
        inp = jnp.sqrt(-jnp.tanh(log_a) * (a * a + 1.0)) * (i * xc)
        for j in range(n_tiles):
            slot = b * n_tiles + j
            a_ref[pl.ds(slot, T_SCAN, stride=LRU_PITCH), :] = a[:, j * LANES:(j + 1) * LANES]
            b_ref[pl.ds(slot, T_SCAN, stride=LRU_PITCH), :] = inp[:, j * LANES:(j + 1) * LANES]

    def step(t, h):
        base = t * LRU_PITCH
        h = a_ref[pl.ds(base, LRU_SLOTS), :] * h + b_ref[pl.ds(base, LRU_SLOTS), :]
        b_ref[pl.ds(base, LRU_SLOTS), :] = h
        return h

    h_ref[...] = lax.fori_loop(0, T_SCAN, step, h_ref[...], unroll=8)

    for b in range(BATCH):
        h = jnp.concatenate(
            [b_ref[pl.ds(b * n_tiles + j, T_SCAN, stride=LRU_PITCH), :] for j in range(n_tiles)],
            axis=1)
        o_ref[b] = (h * jax.nn.gelu(gate_ref[b])).astype(MIX_OUT)


def _lru(zmix3, cw, cb, wa, ba, wx, bx, lam, layer):
    vec = _layer_block((1, MIXER_WIDTH), layer)
    mat = _layer_block((MIXER_WIDTH, MIXER_WIDTH), layer)
    return pl.pallas_call(
        _lru_kernel,
        grid=(SEQ // T_SCAN,),
        in_specs=[pl.BlockSpec((BATCH, T_SCAN, MIXER_WIDTH), lambda c: (0, c, 3)),
                  pl.BlockSpec((BATCH, T_SCAN, MIXER_WIDTH), lambda c: (0, c, 4)),
                  _layer_block((LRU_CONV, MIXER_WIDTH), layer), vec, mat, vec, mat, vec, vec],
        out_specs=pl.BlockSpec((BATCH, T_SCAN, MIXER_WIDTH), lambda c: (0, c, 0)),
        out_shape=jax.ShapeDtypeStruct((BATCH, SEQ, MIXER_WIDTH), MIX_OUT),
        scratch_shapes=[pltpu.VMEM((BATCH, SUBLANES, MIXER_WIDTH), F32),
                        pltpu.VMEM((T_SCAN * LRU_PITCH, LANES), F32),
                        pltpu.VMEM((T_SCAN * LRU_PITCH, LANES), F32),
                        pltpu.VMEM((LRU_SLOTS, LANES), F32)],
        compiler_params=_params(1),
        name="rglru",
    )(zmix3, zmix3, cw, cb, wa, ba, wx, bx, lam)


def _attn_kernel(q_ref, qadd_ref, k_ref, kadd_ref, v_ref, o_ref, s_ref, m_ref, acc_ref):
    i = pl.program_id(2)
    scale = FOX_HEAD_DIM ** -0.5 * LOG2E
    neg = jnp.finfo(F32).min
    denom_lane = lax.broadcasted_iota(jnp.int32, (TK, HEAD_PAD), 1) == FOX_HEAD_DIM
    slots = [slice(n * HEAD_PAD, (n + 1) * HEAD_PAD) for n in range(ATTN_HEADS_PER_STEP)]
    qs = [(q_ref[:, sl].astype(F32) * scale + qadd_ref[:, sl].astype(F32)).astype(BF16) for sl in slots]

    def logits(n, j):
        start = pl.multiple_of(j * TK, TK)
        ks = k_ref[pl.ds(start, TK), slots[n]] + kadd_ref[pl.ds(start, TK), slots[n]]
        return lax.dot_general(qs[n], ks, (((1,), (1,)), ((), ())), preferred_element_type=F32)

    def block(j, masked, prefetch):
        start = pl.multiple_of(j * TK, TK)
        for n, sl in enumerate(slots):
            s = s_ref[n]
            vs = jnp.where(denom_lane, 1.0, v_ref[pl.ds(start, TK), sl]).astype(BF16)
            if masked:
                rr = lax.broadcasted_iota(jnp.int32, (TQ, TK), 0)
                cc = lax.broadcasted_iota(jnp.int32, (TQ, TK), 1)
                s = jnp.where(cc <= rr, s, neg)
            m = m_ref[n]
            m_new = jnp.maximum(m, jnp.max(s, axis=1, keepdims=True))
            alpha = jnp.exp2(m - m_new)
            p = jnp.exp2(s - jnp.concatenate([m_new] * (TK // LANES), axis=1))
            m_ref[n] = m_new
            acc_ref[n] = alpha * acc_ref[n] + jnp.dot(p.astype(BF16), vs, preferred_element_type=F32)
            if prefetch:
                s_ref[n] = logits(n, j + 1)

    for n in range(ATTN_HEADS_PER_STEP):
        s_ref[n] = logits(n, 0)
    m_ref[...] = jnp.full(m_ref.shape, neg, F32)
    acc_ref[...] = jnp.zeros_like(acc_ref)

    @pl.loop(0, i // 2)
    def _(jj):
        block(2 * jj, False, True)
        block(2 * jj + 1, False, True)

    @pl.when(i % 2 == 1)
    def _():
        block(i - 1, False, True)

    block(i, True, False)
    out_lane = lax.broadcasted_iota(jnp.int32, (TQ, HEAD_PAD), 1) < FOX_HEAD_DIM
    for n, sl in enumerate(slots):
        acc = acc_ref[n]
        o_ref[:, sl] = jnp.where(out_lane, acc / acc[:, FOX_HEAD_DIM:FOX_HEAD_DIM + 1], 0.0).astype(MIX_OUT)


def _attn(qkv, qadd, kadd):
    n_q = SEQ // TQ
    n_hg = FOX_HEADS // ATTN_HEADS_PER_STEP
    width = ATTN_HEADS_PER_STEP * HEAD_PAD
    qrow = lambda b, h, i: (b * n_q + i, h)
    return pl.pallas_call(
        _attn_kernel,
        grid=(BATCH, n_hg, n_q),
        in_specs=[pl.BlockSpec((TQ, width), qrow),
                  pl.BlockSpec((TQ, width), qrow),
                  pl.BlockSpec((SEQ, width), lambda b, h, i: (b, n_hg + h)),
                  pl.BlockSpec((SEQ, width), lambda b, h, i: (b, h)),
                  pl.BlockSpec((SEQ, width), lambda b, h, i: (b, 2 * n_hg + h))],
        out_specs=pl.BlockSpec((TQ, width), qrow),
        out_shape=jax.ShapeDtypeStruct((N_TOK, FOX_HEADS * HEAD_PAD), MIX_OUT),
        scratch_shapes=[pltpu.VMEM((ATTN_HEADS_PER_STEP, TQ, TK), F32),
                        pltpu.VMEM((ATTN_HEADS_PER_STEP, TQ, LANES), F32),
                        pltpu.VMEM((ATTN_HEADS_PER_STEP, TQ, HEAD_PAD), F32)],
        compiler_params=_params(3),
        name="fox_attention",
    )(qkv, qadd, qkv, kadd, qkv)


def _merge_kernel(ya_ref, yb_ref, yc_ref, yd_ref, x_ref, g_ref, gd_ref, w_ref, o_ref):
    w = MIXER_WIDTH
    parts = [_rms(ya_ref[...].astype(F32), g_ref[:, 0:w], w),
             _rms(yb_ref[...].astype(F32), g_ref[:, w:2 * w], w),
             _rms(yc_ref[...].astype(F32), g_ref[:, 2 * w:3 * w], w),
             _rms(yd_ref[...].astype(F32), gd_ref[...], w)]
    y = jnp.concatenate(parts, axis=1).astype(BF16)
    o_ref[...] = x_ref[...] + jnp.dot(y, w_ref[...], preferred_element_type=F32)


def _merge(ya, yb, yc, yd, x, g, gd, w, layer):
    row = lambda i: (i, 0)
    k_dim = 3 * MIXER_WIDTH + FOX_HEADS * HEAD_PAD
    mix = pl.BlockSpec((TM_MERGE, MIXER_WIDTH), row)
    return pl.pallas_call(
        _merge_kernel,
        grid=(N_TOK // TM_MERGE,),
        in_specs=[mix, mix, mix,
                  pl.BlockSpec((TM_MERGE, FOX_HEADS * HEAD_PAD), row),
                  pl.BlockSpec((TM_MERGE, D_MODEL), row),
                  _layer_block((1, 3 * MIXER_WIDTH), layer),
                  _layer_block((1, FOX_HEADS * HEAD_PAD), layer),
                  _layer_block((k_dim, D_MODEL), layer)],
        out_specs=pl.BlockSpec((TM_MERGE, D_MODEL), row),
        out_shape=jax.ShapeDtypeStruct((N_TOK, D_MODEL), F32),
        compiler_params=_params(1),
        name="merge_out_proj",
    )(ya, yb, yc, yd, x, g, gd, w)


def _mlp_kernel(x_ref, g_ref, w1_ref, w2_ref, fg_ref, o_ref, h_ref, *, final_norm):
    j = pl.program_id(1)

    @pl.when(j == 0)
    def _():
        x = x_ref[...]
        h_ref[...] = _rms(x, g_ref[...], D_MODEL).astype(BF16)
        o_ref[...] = x

    a = jnp.dot(h_ref[...], w1_ref[...], preferred_element_type=F32)
    a = jnp.square(jnp.maximum(a, 0.0)).astype(BF16)
    o_ref[...] += jnp.dot(a, w2_ref[...], preferred_element_type=F32)

    if final_norm:
        @pl.when(j == pl.num_programs(1) - 1)
        def _():
            o_ref[...] = _rms(o_ref[...], fg_ref[...], D_MODEL)


def _mlp(x, g, w1, w2, fg, layer, final_norm):
    return pl.pallas_call(
        functools.partial(_mlp_kernel, final_norm=final_norm),
        grid=(N_TOK // TM_MLP, D_FF // TF_MLP),
        in_specs=[pl.BlockSpec((TM_MLP, D_MODEL), lambda i, j: (i, 0)),
                  _layer_block((1, D_MODEL), layer),
                  pl.BlockSpec((None, D_MODEL, TF_MLP), lambda i, j: (layer, 0, j)),
                  pl.BlockSpec((None, TF_MLP, D_MODEL), lambda i, j: (layer, j, 0)),
                  pl.BlockSpec((1, D_MODEL), lambda i, j: (0, 0))],
        out_specs=pl.BlockSpec((TM_MLP, D_MODEL), lambda i, j: (i, 0)),
        out_shape=jax.ShapeDtypeStruct((N_TOK, D_MODEL), F32),
        scratch_shapes=[pltpu.VMEM((TM_MLP, D_MODEL), BF16)],
        compiler_params=_params(2),
        name="mlp",
    )(x, g, w1, w2, fg)


def _pad_heads(w):
    lead = w.shape[:-1]
    w = w.reshape(*lead, FOX_HEADS, FOX_HEAD_DIM)
    w = jnp.pad(w, [(0, 0)] * (len(lead) + 1) + [(0, HEAD_PAD - FOX_HEAD_DIM)])
    return w.reshape(*lead, FOX_HEADS * HEAD_PAD)


def _block_diag(blocks):
    n, g, r, c = blocks.shape
    eye = jnp.eye(g, dtype=blocks.dtype)
    return jnp.einsum('ngrc,gh->ngrhc', blocks, eye).reshape(n, g * r, g * c)


def _s5_discretize(lam_re, lam_im, log_dt, b_re, b_im):
    dt = jnp.exp(log_dt)[..., None]
    mag = jnp.exp(lam_re * dt)
    abar_re = mag * jnp.cos(lam_im * dt)
    abar_im = mag * jnp.sin(lam_im * dt)
    denom = jnp.square(lam_re) + jnp.square(lam_im)
    num_re = abar_re - 1.0
    num_im = abar_im
    fac_re = (num_re * lam_re + num_im * lam_im) / denom
    fac_im = (num_im * lam_re - num_re * lam_im) / denom
    bbar_re = fac_re[..., None] * b_re - fac_im[..., None] * b_im
    bbar_im = fac_re[..., None] * b_im + fac_im[..., None] * b_re
    return abar_re, abar_im, bbar_re, bbar_im


def kernel(x, norm1_g, w_in, sgu_norm_g, sgu_w, sgu_b, s5_lambda_re, s5_lambda_im, s5_log_dt, s5_b_re, s5_b_im, s5_c_re, s5_c_im, s5_d, s5_glu_w, s5_glu_b, lru_conv_w, lru_conv_b, lru_wa, lru_ba, lru_wx, lru_bx, lru_lambda, fox_fgate_b, mix_norm_g, w_out, norm2_g, w_mlp_in, w_mlp_out, final_g):
    w = MIXER_WIDTH
    row = lambda v: v.reshape(DEPTH, 1, -1)

    w_in_p = jnp.pad(w_in.astype(BF16), ((0, 0), (0, 0), (0, W_IN_COLS - w_in.shape[-1])))
    bf = row(jnp.pad(fox_fgate_b, ((0, 0), (0, LANES - FOX_HEADS))))

    sgu_wcat = jnp.transpose(sgu_w, (0, 2, 1, 3)).reshape(DEPTH, SGU_CHUNK, SGU_HEADS * SGU_CHUNK)
    sgu_bias = jnp.repeat(jnp.transpose(sgu_b, (0, 2, 1)), w // SGU_HEADS, axis=2)

    abar_re, abar_im, bbar_re, bbar_im = _s5_discretize(s5_lambda_re, s5_lambda_im, s5_log_dt, s5_b_re, s5_b_im)
    swap = lambda t: jnp.transpose(t, (0, 1, 3, 2))
    bbd = jnp.concatenate([_block_diag(swap(bbar_re)), _block_diag(swap(bbar_im))], axis=2).astype(BF16)
    cbd = jnp.concatenate([_block_diag(swap(s5_c_re)), -_block_diag(swap(s5_c_im))], axis=1).astype(BF16)
    lam = jnp.concatenate([abar_re.reshape(DEPTH, SUBLANES, LANES), abar_im.reshape(DEPTH, SUBLANES, LANES)], axis=1)
    glu_w = s5_glu_w.astype(BF16)

    wa_bd = _block_diag(lru_wa).astype(BF16)
    wx_bd = _block_diag(lru_wx).astype(BF16)

    w_o_pad = jnp.concatenate(
        [w_out[:, 0:3 * w],
         jnp.pad(w_out[:, 3 * w:].reshape(DEPTH, FOX_HEADS, FOX_HEAD_DIM, D_MODEL),
                 ((0, 0), (0, 0), (0, HEAD_PAD - FOX_HEAD_DIM), (0, 0))).reshape(DEPTH, FOX_HEADS * HEAD_PAD, D_MODEL)],
        axis=1).astype(BF16)
    g_mix = row(mix_norm_g[:, 0:3 * w])
    g_mix_d = _pad_heads(row(mix_norm_g[:, 3 * w:]))
    w1 = w_mlp_in.astype(BF16)
    w2 = w_mlp_out.astype(BF16)
    g1, g2 = row(norm1_g), row(norm2_g)
    fg = final_g.reshape(1, D_MODEL)

    xf = x.reshape(N_TOK, D_MODEL)
    for l in range(DEPTH):
        zmix, qkv, zf = _in_proj(xf, g1, w_in_p, l)
        zmix3 = zmix.reshape(BATCH, SEQ, ZMIX_COLS)
        qadd, kadd = _fcum(zf.reshape(BATCH, SEQ, LANES), bf, l)
        qadd = qadd.reshape(N_TOK, FOX_HEADS * HEAD_PAD)
        kadd = kadd.reshape(N_TOK, FOX_HEADS * HEAD_PAD)
        y_a = _sgu(zmix, row(sgu_norm_g), sgu_wcat, sgu_bias, l)
        y_b = _s5(zmix3, bbd, lam, cbd, row(s5_d), glu_w, row(s5_glu_b), l)
        y_c = _lru(zmix3, lru_conv_w, row(lru_conv_b), wa_bd, row(lru_ba.reshape(DEPTH, w)),
                   wx_bd, row(lru_bx.reshape(DEPTH, w)), row(lru_lambda), l)
        y_d = _attn(qkv, qadd, kadd)
        xf = _merge(y_a, y_b.reshape(N_TOK, w), y_c.reshape(N_TOK, w), y_d, xf, g_mix, g_mix_d, w_o_pad, l)
        xf = _mlp(xf, g2, w1, w2, fg, l, final_norm=(l == DEPTH - 1))
    return xf.reshape(BATCH, SEQ, D_MODEL)
```

```python
import functools

import jax
import jax.numpy as jnp
import numpy as np
from jax import lax
from jax.experimental import pallas as pl
from jax.experimental.pallas import tpu as pltpu

D_MODEL = 1024
BATCH = 4
SEQ = 4096
DEPTH = 4
N_TOK = BATCH * SEQ
MIXER_WIDTH = 256
SGU_HEADS = 4
SGU_CHUNK = 128
S5_GROUP = 16
S5_GROUPS = 16
S5_STATE = 64
LRU_HEADS = 4
LRU_CONV = 4
LRU_C = 8.0
FOX_HEADS = 4
FOX_HEAD_DIM = 64
D_FF = 4 * D_MODEL
RMS_EPS = 1e-6
LOG2E = 1.4426950408889634

LANES = 128
SUBLANES = 8
HEAD_PAD = LANES
ZMIX_COLS = 5 * MIXER_WIDTH
W_IN_COLS = 8 * MIXER_WIDTH + LANES
QKV_COLS = 3 * FOX_HEADS * HEAD_PAD
S5_NSTATE = S5_GROUPS * S5_STATE
S5_ROWS = 2 * S5_NSTATE // LANES
SCAN_ROW_GAP = 4
S5_PITCH = S5_ROWS + SCAN_ROW_GAP

TM_IN = 512
TM_MERGE = 512
TM_MLP = 1024
TF_MLP = 1024
T_SGU = 2048
T_SCAN = 256
T_CUM = 256
TQ = 512
TK = 512
ATTN_HEADS_PER_STEP = 4
VMEM_LIMIT = 48 * 1024 * 1024

F32 = jnp.float32
BF16 = jnp.bfloat16
MIX_OUT = BF16


def _params(n_axes):
    return pltpu.CompilerParams(dimension_semantics=("arbitrary",) * n_axes,
                                vmem_limit_bytes=VMEM_LIMIT)


def _layer_block(shape, layer):
    zeros = (0,) * len(shape)
    return pl.BlockSpec((None,) + tuple(shape), lambda *_: (layer,) + zeros)


def _rms(x, g, width):
    ms = jnp.sum(jnp.square(x), axis=-1, keepdims=True) * (1.0 / width)
    return x * lax.rsqrt(ms + RMS_EPS) * g


def _softplus(x):
    return jnp.maximum(x, 0.0) + jnp.log1p(jnp.exp(-jnp.abs(x)))


def _in_proj_kernel(x_ref, g_ref, w_ref, zmix_ref, qkv_ref, zf_ref):
    h = _rms(x_ref[...], g_ref[...], D_MODEL).astype(BF16)
    zmix_ref[...] = jnp.dot(h, w_ref[:, 0:ZMIX_COLS], preferred_element_type=F32)
    qkv_lo = ZMIX_COLS
    qkv_hi = ZMIX_COLS + 3 * MIXER_WIDTH
    zqkv = jnp.dot(h, w_ref[:, qkv_lo:qkv_hi], preferred_element_type=F32)
    pad = jnp.zeros((TM_IN, HEAD_PAD - FOX_HEAD_DIM), F32)
    for slot in range(3 * FOX_HEADS):
        head = zqkv[:, slot * FOX_HEAD_DIM:(slot + 1) * FOX_HEAD_DIM]
        qkv_ref[:, slot * HEAD_PAD:(slot + 1) * HEAD_PAD] = jnp.concatenate([head, pad], axis=1).astype(BF16)
    zf_ref[...] = jnp.dot(h, w_ref[:, qkv_hi:W_IN_COLS], preferred_element_type=F32)


def _in_proj(x, g, w, layer):
    row = lambda i: (i, 0)
    return pl.pallas_call(
        _in_proj_kernel,
        grid=(N_TOK // TM_IN,),
        in_specs=[pl.BlockSpec((TM_IN, D_MODEL), row),
                  _layer_block((1, D_MODEL), layer),
                  _layer_block((D_MODEL, W_IN_COLS), layer)],
        out_specs=[pl.BlockSpec((TM_IN, ZMIX_COLS), row),
                   pl.BlockSpec((TM_IN, QKV_COLS), row),
                   pl.BlockSpec((TM_IN, LANES), row)],
        out_shape=[jax.ShapeDtypeStruct((N_TOK, ZMIX_COLS), F32),
                   jax.ShapeDtypeStruct((N_TOK, QKV_COLS), BF16),
                   jax.ShapeDtypeStruct((N_TOK, LANES), F32)],
        compiler_params=_params(1),
        name="in_proj",
    )(x, g, w)


def _split3(x):
    hi = x.astype(BF16)
    r1 = x - hi.astype(F32)
    mid = r1.astype(BF16)
    lo = (r1 - mid.astype(F32)).astype(BF16)
    return hi, mid, lo


def _bias_placement():
    pq = np.zeros((LANES, FOX_HEADS * HEAD_PAD), np.float32)
    pk = np.zeros_like(pq)
    ones_q = np.zeros((1, FOX_HEADS * HEAD_PAD), np.float32)
    ones_k = np.zeros_like(ones_q)
    for h in range(FOX_HEADS):
        for piece in range(3):
            pq[piece * FOX_HEADS + h, h * HEAD_PAD + FOX_HEAD_DIM + piece] = 1.0
            pk[piece * FOX_HEADS + h, h * HEAD_PAD + FOX_HEAD_DIM + 3 + piece] = -1.0
            ones_q[0, h * HEAD_PAD + FOX_HEAD_DIM + 3 + piece] = 1.0
            ones_k[0, h * HEAD_PAD + FOX_HEAD_DIM + piece] = 1.0
    return (jnp.asarray(pq, BF16), jnp.asarray(pk, BF16), jnp.asarray(ones_q), jnp.asarray(ones_k))


def _fcum_kernel(zf_ref, bf_ref, tril_ref, pq_ref, pk_ref, oq_ref, ok_ref, qadd_ref, kadd_ref, carry_ref):
    @pl.when(pl.program_id(0) == 0)
    def _():
        carry_ref[...] = jnp.zeros_like(carry_ref)

    tril = tril_ref[...]
    lane = lax.broadcasted_iota(jnp.int32, (T_CUM, LANES), 1)
    for b in range(BATCH):
        logit = zf_ref[b] + bf_ref[...]
        log_f = -_softplus(-logit)
        hi, mid, lo = _split3(log_f)
        cs = (jnp.dot(tril, hi, preferred_element_type=F32)
              + jnp.dot(tril, mid, preferred_element_type=F32)
              + jnp.dot(tril, lo, preferred_element_type=F32))
        cum = cs + carry_ref[b]
        carry_ref[b] = cum[T_CUM - 1:T_CUM, :]
        hi, mid, lo = [p.astype(F32) for p in _split3(cum * LOG2E)]
        pieces = jnp.where(lane < FOX_HEADS, hi,
                           jnp.where(lane < 2 * FOX_HEADS, pltpu.roll(mid, FOX_HEADS, axis=1),
                                     jnp.where(lane < 3 * FOX_HEADS, pltpu.roll(lo, 2 * FOX_HEADS, axis=1), 0.0))
                           ).astype(BF16)
        qadd_ref[b] = (jnp.dot(pieces, pq_ref[...], preferred_element_type=F32) + oq_ref[...]).astype(BF16)
        kadd_ref[b] = (jnp.dot(pieces, pk_ref[...], preferred_element_type=F32) + ok_ref[...]).astype(BF16)


def _fcum(zf3, bf, layer):
    pq, pk, ones_q, ones_k = _bias_placement()
    tril = jnp.asarray(np.tril(np.ones((T_CUM, T_CUM), np.float32)), BF16)
    full = lambda c: (0, 0)
    wide = FOX_HEADS * HEAD_PAD
    chunk = lambda c: (0, c, 0)
    return pl.pallas_call(
        _fcum_kernel,
        grid=(SEQ // T_CUM,),
        in_specs=[pl.BlockSpec((BATCH, T_CUM, LANES), chunk),
                  _layer_block((1, LANES), layer),
                  pl.BlockSpec((T_CUM, T_CUM), full),
                  pl.BlockSpec((LANES, wide), full),
                  pl.BlockSpec((LANES, wide), full),
                  pl.BlockSpec((1, wide), full),
                  pl.BlockSpec((1, wide), full)],
        out_specs=[pl.BlockSpec((BATCH, T_CUM, wide), chunk),
                   pl.BlockSpec((BATCH, T_CUM, wide), chunk)],
        out_shape=[jax.ShapeDtypeStruct((BATCH, SEQ, wide), BF16),
                   jax.ShapeDtypeStruct((BATCH, SEQ, wide), BF16)],
        scratch_shapes=[pltpu.VMEM((BATCH, 1, LANES), F32)],
        compiler_params=_params(1),
        name="forget_cumsum",
    )(zf3, bf, tril, pq, pk, ones_q, ones_k)


def _sgu_kernel(zu_ref, zv_ref, g_ref, w_ref, b_ref, o_ref):
    u = jax.nn.gelu(zu_ref[...])
    v = _rms(jax.nn.gelu(zv_ref[...]), g_ref[...], MIXER_WIDTH)
    hd = MIXER_WIDTH // SGU_HEADS
    lane_head = lax.broadcasted_iota(jnp.int32, (SGU_CHUNK, MIXER_WIDTH), 1) // hd
    t_idx = lax.broadcasted_iota(jnp.int32, (SGU_CHUNK, SGU_HEADS * SGU_CHUNK), 0)
    s_idx = lax.broadcasted_iota(jnp.int32, (SGU_CHUNK, SGU_HEADS * SGU_CHUNK), 1) % SGU_CHUNK
    wm = jnp.where(s_idx <= t_idx, w_ref[...], 0.0).astype(BF16)
    bias = b_ref[...]
    for c in range(T_SGU // SGU_CHUNK):
        rows = slice(c * SGU_CHUNK, (c + 1) * SGU_CHUNK)
        vc = v[rows]
        vstack = jnp.concatenate(
            [jnp.where(lane_head == h, vc, 0.0) for h in range(SGU_HEADS)], axis=0).astype(BF16)
        mixed = jnp.dot(wm, vstack, preferred_element_type=F32) + bias
        o_ref[rows, :] = (u[rows] * mixed).astype(MIX_OUT)


def _sgu(zmix, g, w, b, layer):
    return pl.pallas_call(
        _sgu_kernel,
        grid=(N_TOK // T_SGU,),
        in_specs=[pl.BlockSpec((T_SGU, MIXER_WIDTH), lambda i: (i, 0)),
                  pl.BlockSpec((T_SGU, MIXER_WIDTH), lambda i: (i, 1)),
                  _layer_block((1, MIXER_WIDTH), layer),
                  _layer_block((SGU_CHUNK, SGU_HEADS * SGU_CHUNK), layer),
                  _layer_block((SGU_CHUNK, MIXER_WIDTH), layer)],
        out_specs=pl.BlockSpec((T_SGU, MIXER_WIDTH), lambda i: (i, 0)),
        out_shape=jax.ShapeDtypeStruct((N_TOK, MIXER_WIDTH), MIX_OUT),
        compiler_params=_params(1),
        name="sgu",
    )(zmix, zmix, g, w, b)


def _s5_kernel(u_ref, bbd_ref, lam_ref, cbd_ref, d_ref, gw_ref, gb_ref, o_ref, s_ref, h_ref):
    @pl.when(pl.program_id(0) == 0)
    def _():
        h_ref[...] = jnp.zeros_like(h_ref)

    for b in range(BATCH):
        bu = jnp.dot(u_ref[b].astype(BF16), bbd_ref[...], preferred_element_type=F32)
        for j in range(S5_ROWS):
            s_ref[b, pl.ds(j, T_SCAN, stride=S5_PITCH), :] = bu[:, j * LANES:(j + 1) * LANES]

    lam_re = lam_ref[0:SUBLANES, :]
    lam_im = lam_ref[SUBLANES:S5_ROWS, :]

    def step(t, carry):
        base = t * S5_PITCH
        new = []
        for b in range(BATCH):
            h_re, h_im = carry[2 * b], carry[2 * b + 1]
            n_re = lam_re * h_re - lam_im * h_im + s_ref[b, pl.ds(base, SUBLANES), :]
            n_im = lam_re * h_im + lam_im * h_re + s_ref[b, pl.ds(base + SUBLANES, SUBLANES), :]
            s_ref[b, pl.ds(base, SUBLANES), :] = n_re
            s_ref[b, pl.ds(base + SUBLANES, SUBLANES), :] = n_im
            new += [n_re, n_im]
        return tuple(new)

    init = []
    for b in range(BATCH):
        init += [h_ref[b, 0:SUBLANES, :], h_ref[b, SUBLANES:S5_ROWS, :]]
    fin = lax.fori_loop(0, T_SCAN, step, tuple(init), unroll=8)
    for b in range(BATCH):
        h_ref[b, 0:SUBLANES, :] = fin[2 * b]
        h_ref[b, SUBLANES:S5_ROWS, :] = fin[2 * b + 1]

    for b in range(BATCH):
        states = jnp.concatenate(
            [s_ref[b, pl.ds(j, T_SCAN, stride=S5_PITCH), :] for j in range(S5_ROWS)], axis=1)
        u = u_ref[b]
        y = jnp.dot(states.astype(BF16), cbd_ref[...], preferred_element_type=F32) + d_ref[...] * u
        y = jax.nn.gelu(y)
        gate = jnp.dot(y.astype(BF16), gw_ref[...], preferred_element_type=F32) + gb_ref[...]
        o_ref[b] = (y * jax.nn.sigmoid(gate)).astype(MIX_OUT)


def _s5(zmix3, bbd, lam, cbd, d, gw, gb, layer):
    return pl.pallas_call(
        _s5_kernel,
        grid=(SEQ // T_SCAN,),
        in_specs=[pl.BlockSpec((BATCH, T_SCAN, MIXER_WIDTH), lambda c: (0, c, 2)),
                  _layer_block((MIXER_WIDTH, 2 * S5_NSTATE), layer),
                  _layer_block((S5_ROWS, LANES), layer),
                  _layer_block((2 * S5_NSTATE, MIXER_WIDTH), layer),
                  _layer_block((1, MIXER_WIDTH), layer),
                  _layer_block((MIXER_WIDTH, MIXER_WIDTH), layer),
                  _layer_block((1, MIXER_WIDTH), layer)],
        out_specs=pl.BlockSpec((BATCH, T_SCAN, MIXER_WIDTH), lambda c: (0, c, 0)),
        out_shape=jax.ShapeDtypeStruct((BATCH, SEQ, MIXER_WIDTH), MIX_OUT),
        scratch_shapes=[pltpu.VMEM((BATCH, T_SCAN * S5_PITCH, LANES), F32),
                        pltpu.VMEM((BATCH, S5_ROWS, LANES), F32)],
        compiler_params=_params(1),
        name="s5",
    )(zmix3, bbd, lam, cbd, d, gw, gb)


LRU_SLOTS = BATCH * MIXER_WIDTH // LANES
LRU_PITCH = LRU_SLOTS + SCAN_ROW_GAP


def _lru_kernel(x_ref, gate_ref, cw_ref, cb_ref, wa_ref, ba_ref, wx_ref, bx_ref, lam_ref, o_ref,
                tail_ref, a_ref, b_ref, h_ref):
    @pl.when(pl.program_id(0) == 0)
    def _():
        tail_ref[...] = jnp.zeros_like(tail_ref)
        h_ref[...] = jnp.zeros_like(h_ref)

    n_tiles = MIXER_WIDTH // LANES
    decay_rate = LRU_C * _softplus(-lam_ref[...])
    for b in range(BATCH):
        x = x_ref[b]
        xp = jnp.concatenate([tail_ref[b], x], axis=0)
        tail_ref[b] = x[T_SCAN - SUBLANES:T_SCAN, :]
        xc = cb_ref[...]
        for k in range(LRU_CONV):
            off = SUBLANES - (LRU_CONV - 1) + k
            xc = xc + cw_ref[k:k + 1, :] * xp[off:off + T_SCAN, :]
        xcb = xc.astype(BF16)
        r = jax.nn.sigmoid(jnp.dot(xcb, wa_ref[...], preferred_element_type=F32) + ba_ref[...])
        i = jax.nn.sigmoid(jnp.dot(xcb, wx_ref[...], preferred_element_type=F32) + bx_ref[...])
        log_a = -(r * decay_rate)
        a = jnp.exp(log_a)
        inp = jnp.sqrt(-jnp.tanh(log_a) * (a * a + 1.0)) * (i * xc)
        for j in range(n_tiles):
            slot = b * n_tiles + j
            a_ref[pl.ds(slot, T_SCAN, stride=LRU_PITCH), :] = a[:, j * LANES:(j + 1) * LANES]
            b_ref[pl.ds(slot, T_SCAN, stride=LRU_PITCH), :] = inp[:, j * LANES:(j + 1) * LANES]

    def step(t, h):
        base = t * LRU_PITCH
        h = a_ref[pl.ds(base, LRU_SLOTS), :] * h + b_ref[pl.ds(base, LRU_SLOTS), :]
        b_ref[pl.ds(base, LRU_SLOTS), :] = h
        return h

    h_ref[...] = lax.fori_loop(0, T_SCAN, step, h_ref[...], unroll=8)

    for b in range(BATCH):
        h = jnp.concatenate(
            [b_ref[pl.ds(b * n_tiles + j, T_SCAN, stride=LRU_PITCH), :] for j in range(n_tiles)],
            axis=1)
        o_ref[b] = (h * jax.nn.gelu(gate_ref[b])).astype(MIX_OUT)


def _lru(zmix3, cw, cb, wa, ba, wx, bx, lam, layer):
    vec = _layer_block((1, MIXER_WIDTH), layer)
    mat = _layer_block((MIXER_WIDTH, MIXER_WIDTH), layer)
    return pl.pallas_call(
        _lru_kernel,
        grid=(SEQ // T_SCAN,),
        in_specs=[pl.BlockSpec((BATCH, T_SCAN, MIXER_WIDTH), lambda c: (0, c, 3)),
                  pl.BlockSpec((BATCH, T_SCAN, MIXER_WIDTH), lambda c: (0, c, 4)),
                  _layer_block((LRU_CONV, MIXER_WIDTH), layer), vec, mat, vec, mat, vec, vec],
        out_specs=pl.BlockSpec((BATCH, T_SCAN, MIXER_WIDTH), lambda c: (0, c, 0)),
        out_shape=jax.ShapeDtypeStruct((BATCH, SEQ, MIXER_WIDTH), MIX_OUT),
        scratch_shapes=[pltpu.VMEM((BATCH, SUBLANES, MIXER_WIDTH), F32),
                        pltpu.VMEM((T_SCAN * LRU_PITCH, LANES), F32),
                        pltpu.VMEM((T_SCAN * LRU_PITCH, LANES), F32),
                        pltpu.VMEM((LRU_SLOTS, LANES), F32)],
        compiler_params=_params(1),
        name="rglru",
    )(zmix3, zmix3, cw, cb, wa, ba, wx, bx, lam)


def _attn_kernel(q_ref, qadd_ref, k_ref, kadd_ref, v_ref, o_ref, s_ref, m_ref, acc_ref):
    i = pl.program_id(2)
    scale = FOX_HEAD_DIM ** -0.5 * LOG2E
    neg = jnp.finfo(F32).min
    denom_lane = lax.broadcasted_iota(jnp.int32, (TK, HEAD_PAD), 1) == FOX_HEAD_DIM
    slots = [slice(n * HEAD_PAD, (n + 1) * HEAD_PAD) for n in range(ATTN_HEADS_PER_STEP)]
    qs = [(q_ref[:, sl].astype(F32) * scale + qadd_ref[:, sl].astype(F32)).astype(BF16) for sl in slots]

    def logits(n, j):
        start = pl.multiple_of(j * TK, TK)
        ks = k_ref[pl.ds(start, TK), slots[n]] + kadd_ref[pl.ds(start, TK), slots[n]]
        return lax.dot_general(qs[n], ks, (((1,), (1,)), ((), ())), preferred_element_type=F32)

    def block(j, masked, prefetch):
        start = pl.multiple_of(j * TK, TK)
        for n, sl in enumerate(slots):
            s = s_ref[n]
            vs = jnp.where(denom_lane, 1.0, v_ref[pl.ds(start, TK), sl]).astype(BF16)
            if masked:
                rr = lax.broadcasted_iota(jnp.int32, (TQ, TK), 0)
                cc = lax.broadcasted_iota(jnp.int32, (TQ, TK), 1)
                s = jnp.where(cc <= rr, s, neg)
            m = m_ref[n]
            m_new = jnp.maximum(m, jnp.max(s, axis=1, keepdims=True))
            alpha = jnp.exp2(m - m_new)
            p = jnp.exp2(s - jnp.concatenate([m_new] * (TK // LANES), axis=1))
            m_ref[n] = m_new
            acc_ref[n] = alpha * acc_ref[n] + jnp.dot(p.astype(BF16), vs, preferred_element_type=F32)
            if prefetch:
                s_ref[n] = logits(n, j + 1)

    for n in range(ATTN_HEADS_PER_STEP):
        s_ref[n] = logits(n, 0)
    m_ref[...] = jnp.full(m_ref.shape, neg, F32)
    acc_ref[...] = jnp.zeros_like(acc_ref)

    @pl.loop(0, i // 2)
    def _(jj):
        block(2 * jj, False, True)
        block(2 * jj + 1, False, True)

    @pl.when(i % 2 == 1)
    def _():
        block(i - 1, False, True)

    block(i, True, False)
    out_lane = lax.broadcasted_iota(jnp.int32, (TQ, HEAD_PAD), 1) < FOX_HEAD_DIM
    for n, sl in enumerate(slots):
        acc = acc_ref[n]
        o_ref[:, sl] = jnp.where(out_lane, acc / acc[:, FOX_HEAD_DIM:FOX_HEAD_DIM + 1], 0.0).astype(MIX_OUT)


def _attn(qkv, qadd, kadd):
    n_q = SEQ // TQ
    n_hg = FOX_HEADS // ATTN_HEADS_PER_STEP
    width = ATTN_HEADS_PER_STEP * HEAD_PAD
    qrow = lambda b, h, i: (b * n_q + i, h)
    return pl.pallas_call(
        _attn_kernel,
        grid=(BATCH, n_hg, n_q),
        in_specs=[pl.BlockSpec((TQ, width), qrow),
                  pl.BlockSpec((TQ, width), qrow),
                  pl.BlockSpec((SEQ, width), lambda b, h, i: (b, n_hg + h)),
                  pl.BlockSpec((SEQ, width), lambda b, h, i: (b, h)),
                  pl.BlockSpec((SEQ, width), lambda b, h, i: (b, 2 * n_hg + h))],
        out_specs=pl.BlockSpec((TQ, width), qrow),
        out_shape=jax.ShapeDtypeStruct((N_TOK, FOX_HEADS * HEAD_PAD), MIX_OUT),
        scratch_shapes=[pltpu.VMEM((ATTN_HEADS_PER_STEP, TQ, TK), F32),
                        pltpu.VMEM((ATTN_HEADS_PER_STEP, TQ, LANES), F32),
                        pltpu.VMEM((ATTN_HEADS_PER_STEP, TQ, HEAD_PAD), F32)],
        compiler_params=_params(3),
        name="fox_attention",
    )(qkv, qadd, qkv, kadd, qkv)


def _merge_mlp_kernel(ya_ref, yb_ref, yc_ref, yd_ref, x_ref, gm_ref, gmd_ref, wo_ref, g2_ref, w1_ref, w2_ref,
                      fg_ref, o_ref, h_ref, *, final_norm):
    j = pl.program_id(1)

    @pl.when(j == 0)
    def _():
        w = MIXER_WIDTH
        parts = [_rms(ya_ref[...].astype(F32), gm_ref[:, 0:w], w),
                 _rms(yb_ref[...].astype(F32), gm_ref[:, w:2 * w], w),
                 _rms(yc_ref[...].astype(F32), gm_ref[:, 2 * w:3 * w], w),
                 _rms(yd_ref[...].astype(F32), gmd_ref[...], w)]
        y = jnp.concatenate(parts, axis=1).astype(BF16)
        x1 = x_ref[...] + jnp.dot(y, wo_ref[...], preferred_element_type=F32)
        o_ref[...] = x1
        h_ref[...] = _rms(x1, g2_ref[...], D_MODEL).astype(BF16)

    a = jnp.dot(h_ref[...], w1_ref[...], preferred_element_type=F32)
    a = jnp.square(jnp.maximum(a, 0.0)).astype(BF16)
    o_ref[...] += jnp.dot(a, w2_ref[...], preferred_element_type=F32)

    if final_norm:
        @pl.when(j == pl.num_programs(1) - 1)
        def _():
            o_ref[...] = _rms(o_ref[...], fg_ref[...], D_MODEL)


def _merge_mlp(ya, yb, yc, yd, x, gm, gmd, wo, g2, w1, w2, fg, layer, final_norm):
    row = lambda i, j: (i, 0)
    k_dim = 3 * MIXER_WIDTH + FOX_HEADS * HEAD_PAD
    mix = pl.BlockSpec((TM_MLP, MIXER_WIDTH), row)
    return pl.pallas_call(
        functools.partial(_merge_mlp_kernel, final_norm=final_norm),
        grid=(N_TOK // TM_MLP, D_FF // TF_MLP),
        in_specs=[mix, mix, mix,
                  pl.BlockSpec((TM_MLP, FOX_HEADS * HEAD_PAD), row),
                  pl.BlockSpec((TM_MLP, D_MODEL), row),
                  _layer_block((1, 3 * MIXER_WIDTH), layer),
                  _layer_block((1, FOX_HEADS * HEAD_PAD), layer),
                  _layer_block((k_dim, D_MODEL), layer),
                  _layer_block((1, D_MODEL), layer),
                  pl.BlockSpec((None, D_MODEL, TF_MLP), lambda i, j: (layer, 0, j)),
                  pl.BlockSpec((None, TF_MLP, D_MODEL), lambda i, j: (layer, j, 0)),
                  pl.BlockSpec((1, D_MODEL), lambda i, j: (0, 0))],
        out_specs=pl.BlockSpec((TM_MLP, D_MODEL), row),
        out_shape=jax.ShapeDtypeStruct((N_TOK, D_MODEL), F32),
        scratch_shapes=[pltpu.VMEM((TM_MLP, D_MODEL), BF16)],
        compiler_params=_params(2),
        name="merge_mlp",
    )(ya, yb, yc, yd, x, gm, gmd, wo, g2, w1, w2, fg)


def _pad_heads(w):
    lead = w.shape[:-1]
    w = w.reshape(*lead, FOX_HEADS, FOX_HEAD_DIM)
    w = jnp.pad(w, [(0, 0)] * (len(lead) + 1) + [(0, HEAD_PAD - FOX_HEAD_DIM)])
    return w.reshape(*lead, FOX_HEADS * HEAD_PAD)


def _block_diag(blocks):
    n, g, r, c = blocks.shape
    eye = jnp.eye(g, dtype=blocks.dtype)
    return jnp.einsum('ngrc,gh->ngrhc', blocks, eye).reshape(n, g * r, g * c)


def _s5_discretize(lam_re, lam_im, log_dt, b_re, b_im):
    dt = jnp.exp(log_dt)[..., None]
    mag = jnp.exp(lam_re * dt)
    abar_re = mag * jnp.cos(lam_im * dt)
    abar_im = mag * jnp.sin(lam_im * dt)
    denom = jnp.square(lam_re) + jnp.square(lam_im)
    num_re = abar_re - 1.0
    num_im = abar_im
    fac_re = (num_re * lam_re + num_im * lam_im) / denom
    fac_im = (num_im * lam_re - num_re * lam_im) / denom
    bbar_re = fac_re[..., None] * b_re - fac_im[..., None] * b_im
    bbar_im = fac_re[..., None] * b_im + fac_im[..., None] * b_re
    return abar_re, abar_im, bbar_re, bbar_im


def kernel(x, norm1_g, w_in, sgu_norm_g, sgu_w, sgu_b, s5_lambda_re, s5_lambda_im, s5_log_dt, s5_b_re, s5_b_im, s5_c_re, s5_c_im, s5_d, s5_glu_w, s5_glu_b, lru_conv_w, lru_conv_b, lru_wa, lru_ba, lru_wx, lru_bx, lru_lambda, fox_fgate_b, mix_norm_g, w_out, norm2_g, w_mlp_in, w_mlp_out, final_g):
    w = MIXER_WIDTH
    row = lambda v: v.reshape(DEPTH, 1, -1)

    w_in_p = jnp.pad(w_in.astype(BF16), ((0, 0), (0, 0), (0, W_IN_COLS - w_in.shape[-1])))
    bf = row(jnp.pad(fox_fgate_b, ((0, 0), (0, LANES - FOX_HEADS))))

    sgu_wcat = jnp.transpose(sgu_w, (0, 2, 1, 3)).reshape(DEPTH, SGU_CHUNK, SGU_HEADS * SGU_CHUNK)
    sgu_bias = jnp.repeat(jnp.transpose(sgu_b, (0, 2, 1)), w // SGU_HEADS, axis=2)

    abar_re, abar_im, bbar_re, bbar_im = _s5_discretize(s5_lambda_re, s5_lambda_im, s5_log_dt, s5_b_re, s5_b_im)
    swap = lambda t: jnp.transpose(t, (0, 1, 3, 2))
    bbd = jnp.concatenate([_block_diag(swap(bbar_re)), _block_diag(swap(bbar_im))], axis=2).astype(BF16)
    cbd = jnp.concatenate([_block_diag(swap(s5_c_re)), -_block_diag(swap(s5_c_im))], axis=1).astype(BF16)
    lam = jnp.concatenate([abar_re.reshape(DEPTH, SUBLANES, LANES), abar_im.reshape(DEPTH, SUBLANES, LANES)], axis=1)
    glu_w = s5_glu_w.astype(BF16)

    wa_bd = _block_diag(lru_wa).astype(BF16)
    wx_bd = _block_diag(lru_wx).astype(BF16)

    w_o_pad = jnp.concatenate(
        [w_out[:, 0:3 * w],
         jnp.pad(w_out[:, 3 * w:].reshape(DEPTH, FOX_HEADS, FOX_HEAD_DIM, D_MODEL),
                 ((0, 0), (0, 0), (0, HEAD_PAD - FOX_HEAD_DIM), (0, 0))).reshape(DEPTH, FOX_HEADS * HEAD_PAD, D_MODEL)],
        axis=1).astype(BF16)
    g_mix = row(mix_norm_g[:, 0:3 * w])
    g_mix_d = _pad_heads(row(mix_norm_g[:, 3 * w:]))
    w1 = w_mlp_in.astype(BF16)
    w2 = w_mlp_out.astype(BF16)
    g1, g2 = row(norm1_g), row(norm2_g)
    fg = final_g.reshape(1, D_MODEL)

    xf = x.reshape(N_TOK, D_MODEL)
    for l in range(DEPTH):
        zmix, qkv, zf = _in_proj(xf, g1, w_in_p, l)
        zmix3 = zmix.reshape(BATCH, SEQ, ZMIX_COLS)
        qadd, kadd = _fcum(zf.reshape(BATCH, SEQ, LANES), bf, l)
        qadd = qadd.reshape(N_TOK, FOX_HEADS * HEAD_PAD)
        kadd = kadd.reshape(N_TOK, FOX_HEADS * HEAD_PAD)
        y_a = _sgu(zmix, row(sgu_norm_g), sgu_wcat, sgu_bias, l)
        y_b = _s5(zmix3, bbd, lam, cbd, row(s5_d), glu_w, row(s5_glu_b), l)
        y_c = _lru(zmix3, lru_conv_w, row(lru_conv_b), wa_bd, row(lru_ba.reshape(DEPTH, w)),
                   wx_bd, row(lru_bx.reshape(DEPTH, w)), row(lru_lambda), l)
        y_d = _attn(qkv, qadd, kadd)
        xf = _merge_mlp(y_a, y_b.reshape(N_TOK, w), y_c.reshape(N_TOK, w), y_d, xf, g_mix, g_mix_d, w_o_pad,
                        g2, w1, w2, fg, l, final_norm=(l == DEPTH - 1))
    return xf.reshape(BATCH, SEQ, D_MODEL)
```

```python
import functools

import jax
import jax.numpy as jnp
import numpy as np
from jax import lax
from jax.experimental import pallas as pl
from jax.experimental.pallas import tpu as pltpu

D_MODEL = 1024
BATCH = 4
SEQ = 4096
DEPTH = 4
N_TOK = BATCH * SEQ
MIXER_WIDTH = 256
SGU_HEADS = 4
SGU_CHUNK = 128
S5_GROUP = 16
S5_GROUPS = 16
S5_STATE = 64
LRU_HEADS = 4
LRU_CONV = 4
LRU_C = 8.0
FOX_HEADS = 4
FOX_HEAD_DIM = 64
D_FF = 4 * D_MODEL
RMS_EPS = 1e-6
LOG2E = 1.4426950408889634

LANES = 128
SUBLANES = 8
HEAD_PAD = LANES
ZMIX_COLS = 3 * MIXER_WIDTH
W_IN_COLS = 8 * MIXER_WIDTH + LANES
QKV_COLS = 3 * FOX_HEADS * HEAD_PAD
S5_NSTATE = S5_GROUPS * S5_STATE
S5_ROWS = 2 * S5_NSTATE // LANES
SCAN_ROW_GAP = 4
S5_PITCH = S5_ROWS + SCAN_ROW_GAP

TM_IN = 1024
TM_MLP = 1024
TF_MLP = 1024
T_SCAN = 256
T_CUM = 256
TQ = 512
TK = 512
ATTN_HEADS_PER_STEP = 4
VMEM_LIMIT = 48 * 1024 * 1024

F32 = jnp.float32
BF16 = jnp.bfloat16
MIX_OUT = BF16


def _params(n_axes):
    return pltpu.CompilerParams(dimension_semantics=("arbitrary",) * n_axes,
                                vmem_limit_bytes=VMEM_LIMIT)


def _layer_block(shape, layer):
    zeros = (0,) * len(shape)
    return pl.BlockSpec((None,) + tuple(shape), lambda *_: (layer,) + zeros)


def _rms(x, g, width):
    ms = jnp.sum(jnp.square(x), axis=-1, keepdims=True) * (1.0 / width)
    return x * lax.rsqrt(ms + RMS_EPS) * g


def _softplus(x):
    return jnp.maximum(x, 0.0) + jnp.log1p(jnp.exp(-jnp.abs(x)))


def _sgu_mix(zu, zv, g, w, bias, o_ref):
    n_rows = zu.shape[0]
    u = jax.nn.gelu(zu)
    v = _rms(jax.nn.gelu(zv), g, MIXER_WIDTH)
    hd = MIXER_WIDTH // SGU_HEADS
    lane_head = lax.broadcasted_iota(jnp.int32, (SGU_CHUNK, MIXER_WIDTH), 1) // hd
    t_idx = lax.broadcasted_iota(jnp.int32, (SGU_CHUNK, SGU_HEADS * SGU_CHUNK), 0)
    s_idx = lax.broadcasted_iota(jnp.int32, (SGU_CHUNK, SGU_HEADS * SGU_CHUNK), 1) % SGU_CHUNK
    wm = jnp.where(s_idx <= t_idx, w, 0.0).astype(BF16)
    for c in range(n_rows // SGU_CHUNK):
        rows = slice(c * SGU_CHUNK, (c + 1) * SGU_CHUNK)
        vc = v[rows]
        vstack = jnp.concatenate(
            [jnp.where(lane_head == h, vc, 0.0) for h in range(SGU_HEADS)], axis=0).astype(BF16)
        mixed = jnp.dot(wm, vstack, preferred_element_type=F32) + bias
        o_ref[rows, :] = (u[rows] * mixed).astype(MIX_OUT)


def _in_proj_kernel(x_ref, g_ref, w_ref, sg_ref, sw_ref, sb_ref, ya_ref, zmix_ref, qkv_ref, zf_ref):
    w = MIXER_WIDTH
    h = _rms(x_ref[...], g_ref[...], D_MODEL).astype(BF16)
    za = jnp.dot(h, w_ref[:, 0:2 * w], preferred_element_type=F32)
    _sgu_mix(za[:, 0:w], za[:, w:2 * w], sg_ref[...], sw_ref[...], sb_ref[...], ya_ref)
    zmix_ref[...] = jnp.dot(h, w_ref[:, 2 * w:5 * w], preferred_element_type=F32)
    zqkv = jnp.dot(h, w_ref[:, 5 * w:8 * w], preferred_element_type=F32)
    pad = jnp.zeros((TM_IN, HEAD_PAD - FOX_HEAD_DIM), F32)
    for slot in range(3 * FOX_HEADS):
        head = zqkv[:, slot * FOX_HEAD_DIM:(slot + 1) * FOX_HEAD_DIM]
        qkv_ref[:, slot * HEAD_PAD:(slot + 1) * HEAD_PAD] = jnp.concatenate([head, pad], axis=1).astype(BF16)
    zf_ref[...] = jnp.dot(h, w_ref[:, 8 * w:W_IN_COLS], preferred_element_type=F32)


def _in_proj(x, g, w, sgu_g, sgu_w, sgu_b, layer):
    row = lambda i: (i, 0)
    return pl.pallas_call(
        _in_proj_kernel,
        grid=(N_TOK // TM_IN,),
        in_specs=[pl.BlockSpec((TM_IN, D_MODEL), row),
                  _layer_block((1, D_MODEL), layer),
                  _layer_block((D_MODEL, W_IN_COLS), layer),
                  _layer_block((1, MIXER_WIDTH), layer),
                  _layer_block((SGU_CHUNK, SGU_HEADS * SGU_CHUNK), layer),
                  _layer_block((SGU_CHUNK, MIXER_WIDTH), layer)],
        out_specs=[pl.BlockSpec((TM_IN, MIXER_WIDTH), row),
                   pl.BlockSpec((TM_IN, ZMIX_COLS), row),
                   pl.BlockSpec((TM_IN, QKV_COLS), row),
                   pl.BlockSpec((TM_IN, LANES), row)],
        out_shape=[jax.ShapeDtypeStruct((N_TOK, MIXER_WIDTH), MIX_OUT),
                   jax.ShapeDtypeStruct((N_TOK, ZMIX_COLS), F32),
                   jax.ShapeDtypeStruct((N_TOK, QKV_COLS), BF16),
                   jax.ShapeDtypeStruct((N_TOK, LANES), F32)],
        compiler_params=_params(1),
        name="in_proj_sgu",
    )(x, g, w, sgu_g, sgu_w, sgu_b)


def _split3(x):
    hi = x.astype(BF16)
    r1 = x - hi.astype(F32)
    mid = r1.astype(BF16)
    lo = (r1 - mid.astype(F32)).astype(BF16)
    return hi, mid, lo


def _bias_placement():
    pq = np.zeros((LANES, FOX_HEADS * HEAD_PAD), np.float32)
    pk = np.zeros_like(pq)
    ones_q = np.zeros((1, FOX_HEADS * HEAD_PAD), np.float32)
    ones_k = np.zeros_like(ones_q)
    for h in range(FOX_HEADS):
        for piece in range(3):
            pq[piece * FOX_HEADS + h, h * HEAD_PAD + FOX_HEAD_DIM + piece] = 1.0
            pk[piece * FOX_HEADS + h, h * HEAD_PAD + FOX_HEAD_DIM + 3 + piece] = -1.0
            ones_q[0, h * HEAD_PAD + FOX_HEAD_DIM + 3 + piece] = 1.0
            ones_k[0, h * HEAD_PAD + FOX_HEAD_DIM + piece] = 1.0
    return (jnp.asarray(pq, BF16), jnp.asarray(pk, BF16), jnp.asarray(ones_q), jnp.asarray(ones_k))


def _fcum_kernel(zf_ref, bf_ref, tril_ref, pq_ref, pk_ref, oq_ref, ok_ref, qadd_ref, kadd_ref, carry_ref):
    @pl.when(pl.program_id(0) == 0)
    def _():
        carry_ref[...] = jnp.zeros_like(carry_ref)

    tril = tril_ref[...]
    lane = lax.broadcasted_iota(jnp.int32, (T_CUM, LANES), 1)
    for b in range(BATCH):
        logit = zf_ref[b] + bf_ref[...]
        log_f = -_softplus(-logit)
        hi, mid, lo = _split3(log_f)
        cs = (jnp.dot(tril, hi, preferred_element_type=F32)
              + jnp.dot(tril, mid, preferred_element_type=F32)
              + jnp.dot(tril, lo, preferred_element_type=F32))
        cum = cs + carry_ref[b]
        carry_ref[b] = cum[T_CUM - 1:T_CUM, :]
        hi, mid, lo = [p.astype(F32) for p in _split3(cum * LOG2E)]
        pieces = jnp.where(lane < FOX_HEADS, hi,
                           jnp.where(lane < 2 * FOX_HEADS, pltpu.roll(mid, FOX_HEADS, axis=1),
                                     jnp.where(lane < 3 * FOX_HEADS, pltpu.roll(lo, 2 * FOX_HEADS, axis=1), 0.0))
                           ).astype(BF16)
        qadd_ref[b] = (jnp.dot(pieces, pq_ref[...], preferred_element_type=F32) + oq_ref[...]).astype(BF16)
        kadd_ref[b] = (jnp.dot(pieces, pk_ref[...], preferred_element_type=F32) + ok_ref[...]).astype(BF16)


def _fcum(zf3, bf, layer):
    pq, pk, ones_q, ones_k = _bias_placement()
    tril = jnp.asarray(np.tril(np.ones((T_CUM, T_CUM), np.float32)), BF16)
    full = lambda c: (0, 0)
    wide = FOX_HEADS * HEAD_PAD
    chunk = lambda c: (0, c, 0)
    return pl.pallas_call(
        _fcum_kernel,
        grid=(SEQ // T_CUM,),
        in_specs=[pl.BlockSpec((BATCH, T_CUM, LANES), chunk),
                  _layer_block((1, LANES), layer),
                  pl.BlockSpec((T_CUM, T_CUM), full),
                  pl.BlockSpec((LANES, wide), full),
                  pl.BlockSpec((LANES, wide), full),
                  pl.BlockSpec((1, wide), full),
                  pl.BlockSpec((1, wide), full)],
        out_specs=[pl.BlockSpec((BATCH, T_CUM, wide), chunk),
                   pl.BlockSpec((BATCH, T_CUM, wide), chunk)],
        out_shape=[jax.ShapeDtypeStruct((BATCH, SEQ, wide), BF16),
                   jax.ShapeDtypeStruct((BATCH, SEQ, wide), BF16)],
        scratch_shapes=[pltpu.VMEM((BATCH, 1, LANES), F32)],
        compiler_params=_params(1),
        name="forget_cumsum",
    )(zf3, bf, tril, pq, pk, ones_q, ones_k)


def _s5_kernel(u_ref, bbd_ref, lam_ref, cbd_ref, d_ref, gw_ref, gb_ref, o_ref, s_ref, h_ref):
    @pl.when(pl.program_id(0) == 0)
    def _():
        h_ref[...] = jnp.zeros_like(h_ref)

    for b in range(BATCH):
        bu = jnp.dot(u_ref[b].astype(BF16), bbd_ref[...], preferred_element_type=F32)
        for j in range(S5_ROWS):
            s_ref[b, pl.ds(j, T_SCAN, stride=S5_PITCH), :] = bu[:, j * LANES:(j + 1) * LANES]

    lam_re = lam_ref[0:SUBLANES, :]
    lam_im = lam_ref[SUBLANES:S5_ROWS, :]

    def step(t, carry):
        base = t * S5_PITCH
        new = []
        for b in range(BATCH):
            h_re, h_im = carry[2 * b], carry[2 * b + 1]
            n_re = lam_re * h_re - lam_im * h_im + s_ref[b, pl.ds(base, SUBLANES), :]
            n_im = lam_re * h_im + lam_im * h_re + s_ref[b, pl.ds(base + SUBLANES, SUBLANES), :]
            s_ref[b, pl.ds(base, SUBLANES), :] = n_re
            s_ref[b, pl.ds(base + SUBLANES, SUBLANES), :] = n_im
            new += [n_re, n_im]
        return tuple(new)

    init = []
    for b in range(BATCH):
        init += [h_ref[b, 0:SUBLANES, :], h_ref[b, SUBLANES:S5_ROWS, :]]
    fin = lax.fori_loop(0, T_SCAN, step, tuple(init), unroll=8)
    for b in range(BATCH):
        h_ref[b, 0:SUBLANES, :] = fin[2 * b]
        h_ref[b, SUBLANES:S5_ROWS, :] = fin[2 * b + 1]

    states = jnp.concatenate(
        [jnp.concatenate([s_ref[b, pl.ds(j, T_SCAN, stride=S5_PITCH), :].astype(BF16) for j in range(S5_ROWS)],
                         axis=1) for b in range(BATCH)], axis=0)
    u = u_ref[...].reshape(BATCH * T_SCAN, MIXER_WIDTH)
    y = jnp.dot(states, cbd_ref[...], preferred_element_type=F32) + d_ref[...] * u
    y = jax.nn.gelu(y)
    gate = jnp.dot(y.astype(BF16), gw_ref[...], preferred_element_type=F32) + gb_ref[...]
    o_ref[...] = (y * jax.nn.sigmoid(gate)).astype(MIX_OUT).reshape(BATCH, T_SCAN, MIXER_WIDTH)


def _s5(zmix3, bbd, lam, cbd, d, gw, gb, layer):
    return pl.pallas_call(
        _s5_kernel,
        grid=(SEQ // T_SCAN,),
        in_specs=[pl.BlockSpec((BATCH, T_SCAN, MIXER_WIDTH), lambda c: (0, c, 0)),
                  _layer_block((MIXER_WIDTH, 2 * S5_NSTATE), layer),
                  _layer_block((S5_ROWS, LANES), layer),
                  _layer_block((2 * S5_NSTATE, MIXER_WIDTH), layer),
                  _layer_block((1, MIXER_WIDTH), layer),
                  _layer_block((MIXER_WIDTH, MIXER_WIDTH), layer),
                  _layer_block((1, MIXER_WIDTH), layer)],
        out_specs=pl.BlockSpec((BATCH, T_SCAN, MIXER_WIDTH), lambda c: (0, c, 0)),
        out_shape=jax.ShapeDtypeStruct((BATCH, SEQ, MIXER_WIDTH), MIX_OUT),
        scratch_shapes=[pltpu.VMEM((BATCH, T_SCAN * S5_PITCH, LANES), F32),
                        pltpu.VMEM((BATCH, S5_ROWS, LANES), F32)],
        compiler_params=_params(1),
        name="s5",
    )(zmix3, bbd, lam, cbd, d, gw, gb)


LRU_SLOTS = BATCH * MIXER_WIDTH // LANES
LRU_PITCH = LRU_SLOTS + SCAN_ROW_GAP


def _lru_kernel(x_ref, gate_ref, cw_ref, cb_ref, wa_ref, ba_ref, wx_ref, bx_ref, lam_ref, o_ref,
                tail_ref, a_ref, b_ref, h_ref):
    @pl.when(pl.program_id(0) == 0)
    def _():
        tail_ref[...] = jnp.zeros_like(tail_ref)
        h_ref[...] = jnp.zeros_like(h_ref)

    n_tiles = MIXER_WIDTH // LANES
    decay_rate = LRU_C * _softplus(-lam_ref[...])
    for b in range(BATCH):
        x = x_ref[b]
        xp = jnp.concatenate([tail_ref[b], x], axis=0)
        tail_ref[b] = x[T_SCAN - SUBLANES:T_SCAN, :]
        xc = cb_ref[...]
        for k in range(LRU_CONV):
            off = SUBLANES - (LRU_CONV - 1) + k
            xc = xc + cw_ref[k:k + 1, :] * xp[off:off + T_SCAN, :]
        xcb = xc.astype(BF16)
        r = jax.nn.sigmoid(jnp.dot(xcb, wa_ref[...], preferred_element_type=F32) + ba_ref[...])
        i = jax.nn.sigmoid(jnp.dot(xcb, wx_ref[...], preferred_element_type=F32) + bx_ref[...])
        log_a = -(r * decay_rate)
        a = jnp.exp(log_a)
        inp = jnp.sqrt(-jnp.tanh(log_a) * (a * a + 1.0)) * (i * xc)
        for j in range(n_tiles):
            slot = b * n_tiles + j
            a_ref[pl.ds(slot, T_SCAN, stride=LRU_PITCH), :] = a[:, j * LANES:(j + 1) * LANES]
            b_ref[pl.ds(slot, T_SCAN, stride=LRU_PITCH), :] = inp[:, j * LANES:(j + 1) * LANES]

    def step(t, h):
        base = t * LRU_PITCH
        h = a_ref[pl.ds(base, LRU_SLOTS), :] * h + b_ref[pl.ds(base, LRU_SLOTS), :]
        b_ref[pl.ds(base, LRU_SLOTS), :] = h
        return h

    h_ref[...] = lax.fori_loop(0, T_SCAN, step, h_ref[...], unroll=8)

    for b in range(BATCH):
        h = jnp.concatenate(
            [b_ref[pl.ds(b * n_tiles + j, T_SCAN, stride=LRU_PITCH), :] for j in range(n_tiles)],
            axis=1)
        o_ref[b] = (h * jax.nn.gelu(gate_ref[b])).astype(MIX_OUT)


def _lru(zmix3, cw, cb, wa, ba, wx, bx, lam, layer):
    vec = _layer_block((1, MIXER_WIDTH), layer)
    mat = _layer_block((MIXER_WIDTH, MIXER_WIDTH), layer)
    return pl.pallas_call(
        _lru_kernel,
        grid=(SEQ // T_SCAN,),
        in_specs=[pl.BlockSpec((BATCH, T_SCAN, MIXER_WIDTH), lambda c: (0, c, 1)),
                  pl.BlockSpec((BATCH, T_SCAN, MIXER_WIDTH), lambda c: (0, c, 2)),
                  _layer_block((LRU_CONV, MIXER_WIDTH), layer), vec, mat, vec, mat, vec, vec],
        out_specs=pl.BlockSpec((BATCH, T_SCAN, MIXER_WIDTH), lambda c: (0, c, 0)),
        out_shape=jax.ShapeDtypeStruct((BATCH, SEQ, MIXER_WIDTH), MIX_OUT),
        scratch_shapes=[pltpu.VMEM((BATCH, SUBLANES, MIXER_WIDTH), F32),
                        pltpu.VMEM((T_SCAN * LRU_PITCH, LANES), F32),
                        pltpu.VMEM((T_SCAN * LRU_PITCH, LANES), F32),
                        pltpu.VMEM((LRU_SLOTS, LANES), F32)],
        compiler_params=_params(1),
        name="rglru",
    )(zmix3, zmix3, cw, cb, wa, ba, wx, bx, lam)


def _attn_kernel(q_ref, qadd_ref, k_ref, kadd_ref, v_ref, o_ref, s_ref, m_ref, acc_ref):
    i = pl.program_id(2)
    scale = FOX_HEAD_DIM ** -0.5 * LOG2E
    neg = jnp.finfo(F32).min
    denom_lane = lax.broadcasted_iota(jnp.int32, (TK, HEAD_PAD), 1) == FOX_HEAD_DIM
    slots = [slice(n * HEAD_PAD, (n + 1) * HEAD_PAD) for n in range(ATTN_HEADS_PER_STEP)]
    qs = [(q_ref[:, sl].astype(F32) * scale + qadd_ref[:, sl].astype(F32)).astype(BF16) for sl in slots]

    def logits(n, j):
        start = pl.multiple_of(j * TK, TK)
        ks = k_ref[pl.ds(start, TK), slots[n]] + kadd_ref[pl.ds(start, TK), slots[n]]
        return lax.dot_general(qs[n], ks, (((1,), (1,)), ((), ())), preferred_element_type=F32)

    def block(j, masked, prefetch):
        start = pl.multiple_of(j * TK, TK)
        for n, sl in enumerate(slots):
            s = s_ref[n]
            vs = jnp.where(denom_lane, 1.0, v_ref[pl.ds(start, TK), sl]).astype(BF16)
            if masked:
                rr = lax.broadcasted_iota(jnp.int32, (TQ, TK), 0)
                cc = lax.broadcasted_iota(jnp.int32, (TQ, TK), 1)
                s = jnp.where(cc <= rr, s, neg)
            m = m_ref[n]
            m_new = jnp.maximum(m, jnp.max(s, axis=1, keepdims=True))
            alpha = jnp.exp2(m - m_new)
            p = jnp.exp2(s - jnp.concatenate([m_new] * (TK // LANES), axis=1))
            m_ref[n] = m_new
            acc_ref[n] = alpha * acc_ref[n] + jnp.dot(p.astype(BF16), vs, preferred_element_type=F32)
            if prefetch:
                s_ref[n] = logits(n, j + 1)

    for n in range(ATTN_HEADS_PER_STEP):
        s_ref[n] = logits(n, 0)
    m_ref[...] = jnp.full(m_ref.shape, neg, F32)
    acc_ref[...] = jnp.zeros_like(acc_ref)

    @pl.loop(0, i // 2)
    def _(jj):
        block(2 * jj, False, True)
        block(2 * jj + 1, False, True)

    @pl.when(i % 2 == 1)
    def _():
        block(i - 1, False, True)

    block(i, True, False)
    out_lane = lax.broadcasted_iota(jnp.int32, (TQ, HEAD_PAD), 1) < FOX_HEAD_DIM
    for n, sl in enumerate(slots):
        acc = acc_ref[n]
        o_ref[:, sl] = jnp.where(out_lane, acc / acc[:, FOX_HEAD_DIM:FOX_HEAD_DIM + 1], 0.0).astype(MIX_OUT)


def _attn(qkv, qadd, kadd):
    n_q = SEQ // TQ
    n_hg = FOX_HEADS // ATTN_HEADS_PER_STEP
    width = ATTN_HEADS_PER_STEP * HEAD_PAD
    qrow = lambda b, h, i: (b * n_q + i, h)
    return pl.pallas_call(
        _attn_kernel,
        grid=(BATCH, n_hg, n_q),
        in_specs=[pl.BlockSpec((TQ, width), qrow),
                  pl.BlockSpec((TQ, width), qrow),
                  pl.BlockSpec((SEQ, width), lambda b, h, i: (b, n_hg + h)),
                  pl.BlockSpec((SEQ, width), lambda b, h, i: (b, h)),
                  pl.BlockSpec((SEQ, width), lambda b, h, i: (b, 2 * n_hg + h))],
        out_specs=pl.BlockSpec((TQ, width), qrow),
        out_shape=jax.ShapeDtypeStruct((N_TOK, FOX_HEADS * HEAD_PAD), MIX_OUT),
        scratch_shapes=[pltpu.VMEM((ATTN_HEADS_PER_STEP, TQ, TK), F32),
                        pltpu.VMEM((ATTN_HEADS_PER_STEP, TQ, LANES), F32),
                        pltpu.VMEM((ATTN_HEADS_PER_STEP, TQ, HEAD_PAD), F32)],
        compiler_params=_params(3),
        name="fox_attention",
    )(qkv, qadd, qkv, kadd, qkv)


def _merge_mlp_kernel(ya_ref, yb_ref, yc_ref, yd_ref, x_ref, gm_ref, gmd_ref, wo_ref, g2_ref, w1_ref, w2_ref,
                      fg_ref, o_ref, h_ref, *, final_norm):
    j = pl.program_id(1)

    @pl.when(j == 0)
    def _():
        w = MIXER_WIDTH
        parts = [_rms(ya_ref[...].astype(F32), gm_ref[:, 0:w], w),
                 _rms(yb_ref[...].astype(F32), gm_ref[:, w:2 * w], w),
                 _rms(yc_ref[...].astype(F32), gm_ref[:, 2 * w:3 * w], w),
                 _rms(yd_ref[...].astype(F32), gmd_ref[...], w)]
        y = jnp.concatenate(parts, axis=1).astype(BF16)
        x1 = x_ref[...] + jnp.dot(y, wo_ref[...], preferred_element_type=F32)
        o_ref[...] = x1
        h_ref[...] = _rms(x1, g2_ref[...], D_MODEL).astype(BF16)

    a = jnp.dot(h_ref[...], w1_ref[...], preferred_element_type=F32)
    a = jnp.square(jnp.maximum(a, 0.0)).astype(BF16)
    o_ref[...] += jnp.dot(a, w2_ref[...], preferred_element_type=F32)

    if final_norm:
        @pl.when(j == pl.num_programs(1) - 1)
        def _():
            o_ref[...] = _rms(o_ref[...], fg_ref[...], D_MODEL)


def _merge_mlp(ya, yb, yc, yd, x, gm, gmd, wo, g2, w1, w2, fg, layer, final_norm):
    row = lambda i, j: (i, 0)
    k_dim = 3 * MIXER_WIDTH + FOX_HEADS * HEAD_PAD
    mix = pl.BlockSpec((TM_MLP, MIXER_WIDTH), row)
    return pl.pallas_call(
        functools.partial(_merge_mlp_kernel, final_norm=final_norm),
        grid=(N_TOK // TM_MLP, D_FF // TF_MLP),
        in_specs=[mix, mix, mix,
                  pl.BlockSpec((TM_MLP, FOX_HEADS * HEAD_PAD), row),
                  pl.BlockSpec((TM_MLP, D_MODEL), row),
                  _layer_block((1, 3 * MIXER_WIDTH), layer),
                  _layer_block((1, FOX_HEADS * HEAD_PAD), layer),
                  _layer_block((k_dim, D_MODEL), layer),
                  _layer_block((1, D_MODEL), layer),
                  pl.BlockSpec((None, D_MODEL, TF_MLP), lambda i, j: (layer, 0, j)),
                  pl.BlockSpec((None, TF_MLP, D_MODEL), lambda i, j: (layer, j, 0)),
                  pl.BlockSpec((1, D_MODEL), lambda i, j: (0, 0))],
        out_specs=pl.BlockSpec((TM_MLP, D_MODEL), row),
        out_shape=jax.ShapeDtypeStruct((N_TOK, D_MODEL), F32),
        scratch_shapes=[pltpu.VMEM((TM_MLP, D_MODEL), BF16)],
        compiler_params=_params(2),
        name="merge_mlp",
    )(ya, yb, yc, yd, x, gm, gmd, wo, g2, w1, w2, fg)


def _pad_heads(w):
    lead = w.shape[:-1]
    w = w.reshape(*lead, FOX_HEADS, FOX_HEAD_DIM)
    w = jnp.pad(w, [(0, 0)] * (len(lead) + 1) + [(0, HEAD_PAD - FOX_HEAD_DIM)])
    return w.reshape(*lead, FOX_HEADS * HEAD_PAD)


def _block_diag(blocks):
    n, g, r, c = blocks.shape
    eye = jnp.eye(g, dtype=blocks.dtype)
    return jnp.einsum('ngrc,gh->ngrhc', blocks, eye).reshape(n, g * r, g * c)


def _s5_discretize(lam_re, lam_im, log_dt, b_re, b_im):
    dt = jnp.exp(log_dt)[..., None]
    mag = jnp.exp(lam_re * dt)
    abar_re = mag * jnp.cos(lam_im * dt)
    abar_im = mag * jnp.sin(lam_im * dt)
    denom = jnp.square(lam_re) + jnp.square(lam_im)
    num_re = abar_re - 1.0
    num_im = abar_im
    fac_re = (num_re * lam_re + num_im * lam_im) / denom
    fac_im = (num_im * lam_re - num_re * lam_im) / denom
    bbar_re = fac_re[..., None] * b_re - fac_im[..., None] * b_im
    bbar_im = fac_re[..., None] * b_im + fac_im[..., None] * b_re
    return abar_re, abar_im, bbar_re, bbar_im


def kernel(x, norm1_g, w_in, sgu_norm_g, sgu_w, sgu_b, s5_lambda_re, s5_lambda_im, s5_log_dt, s5_b_re, s5_b_im, s5_c_re, s5_c_im, s5_d, s5_glu_w, s5_glu_b, lru_conv_w, lru_conv_b, lru_wa, lru_ba, lru_wx, lru_bx, lru_lambda, fox_fgate_b, mix_norm_g, w_out, norm2_g, w_mlp_in, w_mlp_out, final_g):
    w = MIXER_WIDTH
    row = lambda v: v.reshape(DEPTH, 1, -1)

    w_in_p = jnp.pad(w_in.astype(BF16), ((0, 0), (0, 0), (0, W_IN_COLS - w_in.shape[-1])))
    bf = row(jnp.pad(fox_fgate_b, ((0, 0), (0, LANES - FOX_HEADS))))

    sgu_wcat = jnp.transpose(sgu_w, (0, 2, 1, 3)).reshape(DEPTH, SGU_CHUNK, SGU_HEADS * SGU_CHUNK)
    sgu_bias = jnp.repeat(jnp.transpose(sgu_b, (0, 2, 1)), w // SGU_HEADS, axis=2)

    abar_re, abar_im, bbar_re, bbar_im = _s5_discretize(s5_lambda_re, s5_lambda_im, s5_log_dt, s5_b_re, s5_b_im)
    swap = lambda t: jnp.transpose(t, (0, 1, 3, 2))
    bbd = jnp.concatenate([_block_diag(swap(bbar_re)), _block_diag(swap(bbar_im))], axis=2).astype(BF16)
    cbd = jnp.concatenate([_block_diag(swap(s5_c_re)), -_block_diag(swap(s5_c_im))], axis=1).astype(BF16)
    lam = jnp.concatenate([abar_re.reshape(DEPTH, SUBLANES, LANES), abar_im.reshape(DEPTH, SUBLANES, LANES)], axis=1)
    glu_w = s5_glu_w.astype(BF16)

    wa_bd = _block_diag(lru_wa).astype(BF16)
    wx_bd = _block_diag(lru_wx).astype(BF16)

    w_o_pad = jnp.concatenate(
        [w_out[:, 0:3 * w],
         jnp.pad(w_out[:, 3 * w:].reshape(DEPTH, FOX_HEADS, FOX_HEAD_DIM, D_MODEL),
                 ((0, 0), (0, 0), (0, HEAD_PAD - FOX_HEAD_DIM), (0, 0))).reshape(DEPTH, FOX_HEADS * HEAD_PAD, D_MODEL)],
        axis=1).astype(BF16)
    g_mix = row(mix_norm_g[:, 0:3 * w])
    g_mix_d = _pad_heads(row(mix_norm_g[:, 3 * w:]))
    w1 = w_mlp_in.astype(BF16)
    w2 = w_mlp_out.astype(BF16)
    g1, g2 = row(norm1_g), row(norm2_g)
    fg = final_g.reshape(1, D_MODEL)

    xf = x.reshape(N_TOK, D_MODEL)
    for l in range(DEPTH):
        y_a, zmix, qkv, zf = _in_proj(xf, g1, w_in_p, row(sgu_norm_g), sgu_wcat, sgu_bias, l)
        zmix3 = zmix.reshape(BATCH, SEQ, ZMIX_COLS)
        qadd, kadd = _fcum(zf.reshape(BATCH, SEQ, LANES), bf, l)
        qadd = qadd.reshape(N_TOK, FOX_HEADS * HEAD_PAD)
        kadd = kadd.reshape(N_TOK, FOX_HEADS * HEAD_PAD)
        y_b = _s5(zmix3, bbd, lam, cbd, row(s5_d), glu_w, row(s5_glu_b), l)
        y_c = _lru(zmix3, lru_conv_w, row(lru_conv_b), wa_bd, row(lru_ba.reshape(DEPTH, w)),
                   wx_bd, row(lru_bx.reshape(DEPTH, w)), row(lru_lambda), l)
        y_d = _attn(qkv, qadd, kadd)
        xf = _merge_mlp(y_a, y_b.reshape(N_TOK, w), y_c.reshape(N_TOK, w), y_d, xf, g_mix, g_mix_d, w_o_pad,
                        g2, w1, w2, fg, l, final_norm=(l == DEPTH - 1))
    return xf.reshape(BATCH, SEQ, D_MODEL)
```

```python
import functools

import jax
import jax.numpy as jnp
import numpy as np
from jax import lax
from jax.experimental import pallas as pl
from jax.experimental.pallas import tpu as pltpu

D_MODEL = 1024
BATCH = 4
SEQ = 4096
DEPTH = 4
N_TOK = BATCH * SEQ
MIXER_WIDTH = 256
SGU_HEADS = 4
SGU_CHUNK = 128
S5_GROUP = 16
S5_GROUPS = 16
S5_STATE = 64
LRU_HEADS = 4
LRU_CONV = 4
LRU_C = 8.0
FOX_HEADS = 4
FOX_HEAD_DIM = 64
D_FF = 4 * D_MODEL
RMS_EPS = 1e-6
LOG2E = 1.4426950408889634

LANES = 128
SUBLANES = 8
HEAD_PAD = LANES
ZMIX_COLS = 3 * MIXER_WIDTH
W_IN_COLS = 8 * MIXER_WIDTH + LANES
QKV_COLS = 3 * FOX_HEADS * HEAD_PAD
S5_NSTATE = S5_GROUPS * S5_STATE
S5_ROWS = 2 * S5_NSTATE // LANES
SCAN_ROW_GAP = 4
S5_PITCH = S5_ROWS + SCAN_ROW_GAP

TM_IN = 1024
TM_MLP = 1024
TF_MLP = 1024
T_SCAN = 256
T_CUM = 256
TQ = 512
TK = 512
ACC_ROWS = 80
VMEM_LIMIT = 48 * 1024 * 1024

F32 = jnp.float32
BF16 = jnp.bfloat16
MIX_OUT = BF16


def _params(n_axes):
    return pltpu.CompilerParams(dimension_semantics=("arbitrary",) * n_axes,
                                vmem_limit_bytes=VMEM_LIMIT)


def _layer_block(shape, layer):
    zeros = (0,) * len(shape)
    return pl.BlockSpec((None,) + tuple(shape), lambda *_: (layer,) + zeros)


def _rms(x, g, width):
    ms = jnp.sum(jnp.square(x), axis=-1, keepdims=True) * (1.0 / width)
    return x * lax.rsqrt(ms + RMS_EPS) * g


def _softplus(x):
    return jnp.maximum(x, 0.0) + jnp.log1p(jnp.exp(-jnp.abs(x)))


def _sgu_mix(zu, zv, g, w, bias, o_ref):
    n_rows = zu.shape[0]
    u = jax.nn.gelu(zu)
    v = _rms(jax.nn.gelu(zv), g, MIXER_WIDTH)
    hd = MIXER_WIDTH // SGU_HEADS
    lane_head = lax.broadcasted_iota(jnp.int32, (SGU_CHUNK, MIXER_WIDTH), 1) // hd
    t_idx = lax.broadcasted_iota(jnp.int32, (SGU_CHUNK, SGU_HEADS * SGU_CHUNK), 0)
    s_idx = lax.broadcasted_iota(jnp.int32, (SGU_CHUNK, SGU_HEADS * SGU_CHUNK), 1) % SGU_CHUNK
    wm = jnp.where(s_idx <= t_idx, w, 0.0).astype(BF16)
    for c in range(n_rows // SGU_CHUNK):
        rows = slice(c * SGU_CHUNK, (c + 1) * SGU_CHUNK)
        vc = v[rows]
        vstack = jnp.concatenate(
            [jnp.where(lane_head == h, vc, 0.0) for h in range(SGU_HEADS)], axis=0).astype(BF16)
        mixed = jnp.dot(wm, vstack, preferred_element_type=F32) + bias
        o_ref[rows, :] = (u[rows] * mixed).astype(MIX_OUT)


def _in_proj_kernel(x_ref, g_ref, w_ref, sg_ref, sw_ref, sb_ref, ya_ref, zmix_ref, qkv_ref, zf_ref):
    w = MIXER_WIDTH
    h = _rms(x_ref[...], g_ref[...], D_MODEL).astype(BF16)
    za = jnp.dot(h, w_ref[:, 0:2 * w], preferred_element_type=F32)
    _sgu_mix(za[:, 0:w], za[:, w:2 * w], sg_ref[...], sw_ref[...], sb_ref[...], ya_ref)
    zmix_ref[...] = jnp.dot(h, w_ref[:, 2 * w:5 * w], preferred_element_type=F32)
    zqkv = jnp.dot(h, w_ref[:, 5 * w:8 * w], preferred_element_type=F32)
    pad = jnp.zeros((TM_IN, HEAD_PAD - FOX_HEAD_DIM), F32)
    for slot in range(3 * FOX_HEADS):
        head = zqkv[:, slot * FOX_HEAD_DIM:(slot + 1) * FOX_HEAD_DIM]
        qkv_ref[:, slot * HEAD_PAD:(slot + 1) * HEAD_PAD] = jnp.concatenate([head, pad], axis=1).astype(BF16)
    zf_ref[...] = jnp.dot(h, w_ref[:, 8 * w:W_IN_COLS], preferred_element_type=F32)


def _in_proj(x, g, w, sgu_g, sgu_w, sgu_b, layer):
    row = lambda i: (i, 0)
    return pl.pallas_call(
        _in_proj_kernel,
        grid=(N_TOK // TM_IN,),
        in_specs=[pl.BlockSpec((TM_IN, D_MODEL), row),
                  _layer_block((1, D_MODEL), layer),
                  _layer_block((D_MODEL, W_IN_COLS), layer),
                  _layer_block((1, MIXER_WIDTH), layer),
                  _layer_block((SGU_CHUNK, SGU_HEADS * SGU_CHUNK), layer),
                  _layer_block((SGU_CHUNK, MIXER_WIDTH), layer)],
        out_specs=[pl.BlockSpec((TM_IN, MIXER_WIDTH), row),
                   pl.BlockSpec((TM_IN, ZMIX_COLS), row),
                   pl.BlockSpec((TM_IN, QKV_COLS), row),
                   pl.BlockSpec((TM_IN, LANES), row)],
        out_shape=[jax.ShapeDtypeStruct((N_TOK, MIXER_WIDTH), MIX_OUT),
                   jax.ShapeDtypeStruct((N_TOK, ZMIX_COLS), F32),
                   jax.ShapeDtypeStruct((N_TOK, QKV_COLS), BF16),
                   jax.ShapeDtypeStruct((N_TOK, LANES), F32)],
        compiler_params=_params(1),
        name="in_proj_sgu",
    )(x, g, w, sgu_g, sgu_w, sgu_b)


def _split3(x):
    hi = x.astype(BF16)
    r1 = x - hi.astype(F32)
    mid = r1.astype(BF16)
    lo = (r1 - mid.astype(F32)).astype(BF16)
    return hi, mid, lo


def _bias_placement():
    pq = np.zeros((LANES, FOX_HEADS * HEAD_PAD), np.float32)
    pk = np.zeros_like(pq)
    ones_q = np.zeros((1, FOX_HEADS * HEAD_PAD), np.float32)
    ones_k = np.zeros_like(ones_q)
    for h in range(FOX_HEADS):
        for piece in range(3):
            pq[piece * FOX_HEADS + h, h * HEAD_PAD + FOX_HEAD_DIM + piece] = 1.0
            pk[piece * FOX_HEADS + h, h * HEAD_PAD + FOX_HEAD_DIM + 3 + piece] = -1.0
            ones_q[0, h * HEAD_PAD + FOX_HEAD_DIM + 3 + piece] = 1.0
            ones_k[0, h * HEAD_PAD + FOX_HEAD_DIM + piece] = 1.0
    return (jnp.asarray(pq, BF16), jnp.asarray(pk, BF16), jnp.asarray(ones_q), jnp.asarray(ones_k))


def _fcum_kernel(zf_ref, bf_ref, v_ref, tril_ref, pq_ref, pk_ref, oq_ref, ok_ref, qadd_ref, kadd_ref, vt_ref,
                 carry_ref):
    @pl.when(pl.program_id(0) == 0)
    def _():
        carry_ref[...] = jnp.zeros_like(carry_ref)

    tril = tril_ref[...]
    lane = lax.broadcasted_iota(jnp.int32, (T_CUM, LANES), 1)
    ones_row = lax.broadcasted_iota(jnp.int32, (HEAD_PAD, T_CUM), 0) == FOX_HEAD_DIM
    for b in range(BATCH):
        for n in range(FOX_HEADS):
            v_t = v_ref[b, :, n * HEAD_PAD:(n + 1) * HEAD_PAD].astype(F32).T
            vt_ref[b, n * HEAD_PAD:(n + 1) * HEAD_PAD, :] = jnp.where(ones_row, 1.0, v_t).astype(BF16)
        logit = zf_ref[b] + bf_ref[...]
        log_f = -_softplus(-logit)
        hi, mid, lo = _split3(log_f)
        cs = (jnp.dot(tril, hi, preferred_element_type=F32)
              + jnp.dot(tril, mid, preferred_element_type=F32)
              + jnp.dot(tril, lo, preferred_element_type=F32))
        cum = cs + carry_ref[b]
        carry_ref[b] = cum[T_CUM - 1:T_CUM, :]
        hi, mid, lo = [p.astype(F32) for p in _split3(cum * LOG2E)]
        pieces = jnp.where(lane < FOX_HEADS, hi,
                           jnp.where(lane < 2 * FOX_HEADS, pltpu.roll(mid, FOX_HEADS, axis=1),
                                     jnp.where(lane < 3 * FOX_HEADS, pltpu.roll(lo, 2 * FOX_HEADS, axis=1), 0.0))
                           ).astype(BF16)
        qadd_ref[b] = (jnp.dot(pieces, pq_ref[...], preferred_element_type=F32) + oq_ref[...]).astype(BF16)
        kadd_ref[b] = (jnp.dot(pieces, pk_ref[...], preferred_element_type=F32) + ok_ref[...]).astype(BF16)


def _fcum(zf3, bf, qkv3, layer):
    pq, pk, ones_q, ones_k = _bias_placement()
    tril = jnp.asarray(np.tril(np.ones((T_CUM, T_CUM), np.float32)), BF16)
    full = lambda c: (0, 0)
    wide = FOX_HEADS * HEAD_PAD
    chunk = lambda c: (0, c, 0)
    return pl.pallas_call(
        _fcum_kernel,
        grid=(SEQ // T_CUM,),
        in_specs=[pl.BlockSpec((BATCH, T_CUM, LANES), chunk),
                  _layer_block((1, LANES), layer),
                  pl.BlockSpec((BATCH, T_CUM, wide), lambda c: (0, c, 2)),
                  pl.BlockSpec((T_CUM, T_CUM), full),
                  pl.BlockSpec((LANES, wide), full),
                  pl.BlockSpec((LANES, wide), full),
                  pl.BlockSpec((1, wide), full),
                  pl.BlockSpec((1, wide), full)],
        out_specs=[pl.BlockSpec((BATCH, T_CUM, wide), chunk),
                   pl.BlockSpec((BATCH, T_CUM, wide), chunk),
                   pl.BlockSpec((BATCH, wide, T_CUM), lambda c: (0, 0, c))],
        out_shape=[jax.ShapeDtypeStruct((BATCH, SEQ, wide), BF16),
                   jax.ShapeDtypeStruct((BATCH, SEQ, wide), BF16),
                   jax.ShapeDtypeStruct((BATCH, wide, SEQ), BF16)],
        scratch_shapes=[pltpu.VMEM((BATCH, 1, LANES), F32)],
        compiler_params=_params(1),
        name="forget_cumsum",
    )(zf3, bf, qkv3, tril, pq, pk, ones_q, ones_k)


def _s5_kernel(u_ref, bbd_ref, lam_ref, cbd_ref, d_ref, gw_ref, gb_ref, o_ref, s_ref, h_ref):
    @pl.when(pl.program_id(0) == 0)
    def _():
        h_ref[...] = jnp.zeros_like(h_ref)

    for b in range(BATCH):
        bu = jnp.dot(u_ref[b].astype(BF16), bbd_ref[...], preferred_element_type=F32)
        for j in range(S5_ROWS):
            s_ref[b, pl.ds(j, T_SCAN, stride=S5_PITCH), :] = bu[:, j * LANES:(j + 1) * LANES]

    lam_re = lam_ref[0:SUBLANES, :]
    lam_im = lam_ref[SUBLANES:S5_ROWS, :]

    def step(t, carry):
        base = t * S5_PITCH
        new = []
        for b in range(BATCH):
            h_re, h_im = carry[2 * b], carry[2 * b + 1]
            n_re = lam_re * h_re - lam_im * h_im + s_ref[b, pl.ds(base, SUBLANES), :]
            n_im = lam_re * h_im + lam_im * h_re + s_ref[b, pl.ds(base + SUBLANES, SUBLANES), :]
            s_ref[b, pl.ds(base, SUBLANES), :] = n_re
            s_ref[b, pl.ds(base + SUBLANES, SUBLANES), :] = n_im
            new += [n_re, n_im]
        return tuple(new)

    init = []
    for b in range(BATCH):
        init += [h_ref[b, 0:SUBLANES, :], h_ref[b, SUBLANES:S5_ROWS, :]]
    fin = lax.fori_loop(0, T_SCAN, step, tuple(init), unroll=8)
    for b in range(BATCH):
        h_ref[b, 0:SUBLANES, :] = fin[2 * b]
        h_ref[b, SUBLANES:S5_ROWS, :] = fin[2 * b + 1]

    states = jnp.concatenate(
        [jnp.concatenate([s_ref[b, pl.ds(j, T_SCAN, stride=S5_PITCH), :].astype(BF16) for j in range(S5_ROWS)],
                         axis=1) for b in range(BATCH)], axis=0)
    u = u_ref[...].reshape(BATCH * T_SCAN, MIXER_WIDTH)
    y = jnp.dot(states, cbd_ref[...], preferred_element_type=F32) + d_ref[...] * u
    y = jax.nn.gelu(y)
    gate = jnp.dot(y.astype(BF16), gw_ref[...], preferred_element_type=F32) + gb_ref[...]
    o_ref[...] = (y * jax.nn.sigmoid(gate)).astype(MIX_OUT).reshape(BATCH, T_SCAN, MIXER_WIDTH)


def _s5(zmix3, bbd, lam, cbd, d, gw, gb, layer):
    return pl.pallas_call(
        _s5_kernel,
        grid=(SEQ // T_SCAN,),
        in_specs=[pl.BlockSpec((BATCH, T_SCAN, MIXER_WIDTH), lambda c: (0, c, 0)),
                  _layer_block((MIXER_WIDTH, 2 * S5_NSTATE), layer),
                  _layer_block((S5_ROWS, LANES), layer),
                  _layer_block((2 * S5_NSTATE, MIXER_WIDTH), layer),
                  _layer_block((1, MIXER_WIDTH), layer),
                  _layer_block((MIXER_WIDTH, MIXER_WIDTH), layer),
                  _layer_block((1, MIXER_WIDTH), layer)],
        out_specs=pl.BlockSpec((BATCH, T_SCAN, MIXER_WIDTH), lambda c: (0, c, 0)),
        out_shape=jax.ShapeDtypeStruct((BATCH, SEQ, MIXER_WIDTH), MIX_OUT),
        scratch_shapes=[pltpu.VMEM((BATCH, T_SCAN * S5_PITCH, LANES), F32),
                        pltpu.VMEM((BATCH, S5_ROWS, LANES), F32)],
        compiler_params=_params(1),
        name="s5",
    )(zmix3, bbd, lam, cbd, d, gw, gb)


LRU_SLOTS = BATCH * MIXER_WIDTH // LANES
LRU_PITCH = LRU_SLOTS + SCAN_ROW_GAP


def _lru_kernel(x_ref, gate_ref, cw_ref, cb_ref, wa_ref, ba_ref, wx_ref, bx_ref, lam_ref, o_ref,
                tail_ref, a_ref, b_ref, h_ref):
    @pl.when(pl.program_id(0) == 0)
    def _():
        tail_ref[...] = jnp.zeros_like(tail_ref)
        h_ref[...] = jnp.zeros_like(h_ref)

    n_tiles = MIXER_WIDTH // LANES
    decay_rate = LRU_C * _softplus(-lam_ref[...])
    for b in range(BATCH):
        x = x_ref[b]
        xp = jnp.concatenate([tail_ref[b], x], axis=0)
        tail_ref[b] = x[T_SCAN - SUBLANES:T_SCAN, :]
        xc = cb_ref[...]
        for k in range(LRU_CONV):
            off = SUBLANES - (LRU_CONV - 1) + k
            xc = xc + cw_ref[k:k + 1, :] * xp[off:off + T_SCAN, :]
        xcb = xc.astype(BF16)
        r = jax.nn.sigmoid(jnp.dot(xcb, wa_ref[...], preferred_element_type=F32) + ba_ref[...])
        i = jax.nn.sigmoid(jnp.dot(xcb, wx_ref[...], preferred_element_type=F32) + bx_ref[...])
        log_a = -(r * decay_rate)
        a = jnp.exp(log_a)
        inp = jnp.sqrt(-jnp.tanh(log_a) * (a * a + 1.0)) * (i * xc)
        for j in range(n_tiles):
            slot = b * n_tiles + j
            a_ref[pl.ds(slot, T_SCAN, stride=LRU_PITCH), :] = a[:, j * LANES:(j + 1) * LANES]
            b_ref[pl.ds(slot, T_SCAN, stride=LRU_PITCH), :] = inp[:, j * LANES:(j + 1) * LANES]

    def step(t, h):
        base = t * LRU_PITCH
        h = a_ref[pl.ds(base, LRU_SLOTS), :] * h + b_ref[pl.ds(base, LRU_SLOTS), :]
        b_ref[pl.ds(base, LRU_SLOTS), :] = h
        return h

    h_ref[...] = lax.fori_loop(0, T_SCAN, step, h_ref[...], unroll=8)

    for b in range(BATCH):
        h = jnp.concatenate(
            [b_ref[pl.ds(b * n_tiles + j, T_SCAN, stride=LRU_PITCH), :] for j in range(n_tiles)],
            axis=1)
        o_ref[b] = (h * jax.nn.gelu(gate_ref[b])).astype(MIX_OUT)


def _lru(zmix3, cw, cb, wa, ba, wx, bx, lam, layer):
    vec = _layer_block((1, MIXER_WIDTH), layer)
    mat = _layer_block((MIXER_WIDTH, MIXER_WIDTH), layer)
    return pl.pallas_call(
        _lru_kernel,
        grid=(SEQ // T_SCAN,),
        in_specs=[pl.BlockSpec((BATCH, T_SCAN, MIXER_WIDTH), lambda c: (0, c, 1)),
                  pl.BlockSpec((BATCH, T_SCAN, MIXER_WIDTH), lambda c: (0, c, 2)),
                  _layer_block((LRU_CONV, MIXER_WIDTH), layer), vec, mat, vec, mat, vec, vec],
        out_specs=pl.BlockSpec((BATCH, T_SCAN, MIXER_WIDTH), lambda c: (0, c, 0)),
        out_shape=jax.ShapeDtypeStruct((BATCH, SEQ, MIXER_WIDTH), MIX_OUT),
        scratch_shapes=[pltpu.VMEM((BATCH, SUBLANES, MIXER_WIDTH), F32),
                        pltpu.VMEM((T_SCAN * LRU_PITCH, LANES), F32),
                        pltpu.VMEM((T_SCAN * LRU_PITCH, LANES), F32),
                        pltpu.VMEM((LRU_SLOTS, LANES), F32)],
        compiler_params=_params(1),
        name="rglru",
    )(zmix3, zmix3, cw, cb, wa, ba, wx, bx, lam)


def _attn_kernel(q_ref, qadd_ref, k_ref, kadd_ref, vt_ref, o_ref, s_ref, m_ref, acc_ref):
    i = pl.program_id(2)
    scale = FOX_HEAD_DIM ** -0.5 * LOG2E
    neg = jnp.finfo(F32).min
    slots = [slice(n * HEAD_PAD, (n + 1) * HEAD_PAD) for n in range(FOX_HEADS)]
    qs = [(q_ref[:, sl].astype(F32) * scale + qadd_ref[:, sl].astype(F32)).astype(BF16) for sl in slots]

    def logits_t(n, j):
        start = pl.multiple_of(j * TK, TK)
        ks = k_ref[pl.ds(start, TK), slots[n]] + kadd_ref[pl.ds(start, TK), slots[n]]
        return lax.dot_general(ks, qs[n], (((1,), (1,)), ((), ())), preferred_element_type=F32)

    def block(j, masked, prefetch):
        start = pl.multiple_of(j * TK, TK)
        for n in range(FOX_HEADS):
            s = s_ref[n]
            if masked:
                key = lax.broadcasted_iota(jnp.int32, (TK, TQ), 0)
                qry = lax.broadcasted_iota(jnp.int32, (TK, TQ), 1)
                s = jnp.where(key <= qry, s, neg)
            m = m_ref[n]
            m_new = jnp.maximum(m, jnp.max(s, axis=0, keepdims=True))
            alpha = jnp.exp2(m - m_new)
            p = jnp.exp2(s - jnp.concatenate([m_new] * (TK // SUBLANES), axis=0))
            m_ref[n] = m_new
            vt = vt_ref[0, n * HEAD_PAD:n * HEAD_PAD + ACC_ROWS, pl.ds(start, TK)]
            acc_ref[n] = (jnp.concatenate([alpha] * (ACC_ROWS // SUBLANES), axis=0) * acc_ref[n]
                          + jnp.dot(vt, p.astype(BF16), preferred_element_type=F32))
            if prefetch:
                s_ref[n] = logits_t(n, j + 1)

    for n in range(FOX_HEADS):
        s_ref[n] = logits_t(n, 0)
    m_ref[...] = jnp.full(m_ref.shape, neg, F32)
    acc_ref[...] = jnp.zeros_like(acc_ref)

    @pl.loop(0, i // 2)
    def _(jj):
        block(2 * jj, False, True)
        block(2 * jj + 1, False, True)

    @pl.when(i % 2 == 1)
    def _():
        block(i - 1, False, True)

    block(i, True, False)
    for n, sl in enumerate(slots):
        acc = acc_ref[n]
        out_t = acc[0:FOX_HEAD_DIM] / acc[FOX_HEAD_DIM:FOX_HEAD_DIM + 1]
        out_t = jnp.concatenate([out_t, jnp.zeros((HEAD_PAD - FOX_HEAD_DIM, TQ), F32)], axis=0)
        o_ref[:, sl] = out_t.T.astype(MIX_OUT)


def _attn(qkv, qadd, kadd, vt):
    n_q = SEQ // TQ
    width = FOX_HEADS * HEAD_PAD
    qrow = lambda b, h, i: (b * n_q + i, 0)
    return pl.pallas_call(
        _attn_kernel,
        grid=(BATCH, 1, n_q),
        in_specs=[pl.BlockSpec((TQ, width), qrow),
                  pl.BlockSpec((TQ, width), qrow),
                  pl.BlockSpec((SEQ, width), lambda b, h, i: (b, 1)),
                  pl.BlockSpec((SEQ, width), lambda b, h, i: (b, 0)),
                  pl.BlockSpec((1, width, SEQ), lambda b, h, i: (b, 0, 0))],
        out_specs=pl.BlockSpec((TQ, width), qrow),
        out_shape=jax.ShapeDtypeStruct((N_TOK, width), MIX_OUT),
        scratch_shapes=[pltpu.VMEM((FOX_HEADS, TK, TQ), F32),
                        pltpu.VMEM((FOX_HEADS, SUBLANES, TQ), F32),
                        pltpu.VMEM((FOX_HEADS, ACC_ROWS, TQ), F32)],
        compiler_params=_params(3),
        name="fox_attention",
    )(qkv, qadd, qkv, kadd, vt)


def _merge_mlp_kernel(ya_ref, yb_ref, yc_ref, yd_ref, x_ref, gm_ref, gmd_ref, wo_ref, g2_ref, w1_ref, w2_ref,
                      fg_ref, o_ref, h_ref, *, final_norm):
    j = pl.program_id(1)

    @pl.when(j == 0)
    def _():
        w = MIXER_WIDTH
        parts = [_rms(ya_ref[...].astype(F32), gm_ref[:, 0:w], w),
                 _rms(yb_ref[...].astype(F32), gm_ref[:, w:2 * w], w),
                 _rms(yc_ref[...].astype(F32), gm_ref[:, 2 * w:3 * w], w),
                 _rms(yd_ref[...].astype(F32), gmd_ref[...], w)]
        y = jnp.concatenate(parts, axis=1).astype(BF16)
        x1 = x_ref[...] + jnp.dot(y, wo_ref[...], preferred_element_type=F32)
        o_ref[...] = x1
        h_ref[...] = _rms(x1, g2_ref[...], D_MODEL).astype(BF16)

    a = jnp.dot(h_ref[...], w1_ref[...], preferred_element_type=F32)
    a = jnp.square(jnp.maximum(a, 0.0)).astype(BF16)
    o_ref[...] += jnp.dot(a, w2_ref[...], preferred_element_type=F32)

    if final_norm:
        @pl.when(j == pl.num_programs(1) - 1)
        def _():
            o_ref[...] = _rms(o_ref[...], fg_ref[...], D_MODEL)


def _merge_mlp(ya, yb, yc, yd, x, gm, gmd, wo, g2, w1, w2, fg, layer, final_norm):
    row = lambda i, j: (i, 0)
    k_dim = 3 * MIXER_WIDTH + FOX_HEADS * HEAD_PAD
    mix = pl.BlockSpec((TM_MLP, MIXER_WIDTH), row)
    return pl.pallas_call(
        functools.partial(_merge_mlp_kernel, final_norm=final_norm),
        grid=(N_TOK // TM_MLP, D_FF // TF_MLP),
        in_specs=[mix, mix, mix,
                  pl.BlockSpec((TM_MLP, FOX_HEADS * HEAD_PAD), row),
                  pl.BlockSpec((TM_MLP, D_MODEL), row),
                  _layer_block((1, 3 * MIXER_WIDTH), layer),
                  _layer_block((1, FOX_HEADS * HEAD_PAD), layer),
                  _layer_block((k_dim, D_MODEL), layer),
                  _layer_block((1, D_MODEL), layer),
                  pl.BlockSpec((None, D_MODEL, TF_MLP), lambda i, j: (layer, 0, j)),
                  pl.BlockSpec((None, TF_MLP, D_MODEL), lambda i, j: (layer, j, 0)),
                  pl.BlockSpec((1, D_MODEL), lambda i, j: (0, 0))],
        out_specs=pl.BlockSpec((TM_MLP, D_MODEL), row),
        out_shape=jax.ShapeDtypeStruct((N_TOK, D_MODEL), F32),
        scratch_shapes=[pltpu.VMEM((TM_MLP, D_MODEL), BF16)],
        compiler_params=_params(2),
        name="merge_mlp",
    )(ya, yb, yc, yd, x, gm, gmd, wo, g2, w1, w2, fg)


def _pad_heads(w):
    lead = w.shape[:-1]
    w = w.reshape(*lead, FOX_HEADS, FOX_HEAD_DIM)
    w = jnp.pad(w, [(0, 0)] * (len(lead) + 1) + [(0, HEAD_PAD - FOX_HEAD_DIM)])
    return w.reshape(*lead, FOX_HEADS * HEAD_PAD)


def _block_diag(blocks):
    n, g, r, c = blocks.shape
    eye = jnp.eye(g, dtype=blocks.dtype)
    return jnp.einsum('ngrc,gh->ngrhc', blocks, eye).reshape(n, g * r, g * c)


def _s5_discretize(lam_re, lam_im, log_dt, b_re, b_im):
    dt = jnp.exp(log_dt)[..., None]
    mag = jnp.exp(lam_re * dt)
    abar_re = mag * jnp.cos(lam_im * dt)
    abar_im = mag * jnp.sin(lam_im * dt)
    denom = jnp.square(lam_re) + jnp.square(lam_im)
    num_re = abar_re - 1.0
    num_im = abar_im
    fac_re = (num_re * lam_re + num_im * lam_im) / denom
    fac_im = (num_im * lam_re - num_re * lam_im) / denom
    bbar_re = fac_re[..., None] * b_re - fac_im[..., None] * b_im
    bbar_im = fac_re[..., None] * b_im + fac_im[..., None] * b_re
    return abar_re, abar_im, bbar_re, bbar_im


def kernel(x, norm1_g, w_in, sgu_norm_g, sgu_w, sgu_b, s5_lambda_re, s5_lambda_im, s5_log_dt, s5_b_re, s5_b_im, s5_c_re, s5_c_im, s5_d, s5_glu_w, s5_glu_b, lru_conv_w, lru_conv_b, lru_wa, lru_ba, lru_wx, lru_bx, lru_lambda, fox_fgate_b, mix_norm_g, w_out, norm2_g, w_mlp_in, w_mlp_out, final_g):
    w = MIXER_WIDTH
    row = lambda v: v.reshape(DEPTH, 1, -1)

    w_in_p = jnp.pad(w_in.astype(BF16), ((0, 0), (0, 0), (0, W_IN_COLS - w_in.shape[-1])))
    bf = row(jnp.pad(fox_fgate_b, ((0, 0), (0, LANES - FOX_HEADS))))

    sgu_wcat = jnp.transpose(sgu_w, (0, 2, 1, 3)).reshape(DEPTH, SGU_CHUNK, SGU_HEADS * SGU_CHUNK)
    sgu_bias = jnp.repeat(jnp.transpose(sgu_b, (0, 2, 1)), w // SGU_HEADS, axis=2)

    abar_re, abar_im, bbar_re, bbar_im = _s5_discretize(s5_lambda_re, s5_lambda_im, s5_log_dt, s5_b_re, s5_b_im)
    swap = lambda t: jnp.transpose(t, (0, 1, 3, 2))
    bbd = jnp.concatenate([_block_diag(swap(bbar_re)), _block_diag(swap(bbar_im))], axis=2).astype(BF16)
    cbd = jnp.concatenate([_block_diag(swap(s5_c_re)), -_block_diag(swap(s5_c_im))], axis=1).astype(BF16)
    lam = jnp.concatenate([abar_re.reshape(DEPTH, SUBLANES, LANES), abar_im.reshape(DEPTH, SUBLANES, LANES)], axis=1)
    glu_w = s5_glu_w.astype(BF16)

    wa_bd = _block_diag(lru_wa).astype(BF16)
    wx_bd = _block_diag(lru_wx).astype(BF16)

    w_o_pad = jnp.concatenate(
        [w_out[:, 0:3 * w],
         jnp.pad(w_out[:, 3 * w:].reshape(DEPTH, FOX_HEADS, FOX_HEAD_DIM, D_MODEL),
                 ((0, 0), (0, 0), (0, HEAD_PAD - FOX_HEAD_DIM), (0, 0))).reshape(DEPTH, FOX_HEADS * HEAD_PAD, D_MODEL)],
        axis=1).astype(BF16)
    g_mix = row(mix_norm_g[:, 0:3 * w])
    g_mix_d = _pad_heads(row(mix_norm_g[:, 3 * w:]))
    w1 = w_mlp_in.astype(BF16)
    w2 = w_mlp_out.astype(BF16)
    g1, g2 = row(norm1_g), row(norm2_g)
    fg = final_g.reshape(1, D_MODEL)

    xf = x.reshape(N_TOK, D_MODEL)
    for l in range(DEPTH):
        y_a, zmix, qkv, zf = _in_proj(xf, g1, w_in_p, row(sgu_norm_g), sgu_wcat, sgu_bias, l)
        zmix3 = zmix.reshape(BATCH, SEQ, ZMIX_COLS)
        qadd, kadd, vt = _fcum(zf.reshape(BATCH, SEQ, LANES), bf, qkv.reshape(BATCH, SEQ, QKV_COLS), l)
        qadd = qadd.reshape(N_TOK, FOX_HEADS * HEAD_PAD)
        kadd = kadd.reshape(N_TOK, FOX_HEADS * HEAD_PAD)
        y_b = _s5(zmix3, bbd, lam, cbd, row(s5_d), glu_w, row(s5_glu_b), l)
        y_c = _lru(zmix3, lru_conv_w, row(lru_conv_b), wa_bd, row(lru_ba.reshape(DEPTH, w)),
                   wx_bd, row(lru_bx.reshape(DEPTH, w)), row(lru_lambda), l)
        y_d = _attn(qkv, qadd, kadd, vt)
        xf = _merge_mlp(y_a, y_b.reshape(N_TOK, w), y_c.reshape(N_TOK, w), y_d, xf, g_mix, g_mix_d, w_o_pad,
                        g2, w1, w2, fg, l, final_norm=(l == DEPTH - 1))
    return xf.reshape(BATCH, SEQ, D_MODEL)
```

```python
import functools

import jax
import jax.numpy as jnp
import numpy as np
from jax import lax
from jax.experimental import pallas as pl
from jax.experimental.pallas import tpu as pltpu

D_MODEL = 1024
BATCH = 4
SEQ = 4096
DEPTH = 4
N_TOK = BATCH * SEQ
MIXER_WIDTH = 256
SGU_HEADS = 4
SGU_CHUNK = 128
S5_GROUP = 16
S5_GROUPS = 16
S5_STATE = 64
LRU_HEADS = 4
LRU_CONV = 4
LRU_C = 8.0
FOX_HEADS = 4
FOX_HEAD_DIM = 64
D_FF = 4 * D_MODEL
RMS_EPS = 1e-6
LOG2E = 1.4426950408889634

LANES = 128
SUBLANES = 8
HEAD_PAD = LANES
ZMIX_COLS = 3 * MIXER_WIDTH
W_IN_COLS = 8 * MIXER_WIDTH + LANES
QKV_COLS = 3 * FOX_HEADS * HEAD_PAD
S5_NSTATE = S5_GROUPS * S5_STATE
S5_ROWS = 2 * S5_NSTATE // LANES
SCAN_ROW_GAP = 4
S5_PITCH = S5_ROWS + SCAN_ROW_GAP

TM_IN = 1024
TM_MLP = 1024
TF_MLP = 1024
T_SCAN = 256
T_CUM = 256
TQ = 512
TK = 512
ACC_ROWS = 80
VMEM_LIMIT = 48 * 1024 * 1024
VMEM_LIMIT_MLP = 56 * 1024 * 1024

F32 = jnp.float32
BF16 = jnp.bfloat16
MIX_OUT = BF16


def _params(n_axes, vmem_limit=VMEM_LIMIT):
    return pltpu.CompilerParams(dimension_semantics=("arbitrary",) * n_axes,
                                vmem_limit_bytes=vmem_limit)


def _layer_block(shape, layer):
    zeros = (0,) * len(shape)
    return pl.BlockSpec((None,) + tuple(shape), lambda *_: (layer,) + zeros)


def _rms(x, g, width):
    ms = jnp.sum(jnp.square(x), axis=-1, keepdims=True) * (1.0 / width)
    return x * lax.rsqrt(ms + RMS_EPS) * g


def _softplus(x):
    return jnp.maximum(x, 0.0) + jnp.log1p(jnp.exp(-jnp.abs(x)))


def _sgu_mix(zu, zv, g, w, bias, o_ref):
    n_rows = zu.shape[0]
    u = jax.nn.gelu(zu)
    v = _rms(jax.nn.gelu(zv), g, MIXER_WIDTH)
    hd = MIXER_WIDTH // SGU_HEADS
    lane_head = lax.broadcasted_iota(jnp.int32, (SGU_CHUNK, MIXER_WIDTH), 1) // hd
    t_idx = lax.broadcasted_iota(jnp.int32, (SGU_CHUNK, SGU_HEADS * SGU_CHUNK), 0)
    s_idx = lax.broadcasted_iota(jnp.int32, (SGU_CHUNK, SGU_HEADS * SGU_CHUNK), 1) % SGU_CHUNK
    wm = jnp.where(s_idx <= t_idx, w, 0.0).astype(BF16)
    for c in range(n_rows // SGU_CHUNK):
        rows = slice(c * SGU_CHUNK, (c + 1) * SGU_CHUNK)
        vc = v[rows]
        vstack = jnp.concatenate(
            [jnp.where(lane_head == h, vc, 0.0) for h in range(SGU_HEADS)], axis=0).astype(BF16)
        mixed = jnp.dot(wm, vstack, preferred_element_type=F32) + bias
        o_ref[rows, :] = (u[rows] * mixed).astype(MIX_OUT)


def _in_proj_kernel(x_ref, g_ref, w_ref, sg_ref, sw_ref, sb_ref, ya_ref, zmix_ref, qkv_ref, zf_ref):
    w = MIXER_WIDTH
    h = _rms(x_ref[...], g_ref[...], D_MODEL).astype(BF16)
    z5 = jnp.dot(h, w_ref[:, 0:5 * w], preferred_element_type=F32)
    zmix_ref[...] = z5[:, 2 * w:5 * w]
    _sgu_mix(z5[:, 0:w], z5[:, w:2 * w], sg_ref[...], sw_ref[...], sb_ref[...], ya_ref)
    zqkvf = jnp.dot(h, w_ref[:, 5 * w:W_IN_COLS], preferred_element_type=F32)
    zf_ref[...] = zqkvf[:, 3 * w:]
    pad = jnp.zeros((TM_IN, HEAD_PAD - FOX_HEAD_DIM), F32)
    for slot in range(3 * FOX_HEADS):
        head = zqkvf[:, slot * FOX_HEAD_DIM:(slot + 1) * FOX_HEAD_DIM]
        qkv_ref[:, slot * HEAD_PAD:(slot + 1) * HEAD_PAD] = jnp.concatenate([head, pad], axis=1).astype(BF16)


def _in_proj(x, g, w, sgu_g, sgu_w, sgu_b, layer):
    row = lambda i: (i, 0)
    return pl.pallas_call(
        _in_proj_kernel,
        grid=(N_TOK // TM_IN,),
        in_specs=[pl.BlockSpec((TM_IN, D_MODEL), row),
                  _layer_block((1, D_MODEL), layer),
                  _layer_block((D_MODEL, W_IN_COLS), layer),
                  _layer_block((1, MIXER_WIDTH), layer),
                  _layer_block((SGU_CHUNK, SGU_HEADS * SGU_CHUNK), layer),
                  _layer_block((SGU_CHUNK, MIXER_WIDTH), layer)],
        out_specs=[pl.BlockSpec((TM_IN, MIXER_WIDTH), row),
                   pl.BlockSpec((TM_IN, ZMIX_COLS), row),
                   pl.BlockSpec((TM_IN, QKV_COLS), row),
                   pl.BlockSpec((TM_IN, LANES), row)],
        out_shape=[jax.ShapeDtypeStruct((N_TOK, MIXER_WIDTH), MIX_OUT),
                   jax.ShapeDtypeStruct((N_TOK, ZMIX_COLS), F32),
                   jax.ShapeDtypeStruct((N_TOK, QKV_COLS), BF16),
                   jax.ShapeDtypeStruct((N_TOK, LANES), F32)],
        compiler_params=_params(1),
        name="in_proj_sgu",
    )(x, g, w, sgu_g, sgu_w, sgu_b)


def _split3(x):
    hi = x.astype(BF16)
    r1 = x - hi.astype(F32)
    mid = r1.astype(BF16)
    lo = (r1 - mid.astype(F32)).astype(BF16)
    return hi, mid, lo


def _bias_placement():
    pq = np.zeros((LANES, FOX_HEADS * HEAD_PAD), np.float32)
    pk = np.zeros_like(pq)
    ones_q = np.zeros((1, FOX_HEADS * HEAD_PAD), np.float32)
    ones_k = np.zeros_like(ones_q)
    for h in range(FOX_HEADS):
        for piece in range(3):
            pq[piece * FOX_HEADS + h, h * HEAD_PAD + FOX_HEAD_DIM + piece] = 1.0
            pk[piece * FOX_HEADS + h, h * HEAD_PAD + FOX_HEAD_DIM + 3 + piece] = -1.0
            ones_q[0, h * HEAD_PAD + FOX_HEAD_DIM + 3 + piece] = 1.0
            ones_k[0, h * HEAD_PAD + FOX_HEAD_DIM + piece] = 1.0
    return (jnp.asarray(pq, BF16), jnp.asarray(pk, BF16), jnp.asarray(ones_q), jnp.asarray(ones_k))


def _fcum_kernel(zf_ref, bf_ref, v_ref, tril_ref, pq_ref, pk_ref, oq_ref, ok_ref, qadd_ref, kadd_ref, vt_ref,
                 carry_ref):
    @pl.when(pl.program_id(0) == 0)
    def _():
        carry_ref[...] = jnp.zeros_like(carry_ref)

    tril = tril_ref[...]
    lane = lax.broadcasted_iota(jnp.int32, (T_CUM, LANES), 1)
    ones_row = lax.broadcasted_iota(jnp.int32, (HEAD_PAD, T_CUM), 0) == FOX_HEAD_DIM
    for b in range(BATCH):
        for n in range(FOX_HEADS):
            v_t = v_ref[b, :, n * HEAD_PAD:(n + 1) * HEAD_PAD].astype(F32).T
            vt_ref[b, n * HEAD_PAD:(n + 1) * HEAD_PAD, :] = jnp.where(ones_row, 1.0, v_t).astype(BF16)
        logit = zf_ref[b] + bf_ref[...]
        log_f = -_softplus(-logit)
        hi, mid, lo = _split3(log_f)
        cs = (jnp.dot(tril, hi, preferred_element_type=F32)
              + jnp.dot(tril, mid, preferred_element_type=F32)
              + jnp.dot(tril, lo, preferred_element_type=F32))
        cum = cs + carry_ref[b]
        carry_ref[b] = cum[T_CUM - 1:T_CUM, :]
        hi, mid, lo = [p.astype(F32) for p in _split3(cum * LOG2E)]
        pieces = jnp.where(lane < FOX_HEADS, hi,
                           jnp.where(lane < 2 * FOX_HEADS, pltpu.roll(mid, FOX_HEADS, axis=1),
                                     jnp.where(lane < 3 * FOX_HEADS, pltpu.roll(lo, 2 * FOX_HEADS, axis=1), 0.0))
                           ).astype(BF16)
        qadd_ref[b] = (jnp.dot(pieces, pq_ref[...], preferred_element_type=F32) + oq_ref[...]).astype(BF16)
        kadd_ref[b] = (jnp.dot(pieces, pk_ref[...], preferred_element_type=F32) + ok_ref[...]).astype(BF16)


def _fcum(zf3, bf, qkv3, layer):
    pq, pk, ones_q, ones_k = _bias_placement()
    tril = jnp.asarray(np.tril(np.ones((T_CUM, T_CUM), np.float32)), BF16)
    full = lambda c: (0, 0)
    wide = FOX_HEADS * HEAD_PAD
    chunk = lambda c: (0, c, 0)
    return pl.pallas_call(
        _fcum_kernel,
        grid=(SEQ // T_CUM,),
        in_specs=[pl.BlockSpec((BATCH, T_CUM, LANES), chunk),
                  _layer_block((1, LANES), layer),
                  pl.BlockSpec((BATCH, T_CUM, wide), lambda c: (0, c, 2)),
                  pl.BlockSpec((T_CUM, T_CUM), full),
                  pl.BlockSpec((LANES, wide), full),
                  pl.BlockSpec((LANES, wide), full),
                  pl.BlockSpec((1, wide), full),
                  pl.BlockSpec((1, wide), full)],
        out_specs=[pl.BlockSpec((BATCH, T_CUM, wide), chunk),
                   pl.BlockSpec((BATCH, T_CUM, wide), chunk),
                   pl.BlockSpec((BATCH, wide, T_CUM), lambda c: (0, 0, c))],
        out_shape=[jax.ShapeDtypeStruct((BATCH, SEQ, wide), BF16),
                   jax.ShapeDtypeStruct((BATCH, SEQ, wide), BF16),
                   jax.ShapeDtypeStruct((BATCH, wide, SEQ), BF16)],
        scratch_shapes=[pltpu.VMEM((BATCH, 1, LANES), F32)],
        compiler_params=_params(1),
        name="forget_cumsum",
    )(zf3, bf, qkv3, tril, pq, pk, ones_q, ones_k)


def _s5_kernel(u_ref, bbd_ref, lam_ref, cbd_ref, d_ref, gw_ref, gb_ref, o_ref, s_ref, h_ref):
    @pl.when(pl.program_id(0) == 0)
    def _():
        h_ref[...] = jnp.zeros_like(h_ref)

    for b in range(BATCH):
        bu = jnp.dot(u_ref[b].astype(BF16), bbd_ref[...], preferred_element_type=F32)
        for j in range(S5_ROWS):
            s_ref[b, pl.ds(j, T_SCAN, stride=S5_PITCH), :] = bu[:, j * LANES:(j + 1) * LANES]

    lam_re = lam_ref[0:SUBLANES, :]
    lam_im = lam_ref[SUBLANES:S5_ROWS, :]

    def step(t, carry):
        base = t * S5_PITCH
        new = []
        for b in range(BATCH):
            h_re, h_im = carry[2 * b], carry[2 * b + 1]
            n_re = lam_re * h_re - lam_im * h_im + s_ref[b, pl.ds(base, SUBLANES), :]
            n_im = lam_re * h_im + lam_im * h_re + s_ref[b, pl.ds(base + SUBLANES, SUBLANES), :]
            s_ref[b, pl.ds(base, SUBLANES), :] = n_re
            s_ref[b, pl.ds(base + SUBLANES, SUBLANES), :] = n_im
            new += [n_re, n_im]
        return tuple(new)

    init = []
    for b in range(BATCH):
        init += [h_ref[b, 0:SUBLANES, :], h_ref[b, SUBLANES:S5_ROWS, :]]
    fin = lax.fori_loop(0, T_SCAN, step, tuple(init), unroll=8)
    for b in range(BATCH):
        h_ref[b, 0:SUBLANES, :] = fin[2 * b]
        h_ref[b, SUBLANES:S5_ROWS, :] = fin[2 * b + 1]

    states = jnp.concatenate(
        [jnp.concatenate([s_ref[b, pl.ds(j, T_SCAN, stride=S5_PITCH), :].astype(BF16) for j in range(S5_ROWS)],
                         axis=1) for b in range(BATCH)], axis=0)
    u = u_ref[...].reshape(BATCH * T_SCAN, MIXER_WIDTH)
    y = jnp.dot(states, cbd_ref[...], preferred_element_type=F32) + d_ref[...] * u
    y = jax.nn.gelu(y)
    gate = jnp.dot(y.astype(BF16), gw_ref[...], preferred_element_type=F32) + gb_ref[...]
    o_ref[...] = (y * jax.nn.sigmoid(gate)).astype(MIX_OUT).reshape(BATCH, T_SCAN, MIXER_WIDTH)


def _s5(zmix3, bbd, lam, cbd, d, gw, gb, layer):
    return pl.pallas_call(
        _s5_kernel,
        grid=(SEQ // T_SCAN,),
        in_specs=[pl.BlockSpec((BATCH, T_SCAN, MIXER_WIDTH), lambda c: (0, c, 0)),
                  _layer_block((MIXER_WIDTH, 2 * S5_NSTATE), layer),
                  _layer_block((S5_ROWS, LANES), layer),
                  _layer_block((2 * S5_NSTATE, MIXER_WIDTH), layer),
                  _layer_block((1, MIXER_WIDTH), layer),
                  _layer_block((MIXER_WIDTH, MIXER_WIDTH), layer),
                  _layer_block((1, MIXER_WIDTH), layer)],
        out_specs=pl.BlockSpec((BATCH, T_SCAN, MIXER_WIDTH), lambda c: (0, c, 0)),
        out_shape=jax.ShapeDtypeStruct((BATCH, SEQ, MIXER_WIDTH), MIX_OUT),
        scratch_shapes=[pltpu.VMEM((BATCH, T_SCAN * S5_PITCH, LANES), F32),
                        pltpu.VMEM((BATCH, S5_ROWS, LANES), F32)],
        compiler_params=_params(1),
        name="s5",
    )(zmix3, bbd, lam, cbd, d, gw, gb)


LRU_SLOTS = BATCH * MIXER_WIDTH // LANES
LRU_PITCH = LRU_SLOTS + SCAN_ROW_GAP


def _lru_kernel(x_ref, gate_ref, cw_ref, cb_ref, wa_ref, ba_ref, wx_ref, bx_ref, lam_ref, o_ref,
                tail_ref, a_ref, b_ref, h_ref):
    @pl.when(pl.program_id(0) == 0)
    def _():
        tail_ref[...] = jnp.zeros_like(tail_ref)
        h_ref[...] = jnp.zeros_like(h_ref)

    n_tiles = MIXER_WIDTH // LANES
    decay_rate = LRU_C * _softplus(-lam_ref[...])
    for b in range(BATCH):
        x = x_ref[b]
        xp = jnp.concatenate([tail_ref[b], x], axis=0)
        tail_ref[b] = x[T_SCAN - SUBLANES:T_SCAN, :]
        xc = cb_ref[...]
        for k in range(LRU_CONV):
            off = SUBLANES - (LRU_CONV - 1) + k
            xc = xc + cw_ref[k:k + 1, :] * xp[off:off + T_SCAN, :]
        xcb = xc.astype(BF16)
        r = jax.nn.sigmoid(jnp.dot(xcb, wa_ref[...], preferred_element_type=F32) + ba_ref[...])
        i = jax.nn.sigmoid(jnp.dot(xcb, wx_ref[...], preferred_element_type=F32) + bx_ref[...])
        log_a = -(r * decay_rate)
        a = jnp.exp(log_a)
        inp = jnp.sqrt(-jnp.tanh(log_a) * (a * a + 1.0)) * (i * xc)
        for j in range(n_tiles):
            slot = b * n_tiles + j
            a_ref[pl.ds(slot, T_SCAN, stride=LRU_PITCH), :] = a[:, j * LANES:(j + 1) * LANES]
            b_ref[pl.ds(slot, T_SCAN, stride=LRU_PITCH), :] = inp[:, j * LANES:(j + 1) * LANES]

    def step(t, h):
        base = t * LRU_PITCH
        h = a_ref[pl.ds(base, LRU_SLOTS), :] * h + b_ref[pl.ds(base, LRU_SLOTS), :]
        b_ref[pl.ds(base, LRU_SLOTS), :] = h
        return h

    h_ref[...] = lax.fori_loop(0, T_SCAN, step, h_ref[...], unroll=8)

    for b in range(BATCH):
        h = jnp.concatenate(
            [b_ref[pl.ds(b * n_tiles + j, T_SCAN, stride=LRU_PITCH), :] for j in range(n_tiles)],
            axis=1)
        o_ref[b] = (h * jax.nn.gelu(gate_ref[b])).astype(MIX_OUT)


def _lru(zmix3, cw, cb, wa, ba, wx, bx, lam, layer):
    vec = _layer_block((1, MIXER_WIDTH), layer)
    mat = _layer_block((MIXER_WIDTH, MIXER_WIDTH), layer)
    return pl.pallas_call(
        _lru_kernel,
        grid=(SEQ // T_SCAN,),
        in_specs=[pl.BlockSpec((BATCH, T_SCAN, MIXER_WIDTH), lambda c: (0, c, 1)),
                  pl.BlockSpec((BATCH, T_SCAN, MIXER_WIDTH), lambda c: (0, c, 2)),
                  _layer_block((LRU_CONV, MIXER_WIDTH), layer), vec, mat, vec, mat, vec, vec],
        out_specs=pl.BlockSpec((BATCH, T_SCAN, MIXER_WIDTH), lambda c: (0, c, 0)),
        out_shape=jax.ShapeDtypeStruct((BATCH, SEQ, MIXER_WIDTH), MIX_OUT),
        scratch_shapes=[pltpu.VMEM((BATCH, SUBLANES, MIXER_WIDTH), F32),
                        pltpu.VMEM((T_SCAN * LRU_PITCH, LANES), F32),
                        pltpu.VMEM((T_SCAN * LRU_PITCH, LANES), F32),
                        pltpu.VMEM((LRU_SLOTS, LANES), F32)],
        compiler_params=_params(1),
        name="rglru",
    )(zmix3, zmix3, cw, cb, wa, ba, wx, bx, lam)


def _attn_kernel(q_ref, qadd_ref, k_ref, kadd_ref, vt_ref, o_ref, s_ref, m_ref, acc_ref):
    i = pl.program_id(2)
    scale = FOX_HEAD_DIM ** -0.5 * LOG2E
    neg = jnp.finfo(F32).min
    slots = [slice(n * HEAD_PAD, (n + 1) * HEAD_PAD) for n in range(FOX_HEADS)]
    qs = [(q_ref[:, sl].astype(F32) * scale + qadd_ref[:, sl].astype(F32)).astype(BF16) for sl in slots]

    def logits_t(n, j):
        start = pl.multiple_of(j * TK, TK)
        ks = k_ref[pl.ds(start, TK), slots[n]] + kadd_ref[pl.ds(start, TK), slots[n]]
        return lax.dot_general(ks, qs[n], (((1,), (1,)), ((), ())), preferred_element_type=F32)

    def block(j, masked, prefetch):
        start = pl.multiple_of(j * TK, TK)
        for n in range(FOX_HEADS):
            s = s_ref[n]
            if masked:
                key = lax.broadcasted_iota(jnp.int32, (TK, TQ), 0)
                qry = lax.broadcasted_iota(jnp.int32, (TK, TQ), 1)
                s = jnp.where(key <= qry, s, neg)
            m = m_ref[n]
            m_new = jnp.maximum(m, jnp.max(s, axis=0, keepdims=True))
            alpha = jnp.exp2(m - m_new)
            p = jnp.exp2(s - jnp.concatenate([m_new] * (TK // SUBLANES), axis=0))
            m_ref[n] = m_new
            vt = vt_ref[0, n * HEAD_PAD:n * HEAD_PAD + ACC_ROWS, pl.ds(start, TK)]
            acc_ref[n] = (jnp.concatenate([alpha] * (ACC_ROWS // SUBLANES), axis=0) * acc_ref[n]
                          + jnp.dot(vt, p.astype(BF16), preferred_element_type=F32))
            if prefetch:
                s_ref[n] = logits_t(n, j + 1)

    for n in range(FOX_HEADS):
        s_ref[n] = logits_t(n, 0)
    m_ref[...] = jnp.full(m_ref.shape, neg, F32)
    acc_ref[...] = jnp.zeros_like(acc_ref)

    @pl.loop(0, i // 2)
    def _(jj):
        block(2 * jj, False, True)
        block(2 * jj + 1, False, True)

    @pl.when(i % 2 == 1)
    def _():
        block(i - 1, False, True)

    block(i, True, False)
    for n, sl in enumerate(slots):
        acc = acc_ref[n]
        out_t = acc[0:FOX_HEAD_DIM] / acc[FOX_HEAD_DIM:FOX_HEAD_DIM + 1]
        out_t = jnp.concatenate([out_t, jnp.zeros((HEAD_PAD - FOX_HEAD_DIM, TQ), F32)], axis=0)
        o_ref[:, sl] = out_t.T.astype(MIX_OUT)


def _attn(qkv, qadd, kadd, vt):
    n_q = SEQ // TQ
    width = FOX_HEADS * HEAD_PAD
    qrow = lambda b, h, i: (b * n_q + i, 0)
    return pl.pallas_call(
        _attn_kernel,
        grid=(BATCH, 1, n_q),
        in_specs=[pl.BlockSpec((TQ, width), qrow),
                  pl.BlockSpec((TQ, width), qrow),
                  pl.BlockSpec((SEQ, width), lambda b, h, i: (b, 1)),
                  pl.BlockSpec((SEQ, width), lambda b, h, i: (b, 0)),
                  pl.BlockSpec((1, width, SEQ), lambda b, h, i: (b, 0, 0))],
        out_specs=pl.BlockSpec((TQ, width), qrow),
        out_shape=jax.ShapeDtypeStruct((N_TOK, width), MIX_OUT),
        scratch_shapes=[pltpu.VMEM((FOX_HEADS, TK, TQ), F32),
                        pltpu.VMEM((FOX_HEADS, SUBLANES, TQ), F32),
                        pltpu.VMEM((FOX_HEADS, ACC_ROWS, TQ), F32)],
        compiler_params=_params(3),
        name="fox_attention",
    )(qkv, qadd, qkv, kadd, vt)


def _merge_mlp_kernel(ya_ref, yb_ref, yc_ref, yd_ref, x_ref, gm_ref, gmd_ref, wo_ref, g2_ref, w1_ref, w2_ref,
                      fg_ref, o_ref, h_ref, *, final_norm):
    j = pl.program_id(1)

    @pl.when(j == 0)
    def _():
        w = MIXER_WIDTH
        parts = [_rms(ya_ref[...].astype(F32), gm_ref[:, 0:w], w),
                 _rms(yb_ref[...].astype(F32), gm_ref[:, w:2 * w], w),
                 _rms(yc_ref[...].astype(F32), gm_ref[:, 2 * w:3 * w], w),
                 _rms(yd_ref[...].astype(F32), gmd_ref[...], w)]
        y = jnp.concatenate(parts, axis=1).astype(BF16)
        x1 = x_ref[...] + jnp.dot(y, wo_ref[...], preferred_element_type=F32)
        o_ref[...] = x1
        h_ref[...] = _rms(x1, g2_ref[...], D_MODEL).astype(BF16)

    a = jnp.dot(h_ref[...], w1_ref[...].astype(BF16), preferred_element_type=F32)
    a = jnp.square(jnp.maximum(a, 0.0)).astype(BF16)
    o_ref[...] += jnp.dot(a, w2_ref[...].astype(BF16), preferred_element_type=F32)

    if final_norm:
        @pl.when(j == pl.num_programs(1) - 1)
        def _():
            o_ref[...] = _rms(o_ref[...], fg_ref[...], D_MODEL)


def _merge_mlp(ya, yb, yc, yd, x, gm, gmd, wo, g2, w1, w2, fg, layer, final_norm):
    row = lambda i, j: (i, 0)
    k_dim = 3 * MIXER_WIDTH + FOX_HEADS * HEAD_PAD
    mix = pl.BlockSpec((TM_MLP, MIXER_WIDTH), row)
    return pl.pallas_call(
        functools.partial(_merge_mlp_kernel, final_norm=final_norm),
        grid=(N_TOK // TM_MLP, D_FF // TF_MLP),
        in_specs=[mix, mix, mix,
                  pl.BlockSpec((TM_MLP, FOX_HEADS * HEAD_PAD), row),
                  pl.BlockSpec((TM_MLP, D_MODEL), row),
                  _layer_block((1, 3 * MIXER_WIDTH), layer),
                  _layer_block((1, FOX_HEADS * HEAD_PAD), layer),
                  _layer_block((k_dim, D_MODEL), layer),
                  _layer_block((1, D_MODEL), layer),
                  pl.BlockSpec((None, D_MODEL, TF_MLP), lambda i, j: (layer, 0, j)),
                  pl.BlockSpec((None, TF_MLP, D_MODEL), lambda i, j: (layer, j, 0)),
                  pl.BlockSpec((1, D_MODEL), lambda i, j: (0, 0))],
        out_specs=pl.BlockSpec((TM_MLP, D_MODEL), row),
        out_shape=jax.ShapeDtypeStruct((N_TOK, D_MODEL), F32),
        scratch_shapes=[pltpu.VMEM((TM_MLP, D_MODEL), BF16)],
        compiler_params=_params(2, VMEM_LIMIT_MLP),
        name="merge_mlp",
    )(ya, yb, yc, yd, x, gm, gmd, wo, g2, w1, w2, fg)


def _pad_heads(w):
    lead = w.shape[:-1]
    w = w.reshape(*lead, FOX_HEADS, FOX_HEAD_DIM)
    w = jnp.pad(w, [(0, 0)] * (len(lead) + 1) + [(0, HEAD_PAD - FOX_HEAD_DIM)])
    return w.reshape(*lead, FOX_HEADS * HEAD_PAD)


def _block_diag(blocks):
    n, g, r, c = blocks.shape
    eye = jnp.eye(g, dtype=blocks.dtype)
    return jnp.einsum('ngrc,gh->ngrhc', blocks, eye).reshape(n, g * r, g * c)


def _s5_discretize(lam_re, lam_im, log_dt, b_re, b_im):
    dt = jnp.exp(log_dt)[..., None]
    mag = jnp.exp(lam_re * dt)
    abar_re = mag * jnp.cos(lam_im * dt)
    abar_im = mag * jnp.sin(lam_im * dt)
    denom = jnp.square(lam_re) + jnp.square(lam_im)
    num_re = abar_re - 1.0
    num_im = abar_im
    fac_re = (num_re * lam_re + num_im * lam_im) / denom
    fac_im = (num_im * lam_re - num_re * lam_im) / denom
    bbar_re = fac_re[..., None] * b_re - fac_im[..., None] * b_im
    bbar_im = fac_re[..., None] * b_im + fac_im[..., None] * b_re
    return abar_re, abar_im, bbar_re, bbar_im


def kernel(x, norm1_g, w_in, sgu_norm_g, sgu_w, sgu_b, s5_lambda_re, s5_lambda_im, s5_log_dt, s5_b_re, s5_b_im, s5_c_re, s5_c_im, s5_d, s5_glu_w, s5_glu_b, lru_conv_w, lru_conv_b, lru_wa, lru_ba, lru_wx, lru_bx, lru_lambda, fox_fgate_b, mix_norm_g, w_out, norm2_g, w_mlp_in, w_mlp_out, final_g):
    w = MIXER_WIDTH
    row = lambda v: v.reshape(DEPTH, 1, -1)

    w_in_p = jnp.pad(w_in.astype(BF16), ((0, 0), (0, 0), (0, W_IN_COLS - w_in.shape[-1])))
    bf = row(jnp.pad(fox_fgate_b, ((0, 0), (0, LANES - FOX_HEADS))))

    sgu_wcat = jnp.transpose(sgu_w, (0, 2, 1, 3)).reshape(DEPTH, SGU_CHUNK, SGU_HEADS * SGU_CHUNK)
    sgu_bias = jnp.repeat(jnp.transpose(sgu_b, (0, 2, 1)), w // SGU_HEADS, axis=2)

    abar_re, abar_im, bbar_re, bbar_im = _s5_discretize(s5_lambda_re, s5_lambda_im, s5_log_dt, s5_b_re, s5_b_im)
    swap = lambda t: jnp.transpose(t, (0, 1, 3, 2))
    bbd = jnp.concatenate([_block_diag(swap(bbar_re)), _block_diag(swap(bbar_im))], axis=2).astype(BF16)
    cbd = jnp.concatenate([_block_diag(swap(s5_c_re)), -_block_diag(swap(s5_c_im))], axis=1).astype(BF16)
    lam = jnp.concatenate([abar_re.reshape(DEPTH, SUBLANES, LANES), abar_im.reshape(DEPTH, SUBLANES, LANES)], axis=1)
    glu_w = s5_glu_w.astype(BF16)

    wa_bd = _block_diag(lru_wa).astype(BF16)
    wx_bd = _block_diag(lru_wx).astype(BF16)

    w_o_pad = jnp.concatenate(
        [w_out[:, 0:3 * w],
         jnp.pad(w_out[:, 3 * w:].reshape(DEPTH, FOX_HEADS, FOX_HEAD_DIM, D_MODEL),
                 ((0, 0), (0, 0), (0, HEAD_PAD - FOX_HEAD_DIM), (0, 0))).reshape(DEPTH, FOX_HEADS * HEAD_PAD, D_MODEL)],
        axis=1).astype(BF16)
    g_mix = row(mix_norm_g[:, 0:3 * w])
    g_mix_d = _pad_heads(row(mix_norm_g[:, 3 * w:]))
    w1, w2 = w_mlp_in, w_mlp_out
    g1, g2 = row(norm1_g), row(norm2_g)
    fg = final_g.reshape(1, D_MODEL)

    xf = x.reshape(N_TOK, D_MODEL)
    for l in range(DEPTH):
        y_a, zmix, qkv, zf = _in_proj(xf, g1, w_in_p, row(sgu_norm_g), sgu_wcat, sgu_bias, l)
        zmix3 = zmix.reshape(BATCH, SEQ, ZMIX_COLS)
        qadd, kadd, vt = _fcum(zf.reshape(BATCH, SEQ, LANES), bf, qkv.reshape(BATCH, SEQ, QKV_COLS), l)
        qadd = qadd.reshape(N_TOK, FOX_HEADS * HEAD_PAD)
        kadd = kadd.reshape(N_TOK, FOX_HEADS * HEAD_PAD)
        y_b = _s5(zmix3, bbd, lam, cbd, row(s5_d), glu_w, row(s5_glu_b), l)
        y_c = _lru(zmix3, lru_conv_w, row(lru_conv_b), wa_bd, row(lru_ba.reshape(DEPTH, w)),
                   wx_bd, row(lru_bx.reshape(DEPTH, w)), row(lru_lambda), l)
        y_d = _attn(qkv, qadd, kadd, vt)
        xf = _merge_mlp(y_a, y_b.reshape(N_TOK, w), y_c.reshape(N_TOK, w), y_d, xf, g_mix, g_mix_d, w_o_pad,
                        g2, w1, w2, fg, l, final_norm=(l == DEPTH - 1))
    return xf.reshape(BATCH, SEQ, D_MODEL)
```

```python
import functools

import jax
import jax.numpy as jnp
import numpy as np
from jax import lax
from jax.experimental import pallas as pl
from jax.experimental.pallas import tpu as pltpu

D_MODEL = 1024
BATCH = 4
SEQ = 4096
DEPTH = 4
N_TOK = BATCH * SEQ
MIXER_WIDTH = 256
SGU_HEADS = 4
SGU_CHUNK = 128
S5_GROUP = 16
S5_GROUPS = 16
S5_STATE = 64
LRU_HEADS = 4
LRU_CONV = 4
LRU_C = 8.0
FOX_HEADS = 4
FOX_HEAD_DIM = 64
D_FF = 4 * D_MODEL
RMS_EPS = 1e-6
LOG2E = 1.4426950408889634

LANES = 128
SUBLANES = 8
HEAD_PAD = LANES
ZMIX_COLS = 3 * MIXER_WIDTH
W_IN_COLS = 8 * MIXER_WIDTH + LANES
QKV_COLS = 3 * FOX_HEADS * HEAD_PAD
S5_NSTATE = S5_GROUPS * S5_STATE
S5_ROWS = 2 * S5_NSTATE // LANES
SCAN_ROW_GAP = 4
S5_PITCH = S5_ROWS + SCAN_ROW_GAP

TM_IN = 1024
TM_MLP = 1024
TF_MLP = 1024
T_SCAN = 256
TQ = 512
TK = 512
ACC_ROWS = 80
VMEM_LIMIT = 48 * 1024 * 1024
VMEM_LIMIT_MIXERS = 56 * 1024 * 1024

F32 = jnp.float32
BF16 = jnp.bfloat16
MIX_OUT = BF16


def _params(n_axes, vmem_limit=VMEM_LIMIT):
    return pltpu.CompilerParams(dimension_semantics=("arbitrary",) * n_axes,
                                vmem_limit_bytes=vmem_limit)


def _layer_block(shape, layer):
    zeros = (0,) * len(shape)
    return pl.BlockSpec((None,) + tuple(shape), lambda *_: (layer,) + zeros)


def _rms(x, g, width):
    ms = jnp.sum(jnp.square(x), axis=-1, keepdims=True) * (1.0 / width)
    return x * lax.rsqrt(ms + RMS_EPS) * g


def _softplus(x):
    return jnp.maximum(x, 0.0) + jnp.log1p(jnp.exp(-jnp.abs(x)))


def _sgu_mix(zu, zv, g, w, bias, o_ref):
    n_rows = zu.shape[0]
    u = jax.nn.gelu(zu)
    v = _rms(jax.nn.gelu(zv), g, MIXER_WIDTH)
    hd = MIXER_WIDTH // SGU_HEADS
    lane_head = lax.broadcasted_iota(jnp.int32, (SGU_CHUNK, MIXER_WIDTH), 1) // hd
    t_idx = lax.broadcasted_iota(jnp.int32, (SGU_CHUNK, SGU_HEADS * SGU_CHUNK), 0)
    s_idx = lax.broadcasted_iota(jnp.int32, (SGU_CHUNK, SGU_HEADS * SGU_CHUNK), 1) % SGU_CHUNK
    wm = jnp.where(s_idx <= t_idx, w, 0.0).astype(BF16)
    for c in range(n_rows // SGU_CHUNK):
        rows = slice(c * SGU_CHUNK, (c + 1) * SGU_CHUNK)
        vc = v[rows]
        vstack = jnp.concatenate(
            [jnp.where(lane_head == h, vc, 0.0) for h in range(SGU_HEADS)], axis=0).astype(BF16)
        mixed = jnp.dot(wm, vstack, preferred_element_type=F32) + bias
        o_ref[rows, :] = (u[rows] * mixed).astype(MIX_OUT)


def _in_proj_kernel(x_ref, g_ref, w_ref, sg_ref, sw_ref, sb_ref, ya_ref, zmix_ref, qkv_ref, zf_ref):
    w = MIXER_WIDTH
    h = _rms(x_ref[...], g_ref[...], D_MODEL).astype(BF16)
    z5 = jnp.dot(h, w_ref[:, 0:5 * w], preferred_element_type=F32)
    zmix_ref[...] = z5[:, 2 * w:5 * w]
    _sgu_mix(z5[:, 0:w], z5[:, w:2 * w], sg_ref[...], sw_ref[...], sb_ref[...], ya_ref)
    zqkvf = jnp.dot(h, w_ref[:, 5 * w:W_IN_COLS], preferred_element_type=F32)
    zf_ref[...] = zqkvf[:, 3 * w:]
    pad = jnp.zeros((TM_IN, HEAD_PAD - FOX_HEAD_DIM), F32)
    for slot in range(3 * FOX_HEADS):
        head = zqkvf[:, slot * FOX_HEAD_DIM:(slot + 1) * FOX_HEAD_DIM]
        qkv_ref[:, slot * HEAD_PAD:(slot + 1) * HEAD_PAD] = jnp.concatenate([head, pad], axis=1).astype(BF16)


def _in_proj(x, g, w, sgu_g, sgu_w, sgu_b, layer):
    row = lambda i: (i, 0)
    return pl.pallas_call(
        _in_proj_kernel,
        grid=(N_TOK // TM_IN,),
        in_specs=[pl.BlockSpec((TM_IN, D_MODEL), row),
                  _layer_block((1, D_MODEL), layer),
                  _layer_block((D_MODEL, W_IN_COLS), layer),
                  _layer_block((1, MIXER_WIDTH), layer),
                  _layer_block((SGU_CHUNK, SGU_HEADS * SGU_CHUNK), layer),
                  _layer_block((SGU_CHUNK, MIXER_WIDTH), layer)],
        out_specs=[pl.BlockSpec((TM_IN, MIXER_WIDTH), row),
                   pl.BlockSpec((TM_IN, ZMIX_COLS), row),
                   pl.BlockSpec((TM_IN, QKV_COLS), row),
                   pl.BlockSpec((TM_IN, LANES), row)],
        out_shape=[jax.ShapeDtypeStruct((N_TOK, MIXER_WIDTH), MIX_OUT),
                   jax.ShapeDtypeStruct((N_TOK, ZMIX_COLS), F32),
                   jax.ShapeDtypeStruct((N_TOK, QKV_COLS), BF16),
                   jax.ShapeDtypeStruct((N_TOK, LANES), F32)],
        compiler_params=_params(1),
        name="in_proj_sgu",
    )(x, g, w, sgu_g, sgu_w, sgu_b)


def _split3(x):
    hi = x.astype(BF16)
    r1 = x - hi.astype(F32)
    mid = r1.astype(BF16)
    lo = (r1 - mid.astype(F32)).astype(BF16)
    return hi, mid, lo


def _bias_placement():
    pq = np.zeros((LANES, FOX_HEADS * HEAD_PAD), np.float32)
    pk = np.zeros_like(pq)
    ones_q = np.zeros((1, FOX_HEADS * HEAD_PAD), np.float32)
    ones_k = np.zeros_like(ones_q)
    for h in range(FOX_HEADS):
        for piece in range(3):
            pq[piece * FOX_HEADS + h, h * HEAD_PAD + FOX_HEAD_DIM + piece] = 1.0
            pk[piece * FOX_HEADS + h, h * HEAD_PAD + FOX_HEAD_DIM + 3 + piece] = -1.0
            ones_q[0, h * HEAD_PAD + FOX_HEAD_DIM + 3 + piece] = 1.0
            ones_k[0, h * HEAD_PAD + FOX_HEAD_DIM + piece] = 1.0
    return (jnp.asarray(pq, BF16), jnp.asarray(pk, BF16), jnp.asarray(ones_q), jnp.asarray(ones_k))


def _fcum_chunk(zf_ref, bf_ref, v_ref, tril_ref, pq_ref, pk_ref, oq_ref, ok_ref, qadd_ref, kadd_ref, vt_ref,
                carry_ref):
    tril = tril_ref[...]
    lane = lax.broadcasted_iota(jnp.int32, (T_SCAN, LANES), 1)
    ones_row = lax.broadcasted_iota(jnp.int32, (HEAD_PAD, T_SCAN), 0) == FOX_HEAD_DIM
    for b in range(BATCH):
        for n in range(FOX_HEADS):
            v_t = v_ref[b, :, n * HEAD_PAD:(n + 1) * HEAD_PAD].astype(F32).T
            vt_ref[b, n * HEAD_PAD:(n + 1) * HEAD_PAD, :] = jnp.where(ones_row, 1.0, v_t).astype(BF16)
        logit = zf_ref[b] + bf_ref[...]
        log_f = -_softplus(-logit)
        hi, mid, lo = _split3(log_f)
        cs = (jnp.dot(tril, hi, preferred_element_type=F32)
              + jnp.dot(tril, mid, preferred_element_type=F32)
              + jnp.dot(tril, lo, preferred_element_type=F32))
        cum = cs + carry_ref[b]
        carry_ref[b] = cum[T_SCAN - 1:T_SCAN, :]
        hi, mid, lo = [p.astype(F32) for p in _split3(cum * LOG2E)]
        pieces = jnp.where(lane < FOX_HEADS, hi,
                           jnp.where(lane < 2 * FOX_HEADS, pltpu.roll(mid, FOX_HEADS, axis=1),
                                     jnp.where(lane < 3 * FOX_HEADS, pltpu.roll(lo, 2 * FOX_HEADS, axis=1), 0.0))
                           ).astype(BF16)
        qadd_ref[b] = (jnp.dot(pieces, pq_ref[...], preferred_element_type=F32) + oq_ref[...]).astype(BF16)
        kadd_ref[b] = (jnp.dot(pieces, pk_ref[...], preferred_element_type=F32) + ok_ref[...]).astype(BF16)


def _s5_load(z_ref, bbd_ref, s_ref):
    for b in range(BATCH):
        u = z_ref[b, :, 0:MIXER_WIDTH]
        bu = jnp.dot(u.astype(BF16), bbd_ref[...], preferred_element_type=F32)
        for j in range(S5_ROWS):
            s_ref[b, pl.ds(j, T_SCAN, stride=S5_PITCH), :] = bu[:, j * LANES:(j + 1) * LANES]


def _s5_step(t, carry, lam_re, lam_im, s_ref):
    base = t * S5_PITCH
    new = []
    for b in range(BATCH):
        h_re, h_im = carry[2 * b], carry[2 * b + 1]
        n_re = lam_re * h_re - lam_im * h_im + s_ref[b, pl.ds(base, SUBLANES), :]
        n_im = lam_re * h_im + lam_im * h_re + s_ref[b, pl.ds(base + SUBLANES, SUBLANES), :]
        s_ref[b, pl.ds(base, SUBLANES), :] = n_re
        s_ref[b, pl.ds(base + SUBLANES, SUBLANES), :] = n_im
        new += [n_re, n_im]
    return new


def _s5_out(z_ref, cbd_ref, d_ref, gw_ref, gb_ref, o_ref, s_ref):
    states = jnp.concatenate(
        [jnp.concatenate([s_ref[b, pl.ds(j, T_SCAN, stride=S5_PITCH), :].astype(BF16) for j in range(S5_ROWS)],
                         axis=1) for b in range(BATCH)], axis=0)
    u = z_ref[:, :, 0:MIXER_WIDTH].reshape(BATCH * T_SCAN, MIXER_WIDTH)
    y = jnp.dot(states, cbd_ref[...], preferred_element_type=F32) + d_ref[...] * u
    y = jax.nn.gelu(y)
    gate = jnp.dot(y.astype(BF16), gw_ref[...], preferred_element_type=F32) + gb_ref[...]
    o_ref[...] = (y * jax.nn.sigmoid(gate)).astype(MIX_OUT).reshape(BATCH, T_SCAN, MIXER_WIDTH)


LRU_SLOTS = BATCH * MIXER_WIDTH // LANES
LRU_PITCH = LRU_SLOTS + SCAN_ROW_GAP
LRU_TILES = MIXER_WIDTH // LANES


def _lru_load(z_ref, cw_ref, cb_ref, wa_ref, ba_ref, wx_ref, bx_ref, lam_ref, tail_ref, a_ref, b_ref):
    decay_rate = LRU_C * _softplus(-lam_ref[...])
    for b in range(BATCH):
        x = z_ref[b, :, MIXER_WIDTH:2 * MIXER_WIDTH]
        xp = jnp.concatenate([tail_ref[b], x], axis=0)
        tail_ref[b] = x[T_SCAN - SUBLANES:T_SCAN, :]
        xc = cb_ref[...]
        for k in range(LRU_CONV):
            off = SUBLANES - (LRU_CONV - 1) + k
            xc = xc + cw_ref[k:k + 1, :] * xp[off:off + T_SCAN, :]
        xcb = xc.astype(BF16)
        r = jax.nn.sigmoid(jnp.dot(xcb, wa_ref[...], preferred_element_type=F32) + ba_ref[...])
        i = jax.nn.sigmoid(jnp.dot(xcb, wx_ref[...], preferred_element_type=F32) + bx_ref[...])
        log_a = -(r * decay_rate)
        a = jnp.exp(log_a)
        inp = jnp.sqrt(-jnp.tanh(log_a) * (a * a + 1.0)) * (i * xc)
        for j in range(LRU_TILES):
            slot = b * LRU_TILES + j
            a_ref[pl.ds(slot, T_SCAN, stride=LRU_PITCH), :] = a[:, j * LANES:(j + 1) * LANES]
            b_ref[pl.ds(slot, T_SCAN, stride=LRU_PITCH), :] = inp[:, j * LANES:(j + 1) * LANES]


def _lru_step(t, h, a_ref, b_ref):
    base = t * LRU_PITCH
    h = a_ref[pl.ds(base, LRU_SLOTS), :] * h + b_ref[pl.ds(base, LRU_SLOTS), :]
    b_ref[pl.ds(base, LRU_SLOTS), :] = h
    return h


def _lru_out(z_ref, o_ref, b_ref):
    for b in range(BATCH):
        h = jnp.concatenate(
            [b_ref[pl.ds(b * LRU_TILES + j, T_SCAN, stride=LRU_PITCH), :] for j in range(LRU_TILES)], axis=1)
        o_ref[b] = (h * jax.nn.gelu(z_ref[b, :, 2 * MIXER_WIDTH:3 * MIXER_WIDTH])).astype(MIX_OUT)


def _mixers_kernel(z_ref, zf_ref, v_ref,
                   bbd_ref, lam5_ref, cbd_ref, d_ref, gw_ref, gb_ref,
                   cw_ref, cb_ref, wa_ref, ba_ref, wx_ref, bx_ref, lam_ref,
                   bf_ref, tril_ref, pq_ref, pk_ref, oq_ref, ok_ref,
                   yb_ref, yc_ref, qadd_ref, kadd_ref, vt_ref,
                   s_ref, h5_ref, tail_ref, a_ref, b_ref, hl_ref, carry_ref):
    @pl.when(pl.program_id(0) == 0)
    def _():
        h5_ref[...] = jnp.zeros_like(h5_ref)
        tail_ref[...] = jnp.zeros_like(tail_ref)
        hl_ref[...] = jnp.zeros_like(hl_ref)
        carry_ref[...] = jnp.zeros_like(carry_ref)

    _s5_load(z_ref, bbd_ref, s_ref)
    _lru_load(z_ref, cw_ref, cb_ref, wa_ref, ba_ref, wx_ref, bx_ref, lam_ref, tail_ref, a_ref, b_ref)
    _fcum_chunk(zf_ref, bf_ref, v_ref, tril_ref, pq_ref, pk_ref, oq_ref, ok_ref, qadd_ref, kadd_ref, vt_ref,
                carry_ref)

    lam_re = lam5_ref[0:SUBLANES, :]
    lam_im = lam5_ref[SUBLANES:S5_ROWS, :]

    def step(t, carry):
        new5 = _s5_step(t, carry[:-1], lam_re, lam_im, s_ref)
        return tuple(new5) + (_lru_step(t, carry[-1], a_ref, b_ref),)

    init = []
    for b in range(BATCH):
        init += [h5_ref[b, 0:SUBLANES, :], h5_ref[b, SUBLANES:S5_ROWS, :]]
    fin = lax.fori_loop(0, T_SCAN, step, tuple(init) + (hl_ref[...],), unroll=8)
    for b in range(BATCH):
        h5_ref[b, 0:SUBLANES, :] = fin[2 * b]
        h5_ref[b, SUBLANES:S5_ROWS, :] = fin[2 * b + 1]
    hl_ref[...] = fin[-1]

    _s5_out(z_ref, cbd_ref, d_ref, gw_ref, gb_ref, yb_ref, s_ref)
    _lru_out(z_ref, yc_ref, b_ref)


def _mixers(zmix3, zf3, qkv3, s5_params, lru_params, bf, layer):
    bbd, lam5, cbd, d, gw, gb = s5_params
    cw, cb, wa, ba, wx, bx, lam = lru_params
    pq, pk, ones_q, ones_k = _bias_placement()
    tril = jnp.asarray(np.tril(np.ones((T_SCAN, T_SCAN), np.float32)), BF16)
    full = lambda c: (0, 0)
    chunk = lambda c: (0, c, 0)
    wide = FOX_HEADS * HEAD_PAD
    vec = _layer_block((1, MIXER_WIDTH), layer)
    mat = _layer_block((MIXER_WIDTH, MIXER_WIDTH), layer)
    mix_out = pl.BlockSpec((BATCH, T_SCAN, MIXER_WIDTH), chunk)
    bias_out = pl.BlockSpec((BATCH, T_SCAN, wide), chunk)
    return pl.pallas_call(
        _mixers_kernel,
        grid=(SEQ // T_SCAN,),
        in_specs=[pl.BlockSpec((BATCH, T_SCAN, ZMIX_COLS), chunk),
                  pl.BlockSpec((BATCH, T_SCAN, LANES), chunk),
                  pl.BlockSpec((BATCH, T_SCAN, wide), lambda c: (0, c, 2)),
                  _layer_block((MIXER_WIDTH, 2 * S5_NSTATE), layer),
                  _layer_block((S5_ROWS, LANES), layer),
                  _layer_block((2 * S5_NSTATE, MIXER_WIDTH), layer),
                  vec, mat, vec,
                  _layer_block((LRU_CONV, MIXER_WIDTH), layer), vec, mat, vec, mat, vec, vec,
                  _layer_block((1, LANES), layer),
                  pl.BlockSpec((T_SCAN, T_SCAN), full),
                  pl.BlockSpec((LANES, wide), full),
                  pl.BlockSpec((LANES, wide), full),
                  pl.BlockSpec((1, wide), full),
                  pl.BlockSpec((1, wide), full)],
        out_specs=[mix_out, mix_out, bias_out, bias_out,
                   pl.BlockSpec((BATCH, wide, T_SCAN), lambda c: (0, 0, c))],
        out_shape=[jax.ShapeDtypeStruct((BATCH, SEQ, MIXER_WIDTH), MIX_OUT),
                   jax.ShapeDtypeStruct((BATCH, SEQ, MIXER_WIDTH), MIX_OUT),
                   jax.ShapeDtypeStruct((BATCH, SEQ, wide), BF16),
                   jax.ShapeDtypeStruct((BATCH, SEQ, wide), BF16),
                   jax.ShapeDtypeStruct((BATCH, wide, SEQ), BF16)],
        scratch_shapes=[pltpu.VMEM((BATCH, T_SCAN * S5_PITCH, LANES), F32),
                        pltpu.VMEM((BATCH, S5_ROWS, LANES), F32),
                        pltpu.VMEM((BATCH, SUBLANES, MIXER_WIDTH), F32),
                        pltpu.VMEM((T_SCAN * LRU_PITCH, LANES), F32),
                        pltpu.VMEM((T_SCAN * LRU_PITCH, LANES), F32),
                        pltpu.VMEM((LRU_SLOTS, LANES), F32),
                        pltpu.VMEM((BATCH, 1, LANES), F32)],
        compiler_params=_params(1, VMEM_LIMIT_MIXERS),
        name="recurrent_mixers",
    )(zmix3, zf3, qkv3, bbd, lam5, cbd, d, gw, gb, cw, cb, wa, ba, wx, bx, lam, bf, tril, pq, pk, ones_q, ones_k)


def _attn_kernel(q_ref, qadd_ref, k_ref, kadd_ref, vt_ref, o_ref, s_ref, m_ref, acc_ref):
    i = pl.program_id(2)
    scale = FOX_HEAD_DIM ** -0.5 * LOG2E
    neg = jnp.finfo(F32).min
    slots = [slice(n * HEAD_PAD, (n + 1) * HEAD_PAD) for n in range(FOX_HEADS)]
    qs = [(q_ref[:, sl].astype(F32) * scale + qadd_ref[:, sl].astype(F32)).astype(BF16) for sl in slots]

    def logits_t(n, j):
        start = pl.multiple_of(j * TK, TK)
        ks = k_ref[pl.ds(start, TK), slots[n]] + kadd_ref[pl.ds(start, TK), slots[n]]
        return lax.dot_general(ks, qs[n], (((1,), (1,)), ((), ())), preferred_element_type=F32)

    def block(j, masked, prefetch):
        start = pl.multiple_of(j * TK, TK)
        for n in range(FOX_HEADS):
            s = s_ref[n]
            if masked:
                key = lax.broadcasted_iota(jnp.int32, (TK, TQ), 0)
                qry = lax.broadcasted_iota(jnp.int32, (TK, TQ), 1)
                s = jnp.where(key <= qry, s, neg)
            m = m_ref[n]
            m_new = jnp.maximum(m, jnp.max(s, axis=0, keepdims=True))
            alpha = jnp.exp2(m - m_new)
            p = jnp.exp2(s - jnp.concatenate([m_new] * (TK // SUBLANES), axis=0))
            m_ref[n] = m_new
            vt = vt_ref[0, n * HEAD_PAD:n * HEAD_PAD + ACC_ROWS, pl.ds(start, TK)]
            acc_ref[n] = (jnp.concatenate([alpha] * (ACC_ROWS // SUBLANES), axis=0) * acc_ref[n]
                          + jnp.dot(vt, p.astype(BF16), preferred_element_type=F32))
            if prefetch:
                s_ref[n] = logits_t(n, j + 1)

    for n in range(FOX_HEADS):
        s_ref[n] = logits_t(n, 0)
    m_ref[...] = jnp.full(m_ref.shape, neg, F32)
    acc_ref[...] = jnp.zeros_like(acc_ref)

    @pl.loop(0, i // 2)
    def _(jj):
        block(2 * jj, False, True)
        block(2 * jj + 1, False, True)

    @pl.when(i % 2 == 1)
    def _():
        block(i - 1, False, True)

    block(i, True, False)
    for n, sl in enumerate(slots):
        acc = acc_ref[n]
        out_t = acc[0:FOX_HEAD_DIM] / acc[FOX_HEAD_DIM:FOX_HEAD_DIM + 1]
        out_t = jnp.concatenate([out_t, jnp.zeros((HEAD_PAD - FOX_HEAD_DIM, TQ), F32)], axis=0)
        o_ref[:, sl] = out_t.T.astype(MIX_OUT)


def _attn(qkv, qadd, kadd, vt):
    n_q = SEQ // TQ
    width = FOX_HEADS * HEAD_PAD
    qrow = lambda b, h, i: (b * n_q + i, 0)
    return pl.pallas_call(
        _attn_kernel,
        grid=(BATCH, 1, n_q),
        in_specs=[pl.BlockSpec((TQ, width), qrow),
                  pl.BlockSpec((TQ, width), qrow),
                  pl.BlockSpec((SEQ, width), lambda b, h, i: (b, 1)),
                  pl.BlockSpec((SEQ, width), lambda b, h, i: (b, 0)),
                  pl.BlockSpec((1, width, SEQ), lambda b, h, i: (b, 0, 0))],
        out_specs=pl.BlockSpec((TQ, width), qrow),
        out_shape=jax.ShapeDtypeStruct((N_TOK, width), MIX_OUT),
        scratch_shapes=[pltpu.VMEM((FOX_HEADS, TK, TQ), F32),
                        pltpu.VMEM((FOX_HEADS, SUBLANES, TQ), F32),
                        pltpu.VMEM((FOX_HEADS, ACC_ROWS, TQ), F32)],
        compiler_params=_params(3),
        name="fox_attention",
    )(qkv, qadd, qkv, kadd, vt)


def _merge_mlp_kernel(ya_ref, yb_ref, yc_ref, yd_ref, x_ref, gm_ref, gmd_ref, wo_ref, g2_ref, w1_ref, w2_ref,
                      fg_ref, o_ref, h_ref, *, final_norm):
    j = pl.program_id(1)

    @pl.when(j == 0)
    def _():
        w = MIXER_WIDTH
        parts = [_rms(ya_ref[...].astype(F32), gm_ref[:, 0:w], w),
                 _rms(yb_ref[...].astype(F32), gm_ref[:, w:2 * w], w),
                 _rms(yc_ref[...].astype(F32), gm_ref[:, 2 * w:3 * w], w),
                 _rms(yd_ref[...].astype(F32), gmd_ref[...], w)]
        y = jnp.concatenate(parts, axis=1).astype(BF16)
        x1 = x_ref[...] + jnp.dot(y, wo_ref[...], preferred_element_type=F32)
        o_ref[...] = x1
        h_ref[...] = _rms(x1, g2_ref[...], D_MODEL).astype(BF16)

    a = jnp.dot(h_ref[...], w1_ref[...], preferred_element_type=F32)
    a = jnp.square(jnp.maximum(a, 0.0)).astype(BF16)
    o_ref[...] += jnp.dot(a, w2_ref[...], preferred_element_type=F32)

    if final_norm:
        @pl.when(j == pl.num_programs(1) - 1)
        def _():
            o_ref[...] = _rms(o_ref[...], fg_ref[...], D_MODEL)


def _merge_mlp(ya, yb, yc, yd, x, gm, gmd, wo, g2, w1, w2, fg, layer, final_norm):
    row = lambda i, j: (i, 0)
    k_dim = 3 * MIXER_WIDTH + FOX_HEADS * HEAD_PAD
    mix = pl.BlockSpec((TM_MLP, MIXER_WIDTH), row)
    return pl.pallas_call(
        functools.partial(_merge_mlp_kernel, final_norm=final_norm),
        grid=(N_TOK // TM_MLP, D_FF // TF_MLP),
        in_specs=[mix, mix, mix,
                  pl.BlockSpec((TM_MLP, FOX_HEADS * HEAD_PAD), row),
                  pl.BlockSpec((TM_MLP, D_MODEL), row),
                  _layer_block((1, 3 * MIXER_WIDTH), layer),
                  _layer_block((1, FOX_HEADS * HEAD_PAD), layer),
                  _layer_block((k_dim, D_MODEL), layer),
                  _layer_block((1, D_MODEL), layer),
                  pl.BlockSpec((None, D_MODEL, TF_MLP), lambda i, j: (layer, 0, j)),
                  pl.BlockSpec((None, TF_MLP, D_MODEL), lambda i, j: (layer, j, 0)),
                  pl.BlockSpec((1, D_MODEL), lambda i, j: (0, 0))],
        out_specs=pl.BlockSpec((TM_MLP, D_MODEL), row),
        out_shape=jax.ShapeDtypeStruct((N_TOK, D_MODEL), F32),
        scratch_shapes=[pltpu.VMEM((TM_MLP, D_MODEL), BF16)],
        compiler_params=_params(2),
        name="merge_mlp",
    )(ya, yb, yc, yd, x, gm, gmd, wo, g2, w1, w2, fg)


def _pad_heads(w):
    lead = w.shape[:-1]
    w = w.reshape(*lead, FOX_HEADS, FOX_HEAD_DIM)
    w = jnp.pad(w, [(0, 0)] * (len(lead) + 1) + [(0, HEAD_PAD - FOX_HEAD_DIM)])
    return w.reshape(*lead, FOX_HEADS * HEAD_PAD)


def _block_diag(blocks):
    n, g, r, c = blocks.shape
    eye = jnp.eye(g, dtype=blocks.dtype)
    return jnp.einsum('ngrc,gh->ngrhc', blocks, eye).reshape(n, g * r, g * c)


def _s5_discretize(lam_re, lam_im, log_dt, b_re, b_im):
    dt = jnp.exp(log_dt)[..., None]
    mag = jnp.exp(lam_re * dt)
    abar_re = mag * jnp.cos(lam_im * dt)
    abar_im = mag * jnp.sin(lam_im * dt)
    denom = jnp.square(lam_re) + jnp.square(lam_im)
    num_re = abar_re - 1.0
    num_im = abar_im
    fac_re = (num_re * lam_re + num_im * lam_im) / denom
    fac_im = (num_im * lam_re - num_re * lam_im) / denom
    bbar_re = fac_re[..., None] * b_re - fac_im[..., None] * b_im
    bbar_im = fac_re[..., None] * b_im + fac_im[..., None] * b_re
    return abar_re, abar_im, bbar_re, bbar_im


def kernel(x, norm1_g, w_in, sgu_norm_g, sgu_w, sgu_b, s5_lambda_re, s5_lambda_im, s5_log_dt, s5_b_re, s5_b_im, s5_c_re, s5_c_im, s5_d, s5_glu_w, s5_glu_b, lru_conv_w, lru_conv_b, lru_wa, lru_ba, lru_wx, lru_bx, lru_lambda, fox_fgate_b, mix_norm_g, w_out, norm2_g, w_mlp_in, w_mlp_out, final_g):
    w = MIXER_WIDTH
    row = lambda v: v.reshape(DEPTH, 1, -1)

    w_in_p = jnp.pad(w_in.astype(BF16), ((0, 0), (0, 0), (0, W_IN_COLS - w_in.shape[-1])))
    bf = row(jnp.pad(fox_fgate_b, ((0, 0), (0, LANES - FOX_HEADS))))

    sgu_wcat = jnp.transpose(sgu_w, (0, 2, 1, 3)).reshape(DEPTH, SGU_CHUNK, SGU_HEADS * SGU_CHUNK)
    sgu_bias = jnp.repeat(jnp.transpose(sgu_b, (0, 2, 1)), w // SGU_HEADS, axis=2)

    abar_re, abar_im, bbar_re, bbar_im = _s5_discretize(s5_lambda_re, s5_lambda_im, s5_log_dt, s5_b_re, s5_b_im)
    swap = lambda t: jnp.transpose(t, (0, 1, 3, 2))
    bbd = jnp.concatenate([_block_diag(swap(bbar_re)), _block_diag(swap(bbar_im))], axis=2).astype(BF16)
    cbd = jnp.concatenate([_block_diag(swap(s5_c_re)), -_block_diag(swap(s5_c_im))], axis=1).astype(BF16)
    lam = jnp.concatenate([abar_re.reshape(DEPTH, SUBLANES, LANES), abar_im.reshape(DEPTH, SUBLANES, LANES)], axis=1)
    glu_w = s5_glu_w.astype(BF16)

    wa_bd = _block_diag(lru_wa).astype(BF16)
    wx_bd = _block_diag(lru_wx).astype(BF16)

    w_o_pad = jnp.concatenate(
        [w_out[:, 0:3 * w],
         jnp.pad(w_out[:, 3 * w:].reshape(DEPTH, FOX_HEADS, FOX_HEAD_DIM, D_MODEL),
                 ((0, 0), (0, 0), (0, HEAD_PAD - FOX_HEAD_DIM), (0, 0))).reshape(DEPTH, FOX_HEADS * HEAD_PAD, D_MODEL)],
        axis=1).astype(BF16)
    g_mix = row(mix_norm_g[:, 0:3 * w])
    g_mix_d = _pad_heads(row(mix_norm_g[:, 3 * w:]))
    w1 = w_mlp_in.astype(BF16)
    w2 = w_mlp_out.astype(BF16)
    g1, g2 = row(norm1_g), row(norm2_g)
    fg = final_g.reshape(1, D_MODEL)

    xf = x.reshape(N_TOK, D_MODEL)
    for l in range(DEPTH):
        y_a, zmix, qkv, zf = _in_proj(xf, g1, w_in_p, row(sgu_norm_g), sgu_wcat, sgu_bias, l)
        zmix3 = zmix.reshape(BATCH, SEQ, ZMIX_COLS)
        y_b, y_c, qadd, kadd, vt = _mixers(
            zmix3, zf.reshape(BATCH, SEQ, LANES), qkv.reshape(BATCH, SEQ, QKV_COLS),
            (bbd, lam, cbd, row(s5_d), glu_w, row(s5_glu_b)),
            (lru_conv_w, row(lru_conv_b), wa_bd, row(lru_ba.reshape(DEPTH, w)), wx_bd,
             row(lru_bx.reshape(DEPTH, w)), row(lru_lambda)),
            bf, l)
        qadd = qadd.reshape(N_TOK, FOX_HEADS * HEAD_PAD)
        kadd = kadd.reshape(N_TOK, FOX_HEADS * HEAD_PAD)
        y_d = _attn(qkv, qadd, kadd, vt)
        xf = _merge_mlp(y_a, y_b.reshape(N_TOK, w), y_c.reshape(N_TOK, w), y_d, xf, g_mix, g_mix_d, w_o_pad,
                        g2, w1, w2, fg, l, final_norm=(l == DEPTH - 1))
    return xf.reshape(BATCH, SEQ, D_MODEL)
```

```python
import functools

import jax
import jax.numpy as jnp
import numpy as np
from jax import lax
from jax.experimental import pallas as pl
from jax.experimental.pallas import tpu as pltpu

D_MODEL = 1024
BATCH = 4
SEQ = 4096
DEPTH = 4
N_TOK = BATCH * SEQ
MIXER_WIDTH = 256
SGU_HEADS = 4
SGU_CHUNK = 128
S5_GROUP = 16
S5_GROUPS = 16
S5_STATE = 64
LRU_HEADS = 4
LRU_CONV = 4
LRU_C = 8.0
FOX_HEADS = 4
FOX_HEAD_DIM = 64
D_FF = 4 * D_MODEL
RMS_EPS = 1e-6
LOG2E = 1.4426950408889634

LANES = 128
SUBLANES = 8
HEAD_PAD = LANES
ZMIX_COLS = 3 * MIXER_WIDTH
QKV_COLS = 3 * FOX_HEADS * HEAD_PAD
S5_NSTATE = S5_GROUPS * S5_STATE
S5_ROWS = 2 * S5_NSTATE // LANES
SCAN_ROW_GAP = 4
S5_PITCH = S5_ROWS + SCAN_ROW_GAP

TM_IN = 1024
TM_MLP = 1024
TF_MLP = 1024
T_SCAN = 256
TQ = 512
TK = 512
ACC_ROWS = 80
VMEM_LIMIT = 48 * 1024 * 1024
VMEM_LIMIT_MIXERS = 56 * 1024 * 1024

F32 = jnp.float32
BF16 = jnp.bfloat16
MIX_OUT = BF16


def _params(n_axes, vmem_limit=VMEM_LIMIT):
    return pltpu.CompilerParams(dimension_semantics=("arbitrary",) * n_axes,
                                vmem_limit_bytes=vmem_limit)


def _layer_block(shape, layer):
    zeros = (0,) * len(shape)
    return pl.BlockSpec((None,) + tuple(shape), lambda *_: (layer,) + zeros)


def _rms(x, g, width):
    ms = jnp.sum(jnp.square(x), axis=-1, keepdims=True) * (1.0 / width)
    return x * lax.rsqrt(ms + RMS_EPS) * g


def _softplus(x):
    return jnp.maximum(x, 0.0) + jnp.log1p(jnp.exp(-jnp.abs(x)))


def _sgu_mix(zu, zv, g, w, bias, o_ref):
    n_rows = zu.shape[0]
    u = jax.nn.gelu(zu)
    v = _rms(jax.nn.gelu(zv), g, MIXER_WIDTH)
    hd = MIXER_WIDTH // SGU_HEADS
    lane_head = lax.broadcasted_iota(jnp.int32, (SGU_CHUNK, MIXER_WIDTH), 1) // hd
    t_idx = lax.broadcasted_iota(jnp.int32, (SGU_CHUNK, SGU_HEADS * SGU_CHUNK), 0)
    s_idx = lax.broadcasted_iota(jnp.int32, (SGU_CHUNK, SGU_HEADS * SGU_CHUNK), 1) % SGU_CHUNK
    wm = jnp.where(s_idx <= t_idx, w, 0.0).astype(BF16)
    for c in range(n_rows // SGU_CHUNK):
        rows = slice(c * SGU_CHUNK, (c + 1) * SGU_CHUNK)
        vc = v[rows]
        vstack = jnp.concatenate(
            [jnp.where(lane_head == h, vc, 0.0) for h in range(SGU_HEADS)], axis=0).astype(BF16)
        mixed = jnp.dot(wm, vstack, preferred_element_type=F32) + bias
        o_ref[rows, :] = (u[rows] * mixed).astype(MIX_OUT)


def _in_proj_kernel(x_ref, pv_ref, w_ref, wf_ref, sw_ref, sb_ref, ya_ref, zmix_ref, qkv_ref, zf_ref):
    w = MIXER_WIDTH
    h = _rms(x_ref[...], pv_ref[:, 0:D_MODEL], D_MODEL).astype(BF16)
    z5 = jnp.dot(h, w_ref[:, 0:5 * w], preferred_element_type=F32)
    zmix_ref[...] = z5[:, 2 * w:5 * w]
    _sgu_mix(z5[:, 0:w], z5[:, w:2 * w], pv_ref[:, D_MODEL:D_MODEL + w], sw_ref[...], sb_ref[...], ya_ref)
    w_qkvf = jnp.concatenate([w_ref[:, 5 * w:8 * w], wf_ref[...]], axis=1)
    zqkvf = jnp.dot(h, w_qkvf, preferred_element_type=F32)
    zf_ref[...] = zqkvf[:, 3 * w:]
    pad = jnp.zeros((TM_IN, HEAD_PAD - FOX_HEAD_DIM), F32)
    for slot in range(3 * FOX_HEADS):
        head = zqkvf[:, slot * FOX_HEAD_DIM:(slot + 1) * FOX_HEAD_DIM]
        qkv_ref[:, slot * HEAD_PAD:(slot + 1) * HEAD_PAD] = jnp.concatenate([head, pad], axis=1).astype(BF16)


def _in_proj(x, vecs, w, wf, sgu_w, sgu_b, layer):
    row = lambda i: (i, 0)
    return pl.pallas_call(
        _in_proj_kernel,
        grid=(N_TOK // TM_IN,),
        in_specs=[pl.BlockSpec((TM_IN, D_MODEL), row),
                  _layer_block((1, D_MODEL + MIXER_WIDTH), layer),
                  _layer_block((D_MODEL, 8 * MIXER_WIDTH), layer),
                  _layer_block((D_MODEL, LANES), layer),
                  _layer_block((SGU_CHUNK, SGU_HEADS * SGU_CHUNK), layer),
                  _layer_block((SGU_CHUNK, MIXER_WIDTH), layer)],
        out_specs=[pl.BlockSpec((TM_IN, MIXER_WIDTH), row),
                   pl.BlockSpec((TM_IN, ZMIX_COLS), row),
                   pl.BlockSpec((TM_IN, QKV_COLS), row),
                   pl.BlockSpec((TM_IN, LANES), row)],
        out_shape=[jax.ShapeDtypeStruct((N_TOK, MIXER_WIDTH), MIX_OUT),
                   jax.ShapeDtypeStruct((N_TOK, ZMIX_COLS), F32),
                   jax.ShapeDtypeStruct((N_TOK, QKV_COLS), BF16),
                   jax.ShapeDtypeStruct((N_TOK, LANES), F32)],
        compiler_params=_params(1),
        name="in_proj_sgu",
    )(x, vecs, w, wf, sgu_w, sgu_b)


def _split3(x):
    hi = x.astype(BF16)
    r1 = x - hi.astype(F32)
    mid = r1.astype(BF16)
    lo = (r1 - mid.astype(F32)).astype(BF16)
    return hi, mid, lo


def _bias_placement():
    pq = np.zeros((LANES, FOX_HEADS * HEAD_PAD), np.float32)
    pk = np.zeros_like(pq)
    ones_q = np.zeros((1, FOX_HEADS * HEAD_PAD), np.float32)
    ones_k = np.zeros_like(ones_q)
    for h in range(FOX_HEADS):
        for piece in range(3):
            pq[piece * FOX_HEADS + h, h * HEAD_PAD + FOX_HEAD_DIM + piece] = 1.0
            pk[piece * FOX_HEADS + h, h * HEAD_PAD + FOX_HEAD_DIM + 3 + piece] = -1.0
            ones_q[0, h * HEAD_PAD + FOX_HEAD_DIM + 3 + piece] = 1.0
            ones_k[0, h * HEAD_PAD + FOX_HEAD_DIM + piece] = 1.0
    return (jnp.asarray(pq, BF16), jnp.asarray(pk, BF16), jnp.asarray(ones_q), jnp.asarray(ones_k))


def _fcum_chunk(zf_ref, bf_ref, v_ref, tril_ref, pq_ref, pk_ref, oq_ref, ok_ref, qadd_ref, kadd_ref, vt_ref,
                carry_ref, batches):
    tril = tril_ref[...]
    lane = lax.broadcasted_iota(jnp.int32, (T_SCAN, LANES), 1)
    ones_row = lax.broadcasted_iota(jnp.int32, (HEAD_PAD, T_SCAN), 0) == FOX_HEAD_DIM
    for b in batches:
        for n in range(FOX_HEADS):
            v_t = v_ref[b, :, n * HEAD_PAD:(n + 1) * HEAD_PAD].astype(F32).T
            vt_ref[b, n * HEAD_PAD:(n + 1) * HEAD_PAD, :] = jnp.where(ones_row, 1.0, v_t).astype(BF16)
        logit = zf_ref[b] + bf_ref[...]
        log_f = -_softplus(-logit)
        hi, mid, lo = _split3(log_f)
        cs = (jnp.dot(tril, hi, preferred_element_type=F32)
              + jnp.dot(tril, mid, preferred_element_type=F32)
              + jnp.dot(tril, lo, preferred_element_type=F32))
        cum = cs + carry_ref[b]
        carry_ref[b] = cum[T_SCAN - 1:T_SCAN, :]
        hi, mid, lo = [p.astype(F32) for p in _split3(cum * LOG2E)]
        pieces = jnp.where(lane < FOX_HEADS, hi,
                           jnp.where(lane < 2 * FOX_HEADS, pltpu.roll(mid, FOX_HEADS, axis=1),
                                     jnp.where(lane < 3 * FOX_HEADS, pltpu.roll(lo, 2 * FOX_HEADS, axis=1), 0.0))
                           ).astype(BF16)
        qadd_ref[b] = (jnp.dot(pieces, pq_ref[...], preferred_element_type=F32) + oq_ref[...]).astype(BF16)
        kadd_ref[b] = (jnp.dot(pieces, pk_ref[...], preferred_element_type=F32) + ok_ref[...]).astype(BF16)


def _s5_load(z_ref, bbd_ref, s_ref, batches):
    for b in batches:
        u = z_ref[b, :, 0:MIXER_WIDTH]
        bu = jnp.dot(u.astype(BF16), bbd_ref[...], preferred_element_type=F32)
        for j in range(S5_ROWS):
            s_ref[b, pl.ds(j, T_SCAN, stride=S5_PITCH), :] = bu[:, j * LANES:(j + 1) * LANES]


def _s5_step(t, carry, lam_re, lam_im, s_ref):
    base = t * S5_PITCH
    new = []
    for b in range(BATCH):
        h_re, h_im = carry[2 * b], carry[2 * b + 1]
        n_re = lam_re * h_re - lam_im * h_im + s_ref[b, pl.ds(base, SUBLANES), :]
        n_im = lam_re * h_im + lam_im * h_re + s_ref[b, pl.ds(base + SUBLANES, SUBLANES), :]
        s_ref[b, pl.ds(base, SUBLANES), :] = n_re
        s_ref[b, pl.ds(base + SUBLANES, SUBLANES), :] = n_im
        new += [n_re, n_im]
    return new


def _s5_out(z_ref, cbd_ref, d_ref, gw_ref, gb_ref, o_ref, s_ref):
    states = jnp.concatenate(
        [jnp.concatenate([s_ref[b, pl.ds(j, T_SCAN, stride=S5_PITCH), :].astype(BF16) for j in range(S5_ROWS)],
                         axis=1) for b in range(BATCH)], axis=0)
    u = z_ref[:, :, 0:MIXER_WIDTH].reshape(BATCH * T_SCAN, MIXER_WIDTH)
    y = jnp.dot(states, cbd_ref[...], preferred_element_type=F32) + d_ref[...] * u
    y = jax.nn.gelu(y)
    gate = jnp.dot(y.astype(BF16), gw_ref[...], preferred_element_type=F32) + gb_ref[...]
    o_ref[...] = (y * jax.nn.sigmoid(gate)).astype(MIX_OUT).reshape(BATCH, T_SCAN, MIXER_WIDTH)


LRU_SLOTS = BATCH * MIXER_WIDTH // LANES
LRU_PITCH = LRU_SLOTS + SCAN_ROW_GAP
LRU_TILES = MIXER_WIDTH // LANES


def _lru_load(z_ref, cw_ref, cb_ref, wa_ref, ba_ref, wx_ref, bx_ref, lam_ref, tail_ref, a_ref, b_ref, batches):
    decay_rate = LRU_C * _softplus(-lam_ref[...])
    for b in batches:
        x = z_ref[b, :, MIXER_WIDTH:2 * MIXER_WIDTH]
        xp = jnp.concatenate([tail_ref[b], x], axis=0)
        tail_ref[b] = x[T_SCAN - SUBLANES:T_SCAN, :]
        xc = cb_ref[...]
        for k in range(LRU_CONV):
            off = SUBLANES - (LRU_CONV - 1) + k
            xc = xc + cw_ref[k:k + 1, :] * xp[off:off + T_SCAN, :]
        xcb = xc.astype(BF16)
        r = jax.nn.sigmoid(jnp.dot(xcb, wa_ref[...], preferred_element_type=F32) + ba_ref[...])
        i = jax.nn.sigmoid(jnp.dot(xcb, wx_ref[...], preferred_element_type=F32) + bx_ref[...])
        log_a = -(r * decay_rate)
        a = jnp.exp(log_a)
        inp = jnp.sqrt(-jnp.tanh(log_a) * (a * a + 1.0)) * (i * xc)
        for j in range(LRU_TILES):
            slot = b * LRU_TILES + j
            a_ref[pl.ds(slot, T_SCAN, stride=LRU_PITCH), :] = a[:, j * LANES:(j + 1) * LANES]
            b_ref[pl.ds(slot, T_SCAN, stride=LRU_PITCH), :] = inp[:, j * LANES:(j + 1) * LANES]


def _lru_step(t, h, a_ref, b_ref):
    base = t * LRU_PITCH
    h = a_ref[pl.ds(base, LRU_SLOTS), :] * h + b_ref[pl.ds(base, LRU_SLOTS), :]
    b_ref[pl.ds(base, LRU_SLOTS), :] = h
    return h


def _lru_out(z_ref, o_ref, b_ref):
    for b in range(BATCH):
        h = jnp.concatenate(
            [b_ref[pl.ds(b * LRU_TILES + j, T_SCAN, stride=LRU_PITCH), :] for j in range(LRU_TILES)], axis=1)
        o_ref[b] = (h * jax.nn.gelu(z_ref[b, :, 2 * MIXER_WIDTH:3 * MIXER_WIDTH])).astype(MIX_OUT)


def _mixers_kernel(z_ref, zf_ref, v_ref, pv_ref,
                   bbd_ref, lam5_ref, cbd_ref, gw_ref, cw_ref, wa_ref, wx_ref,
                   tril_ref, pq_ref, pk_ref, oq_ref, ok_ref,
                   yb_ref, yc_ref, qadd_ref, kadd_ref, vt_ref,
                   s_ref, h5_ref, tail_ref, a_ref, b_ref, hl_ref, carry_ref):
    w = MIXER_WIDTH
    d_ref, gb_ref, cb_ref, ba_ref, bx_ref, lam_ref = [pv_ref.at[:, k * w:(k + 1) * w] for k in range(6)]
    bf_ref = pv_ref.at[:, 6 * w:6 * w + LANES]
    @pl.when(pl.program_id(0) == 0)
    def _():
        h5_ref[...] = jnp.zeros_like(h5_ref)
        tail_ref[...] = jnp.zeros_like(tail_ref)
        hl_ref[...] = jnp.zeros_like(hl_ref)
        carry_ref[...] = jnp.zeros_like(carry_ref)

    for b in range(BATCH):
        _s5_load(z_ref, bbd_ref, s_ref, (b,))
        _lru_load(z_ref, cw_ref, cb_ref, wa_ref, ba_ref, wx_ref, bx_ref, lam_ref, tail_ref, a_ref, b_ref, (b,))
        _fcum_chunk(zf_ref, bf_ref, v_ref, tril_ref, pq_ref, pk_ref, oq_ref, ok_ref, qadd_ref, kadd_ref, vt_ref,
                    carry_ref, (b,))

    lam_re = lam5_ref[0:SUBLANES, :]
    lam_im = lam5_ref[SUBLANES:S5_ROWS, :]

    def step(t, carry):
        new5 = _s5_step(t, carry[:-1], lam_re, lam_im, s_ref)
        return tuple(new5) + (_lru_step(t, carry[-1], a_ref, b_ref),)

    init = []
    for b in range(BATCH):
        init += [h5_ref[b, 0:SUBLANES, :], h5_ref[b, SUBLANES:S5_ROWS, :]]
    fin = lax.fori_loop(0, T_SCAN, step, tuple(init) + (hl_ref[...],), unroll=8)
    for b in range(BATCH):
        h5_ref[b, 0:SUBLANES, :] = fin[2 * b]
        h5_ref[b, SUBLANES:S5_ROWS, :] = fin[2 * b + 1]
    hl_ref[...] = fin[-1]

    _s5_out(z_ref, cbd_ref, d_ref, gw_ref, gb_ref, yb_ref, s_ref)
    _lru_out(z_ref, yc_ref, b_ref)


MIXER_VEC_COLS = 6 * MIXER_WIDTH + LANES


def _mixers(zmix3, zf3, qkv3, vecs, s5_params, lru_params, layer):
    bbd, lam5, cbd, gw = s5_params
    cw, wa, wx = lru_params
    pq, pk, ones_q, ones_k = _bias_placement()
    tril = jnp.asarray(np.tril(np.ones((T_SCAN, T_SCAN), np.float32)), BF16)
    full = lambda c: (0, 0)
    chunk = lambda c: (0, c, 0)
    wide = FOX_HEADS * HEAD_PAD
    mat = _layer_block((MIXER_WIDTH, MIXER_WIDTH), layer)
    mix_out = pl.BlockSpec((BATCH, T_SCAN, MIXER_WIDTH), chunk)
    bias_out = pl.BlockSpec((BATCH, T_SCAN, wide), chunk)
    return pl.pallas_call(
        _mixers_kernel,
        grid=(SEQ // T_SCAN,),
        in_specs=[pl.BlockSpec((BATCH, T_SCAN, ZMIX_COLS), chunk),
                  pl.BlockSpec((BATCH, T_SCAN, LANES), chunk),
                  pl.BlockSpec((BATCH, T_SCAN, wide), lambda c: (0, c, 2)),
                  _layer_block((1, MIXER_VEC_COLS), layer),
                  _layer_block((MIXER_WIDTH, 2 * S5_NSTATE), layer),
                  _layer_block((S5_ROWS, LANES), layer),
                  _layer_block((2 * S5_NSTATE, MIXER_WIDTH), layer),
                  mat,
                  _layer_block((LRU_CONV, MIXER_WIDTH), layer), mat, mat,
                  pl.BlockSpec((T_SCAN, T_SCAN), full),
                  pl.BlockSpec((LANES, wide), full),
                  pl.BlockSpec((LANES, wide), full),
                  pl.BlockSpec((1, wide), full),
                  pl.BlockSpec((1, wide), full)],
        out_specs=[mix_out, mix_out, bias_out, bias_out,
                   pl.BlockSpec((BATCH, wide, T_SCAN), lambda c: (0, 0, c))],
        out_shape=[jax.ShapeDtypeStruct((BATCH, SEQ, MIXER_WIDTH), MIX_OUT),
                   jax.ShapeDtypeStruct((BATCH, SEQ, MIXER_WIDTH), MIX_OUT),
                   jax.ShapeDtypeStruct((BATCH, SEQ, wide), BF16),
                   jax.ShapeDtypeStruct((BATCH, SEQ, wide), BF16),
                   jax.ShapeDtypeStruct((BATCH, wide, SEQ), BF16)],
        scratch_shapes=[pltpu.VMEM((BATCH, T_SCAN * S5_PITCH, LANES), F32),
                        pltpu.VMEM((BATCH, S5_ROWS, LANES), F32),
                        pltpu.VMEM((BATCH, SUBLANES, MIXER_WIDTH), F32),
                        pltpu.VMEM((T_SCAN * LRU_PITCH, LANES), F32),
                        pltpu.VMEM((T_SCAN * LRU_PITCH, LANES), F32),
                        pltpu.VMEM((LRU_SLOTS, LANES), F32),
                        pltpu.VMEM((BATCH, 1, LANES), F32)],
        compiler_params=_params(1, VMEM_LIMIT_MIXERS),
        name="recurrent_mixers",
    )(zmix3, zf3, qkv3, vecs, bbd, lam5, cbd, gw, cw, wa, wx, tril, pq, pk, ones_q, ones_k)


def _attn_kernel(q_ref, qadd_ref, k_ref, kadd_ref, vt_ref, o_ref, s_ref, m_ref, acc_ref):
    i = pl.program_id(2)
    scale = FOX_HEAD_DIM ** -0.5 * LOG2E
    neg = jnp.finfo(F32).min
    slots = [slice(n * HEAD_PAD, (n + 1) * HEAD_PAD) for n in range(FOX_HEADS)]
    qs = [(q_ref[:, sl].astype(F32) * scale + qadd_ref[:, sl].astype(F32)).astype(BF16) for sl in slots]

    def logits_t(n, j):
        start = pl.multiple_of(j * TK, TK)
        ks = k_ref[pl.ds(start, TK), slots[n]] + kadd_ref[pl.ds(start, TK), slots[n]]
        return lax.dot_general(ks, qs[n], (((1,), (1,)), ((), ())), preferred_element_type=F32)

    def block(j, masked, prefetch):
        start = pl.multiple_of(j * TK, TK)
        for n in range(FOX_HEADS):
            s = s_ref[n]
            if masked:
                key = lax.broadcasted_iota(jnp.int32, (TK, TQ), 0)
                qry = lax.broadcasted_iota(jnp.int32, (TK, TQ), 1)
                s = jnp.where(key <= qry, s, neg)
            m = m_ref[n]
            m_new = jnp.maximum(m, jnp.max(s, axis=0, keepdims=True))
            alpha = jnp.exp2(m - m_new)
            p = jnp.exp2(s - jnp.concatenate([m_new] * (TK // SUBLANES), axis=0))
            m_ref[n] = m_new
            vt = vt_ref[0, n * HEAD_PAD:n * HEAD_PAD + ACC_ROWS, pl.ds(start, TK)]
            acc_ref[n] = (jnp.concatenate([alpha] * (ACC_ROWS // SUBLANES), axis=0) * acc_ref[n]
                          + jnp.dot(vt, p.astype(BF16), preferred_element_type=F32))
            if prefetch:
                s_ref[n] = logits_t(n, j + 1)

    for n in range(FOX_HEADS):
        s_ref[n] = logits_t(n, 0)
    m_ref[...] = jnp.full(m_ref.shape, neg, F32)
    acc_ref[...] = jnp.zeros_like(acc_ref)

    @pl.loop(0, i // 2)
    def _(jj):
        block(2 * jj, False, True)
        block(2 * jj + 1, False, True)

    @pl.when(i % 2 == 1)
    def _():
        block(i - 1, False, True)

    block(i, True, False)
    for n, sl in enumerate(slots):
        acc = acc_ref[n]
        out_t = acc[0:FOX_HEAD_DIM] / acc[FOX_HEAD_DIM:FOX_HEAD_DIM + 1]
        out_t = jnp.concatenate([out_t, jnp.zeros((HEAD_PAD - FOX_HEAD_DIM, TQ), F32)], axis=0)
        o_ref[:, sl] = out_t.T.astype(MIX_OUT)


def _attn(qkv, qadd, kadd, vt):
    n_q = SEQ // TQ
    width = FOX_HEADS * HEAD_PAD
    qrow = lambda b, h, i: (b * n_q + i, 0)
    return pl.pallas_call(
        _attn_kernel,
        grid=(BATCH, 1, n_q),
        in_specs=[pl.BlockSpec((TQ, width), qrow),
                  pl.BlockSpec((TQ, width), qrow),
                  pl.BlockSpec((SEQ, width), lambda b, h, i: (b, 1)),
                  pl.BlockSpec((SEQ, width), lambda b, h, i: (b, 0)),
                  pl.BlockSpec((1, width, SEQ), lambda b, h, i: (b, 0, 0))],
        out_specs=pl.BlockSpec((TQ, width), qrow),
        out_shape=jax.ShapeDtypeStruct((N_TOK, width), MIX_OUT),
        scratch_shapes=[pltpu.VMEM((FOX_HEADS, TK, TQ), F32),
                        pltpu.VMEM((FOX_HEADS, SUBLANES, TQ), F32),
                        pltpu.VMEM((FOX_HEADS, ACC_ROWS, TQ), F32)],
        compiler_params=_params(3),
        name="fox_attention",
    )(qkv, qadd, qkv, kadd, vt)


def _merge_mlp_kernel(ya_ref, yb_ref, yc_ref, yd_ref, x_ref, pv_ref, wo_ref, w1_ref, w2_ref,
                      fg_ref, o_ref, h_ref, *, final_norm):
    j = pl.program_id(1)
    n_abc = 3 * MIXER_WIDTH
    n_d = FOX_HEADS * HEAD_PAD
    gm_ref = pv_ref.at[:, 0:n_abc]
    gmd_ref = pv_ref.at[:, n_abc:n_abc + n_d]
    g2_ref = pv_ref.at[:, n_abc + n_d:n_abc + n_d + D_MODEL]

    @pl.when(j == 0)
    def _():
        w = MIXER_WIDTH
        parts = [_rms(ya_ref[...].astype(F32), gm_ref[:, 0:w], w),
                 _rms(yb_ref[...].astype(F32), gm_ref[:, w:2 * w], w),
                 _rms(yc_ref[...].astype(F32), gm_ref[:, 2 * w:3 * w], w),
                 _rms(yd_ref[...].astype(F32), gmd_ref[...], w)]
        y = jnp.concatenate(parts, axis=1).astype(BF16)
        x1 = x_ref[...] + jnp.dot(y, wo_ref[...], preferred_element_type=F32)
        o_ref[...] = x1
        h_ref[...] = _rms(x1, g2_ref[...], D_MODEL).astype(BF16)

    a = jnp.dot(h_ref[...], w1_ref[...], preferred_element_type=F32)
    a = jnp.square(jnp.maximum(a, 0.0)).astype(BF16)
    o_ref[...] += jnp.dot(a, w2_ref[...], preferred_element_type=F32)

    if final_norm:
        @pl.when(j == pl.num_programs(1) - 1)
        def _():
            o_ref[...] = _rms(o_ref[...], fg_ref[...], D_MODEL)


def _merge_mlp(ya, yb, yc, yd, x, vecs, wo, w1, w2, fg, layer, final_norm):
    row = lambda i, j: (i, 0)
    k_dim = 3 * MIXER_WIDTH + FOX_HEADS * HEAD_PAD
    mix = pl.BlockSpec((TM_MLP, MIXER_WIDTH), row)
    return pl.pallas_call(
        functools.partial(_merge_mlp_kernel, final_norm=final_norm),
        grid=(N_TOK // TM_MLP, D_FF // TF_MLP),
        in_specs=[mix, mix, mix,
                  pl.BlockSpec((TM_MLP, FOX_HEADS * HEAD_PAD), row),
                  pl.BlockSpec((TM_MLP, D_MODEL), row),
                  _layer_block((1, k_dim + D_MODEL), layer),
                  _layer_block((k_dim, D_MODEL), layer),
                  pl.BlockSpec((None, D_MODEL, TF_MLP), lambda i, j: (layer, 0, j)),
                  pl.BlockSpec((None, TF_MLP, D_MODEL), lambda i, j: (layer, j, 0)),
                  pl.BlockSpec((1, D_MODEL), lambda i, j: (0, 0))],
        out_specs=pl.BlockSpec((TM_MLP, D_MODEL), row),
        out_shape=jax.ShapeDtypeStruct((N_TOK, D_MODEL), F32),
        scratch_shapes=[pltpu.VMEM((TM_MLP, D_MODEL), BF16)],
        compiler_params=_params(2),
        name="merge_mlp",
    )(ya, yb, yc, yd, x, vecs, wo, w1, w2, fg)


def _pad_heads(w):
    lead = w.shape[:-1]
    w = w.reshape(*lead, FOX_HEADS, FOX_HEAD_DIM)
    w = jnp.pad(w, [(0, 0)] * (len(lead) + 1) + [(0, HEAD_PAD - FOX_HEAD_DIM)])
    return w.reshape(*lead, FOX_HEADS * HEAD_PAD)


def _block_diag(blocks):
    n, g, r, c = blocks.shape
    eye = jnp.eye(g, dtype=blocks.dtype)
    return jnp.einsum('ngrc,gh->ngrhc', blocks, eye).reshape(n, g * r, g * c)


def _s5_discretize(lam_re, lam_im, log_dt, b_re, b_im):
    dt = jnp.exp(log_dt)[..., None]
    mag = jnp.exp(lam_re * dt)
    abar_re = mag * jnp.cos(lam_im * dt)
    abar_im = mag * jnp.sin(lam_im * dt)
    denom = jnp.square(lam_re) + jnp.square(lam_im)
    num_re = abar_re - 1.0
    num_im = abar_im
    fac_re = (num_re * lam_re + num_im * lam_im) / denom
    fac_im = (num_im * lam_re - num_re * lam_im) / denom
    bbar_re = fac_re[..., None] * b_re - fac_im[..., None] * b_im
    bbar_im = fac_re[..., None] * b_im + fac_im[..., None] * b_re
    return abar_re, abar_im, bbar_re, bbar_im


def kernel(x, norm1_g, w_in, sgu_norm_g, sgu_w, sgu_b, s5_lambda_re, s5_lambda_im, s5_log_dt, s5_b_re, s5_b_im, s5_c_re, s5_c_im, s5_d, s5_glu_w, s5_glu_b, lru_conv_w, lru_conv_b, lru_wa, lru_ba, lru_wx, lru_bx, lru_lambda, fox_fgate_b, mix_norm_g, w_out, norm2_g, w_mlp_in, w_mlp_out, final_g):
    w = MIXER_WIDTH
    row = lambda v: v.reshape(DEPTH, 1, -1)

    w_in_main = w_in[..., 0:8 * w].astype(BF16)
    w_in_f = jnp.pad(w_in[..., 8 * w:], ((0, 0), (0, 0), (0, LANES - FOX_HEADS))).astype(BF16)
    vec_in = row(jnp.concatenate([norm1_g, sgu_norm_g], axis=-1))
    vec_mix = row(jnp.concatenate(
        [s5_d, s5_glu_b, lru_conv_b, lru_ba.reshape(DEPTH, w), lru_bx.reshape(DEPTH, w), lru_lambda,
         jnp.pad(fox_fgate_b, ((0, 0), (0, LANES - FOX_HEADS)))], axis=-1))
    vec_mlp = row(jnp.concatenate(
        [mix_norm_g[:, 0:3 * w], _pad_heads(mix_norm_g[:, 3 * w:]), norm2_g], axis=-1))

    sgu_wcat = jnp.transpose(sgu_w, (0, 2, 1, 3)).reshape(DEPTH, SGU_CHUNK, SGU_HEADS * SGU_CHUNK)
    sgu_bias = jnp.repeat(jnp.transpose(sgu_b, (0, 2, 1)), w // SGU_HEADS, axis=2)

    abar_re, abar_im, bbar_re, bbar_im = _s5_discretize(s5_lambda_re, s5_lambda_im, s5_log_dt, s5_b_re, s5_b_im)
    swap = lambda t: jnp.transpose(t, (0, 1, 3, 2))
    bbd = jnp.concatenate([_block_diag(swap(bbar_re)), _block_diag(swap(bbar_im))], axis=2).astype(BF16)
    cbd = jnp.concatenate([_block_diag(swap(s5_c_re)), -_block_diag(swap(s5_c_im))], axis=1).astype(BF16)
    lam = jnp.concatenate([abar_re.reshape(DEPTH, SUBLANES, LANES), abar_im.reshape(DEPTH, SUBLANES, LANES)], axis=1)
    glu_w = s5_glu_w.astype(BF16)

    wa_bd = _block_diag(lru_wa).astype(BF16)
    wx_bd = _block_diag(lru_wx).astype(BF16)

    w_o_pad = jnp.concatenate(
        [w_out[:, 0:3 * w],
         jnp.pad(w_out[:, 3 * w:].reshape(DEPTH, FOX_HEADS, FOX_HEAD_DIM, D_MODEL),
                 ((0, 0), (0, 0), (0, HEAD_PAD - FOX_HEAD_DIM), (0, 0))).reshape(DEPTH, FOX_HEADS * HEAD_PAD, D_MODEL)],
        axis=1).astype(BF16)
    w1 = w_mlp_in.astype(BF16)
    w2 = w_mlp_out.astype(BF16)
    fg = final_g.reshape(1, D_MODEL)

    xf = x.reshape(N_TOK, D_MODEL)
    for l in range(DEPTH):
        y_a, zmix, qkv, zf = _in_proj(xf, vec_in, w_in_main, w_in_f, sgu_wcat, sgu_bias, l)
        zmix3 = zmix.reshape(BATCH, SEQ, ZMIX_COLS)
        y_b, y_c, qadd, kadd, vt = _mixers(
            zmix3, zf.reshape(BATCH, SEQ, LANES), qkv.reshape(BATCH, SEQ, QKV_COLS), vec_mix,
            (bbd, lam, cbd, glu_w), (lru_conv_w, wa_bd, wx_bd), l)
        qadd = qadd.reshape(N_TOK, FOX_HEADS * HEAD_PAD)
        kadd = kadd.reshape(N_TOK, FOX_HEADS * HEAD_PAD)
        y_d = _attn(qkv, qadd, kadd, vt)
        xf = _merge_mlp(y_a, y_b.reshape(N_TOK, w), y_c.reshape(N_TOK, w), y_d, xf, vec_mlp, w_o_pad,
                        w1, w2, fg, l, final_norm=(l == DEPTH - 1))
    return xf.reshape(BATCH, SEQ, D_MODEL)
```

```python
import functools

import jax
import jax.numpy as jnp
import numpy as np
from jax import lax
from jax.experimental import pallas as pl
from jax.experimental.pallas import tpu as pltpu

D_MODEL = 1024
BATCH = 4
SEQ = 4096
DEPTH = 4
N_TOK = BATCH * SEQ
MIXER_WIDTH = 256
SGU_HEADS = 4
SGU_CHUNK = 128
S5_GROUP = 16
S5_GROUPS = 16
S5_STATE = 64
LRU_HEADS = 4
LRU_CONV = 4
LRU_C = 8.0
FOX_HEADS = 4
FOX_HEAD_DIM = 64
D_FF = 4 * D_MODEL
RMS_EPS = 1e-6
LOG2E = 1.4426950408889634

LANES = 128
SUBLANES = 8
HEAD_PAD = LANES
ZMIX_COLS = 3 * MIXER_WIDTH
QKV_COLS = 3 * FOX_HEADS * HEAD_PAD
S5_NSTATE = S5_GROUPS * S5_STATE
S5_ROWS = 2 * S5_NSTATE // LANES
SCAN_ROW_GAP = 4
S5_PITCH = S5_ROWS + SCAN_ROW_GAP

TM_IN = 1024
TM_MLP = 1024
TF_MLP = 1024
T_SCAN = 256
TQ = 512
TK = 512
ACC_ROWS = 80
VMEM_LIMIT = 48 * 1024 * 1024
VMEM_LIMIT_MIXERS = 56 * 1024 * 1024

F32 = jnp.float32
BF16 = jnp.bfloat16
MIX_OUT = BF16


def _params(n_axes, vmem_limit=VMEM_LIMIT):
    return pltpu.CompilerParams(dimension_semantics=("arbitrary",) * n_axes,
                                vmem_limit_bytes=vmem_limit)


def _layer_block(shape, layer):
    zeros = (0,) * len(shape)
    return pl.BlockSpec((None,) + tuple(shape), lambda *_: (layer,) + zeros)


def _rms(x, g, width):
    ms = jnp.sum(jnp.square(x), axis=-1, keepdims=True) * (1.0 / width)
    return x * lax.rsqrt(ms + RMS_EPS) * g


def _softplus(x):
    return jnp.maximum(x, 0.0) + jnp.log1p(jnp.exp(-jnp.abs(x)))


def _sgu_mix(zu, zv, g, w, bias, o_ref):
    n_rows = zu.shape[0]
    u = jax.nn.gelu(zu)
    v = _rms(jax.nn.gelu(zv), g, MIXER_WIDTH)
    hd = MIXER_WIDTH // SGU_HEADS
    lane_head = lax.broadcasted_iota(jnp.int32, (SGU_CHUNK, MIXER_WIDTH), 1) // hd
    t_idx = lax.broadcasted_iota(jnp.int32, (SGU_CHUNK, SGU_HEADS * SGU_CHUNK), 0)
    s_idx = lax.broadcasted_iota(jnp.int32, (SGU_CHUNK, SGU_HEADS * SGU_CHUNK), 1) % SGU_CHUNK
    wm = jnp.where(s_idx <= t_idx, w, 0.0).astype(BF16)
    for c in range(n_rows // SGU_CHUNK):
        rows = slice(c * SGU_CHUNK, (c + 1) * SGU_CHUNK)
        vc = v[rows]
        vstack = jnp.concatenate(
            [jnp.where(lane_head == h, vc, 0.0) for h in range(SGU_HEADS)], axis=0).astype(BF16)
        mixed = jnp.dot(wm, vstack, preferred_element_type=F32) + bias
        o_ref[rows, :] = (u[rows] * mixed).astype(MIX_OUT)


def _in_proj_kernel(x_ref, pv_ref, w_ref, wf_ref, sw_ref, sb_ref, ya_ref, zmix_ref, qkv_ref, zf_ref):
    w = MIXER_WIDTH
    h = _rms(x_ref[...], pv_ref[:, 0:D_MODEL], D_MODEL).astype(BF16)
    z5 = jnp.dot(h, w_ref[:, 0:5 * w], preferred_element_type=F32)
    zmix_ref[...] = z5[:, 2 * w:5 * w]
    _sgu_mix(z5[:, 0:w], z5[:, w:2 * w], pv_ref[:, D_MODEL:D_MODEL + w], sw_ref[...], sb_ref[...], ya_ref)
    w_qkvf = jnp.concatenate([w_ref[:, 5 * w:8 * w], wf_ref[...]], axis=1)
    zqkvf = jnp.dot(h, w_qkvf, preferred_element_type=F32)
    zf_ref[...] = zqkvf[:, 3 * w:]
    pad = jnp.zeros((TM_IN, HEAD_PAD - FOX_HEAD_DIM), F32)
    for slot in range(3 * FOX_HEADS):
        head = zqkvf[:, slot * FOX_HEAD_DIM:(slot + 1) * FOX_HEAD_DIM]
        qkv_ref[:, slot * HEAD_PAD:(slot + 1) * HEAD_PAD] = jnp.concatenate([head, pad], axis=1).astype(BF16)


def _in_proj(x, vecs, w, wf, sgu_w, sgu_b, layer):
    row = lambda i: (i, 0)
    return pl.pallas_call(
        _in_proj_kernel,
        grid=(N_TOK // TM_IN,),
        in_specs=[pl.BlockSpec((TM_IN, D_MODEL), row),
                  _layer_block((1, D_MODEL + MIXER_WIDTH), layer),
                  _layer_block((D_MODEL, 8 * MIXER_WIDTH), layer),
                  _layer_block((D_MODEL, LANES), layer),
                  _layer_block((SGU_CHUNK, SGU_HEADS * SGU_CHUNK), layer),
                  _layer_block((SGU_CHUNK, MIXER_WIDTH), layer)],
        out_specs=[pl.BlockSpec((TM_IN, MIXER_WIDTH), row),
                   pl.BlockSpec((TM_IN, ZMIX_COLS), row),
                   pl.BlockSpec((TM_IN, QKV_COLS), row),
                   pl.BlockSpec((TM_IN, LANES), row)],
        out_shape=[jax.ShapeDtypeStruct((N_TOK, MIXER_WIDTH), MIX_OUT),
                   jax.ShapeDtypeStruct((N_TOK, ZMIX_COLS), F32),
                   jax.ShapeDtypeStruct((N_TOK, QKV_COLS), BF16),
                   jax.ShapeDtypeStruct((N_TOK, LANES), F32)],
        compiler_params=_params(1),
        name="in_proj_sgu",
    )(x, vecs, w, wf, sgu_w, sgu_b)


def _split3(x):
    hi = x.astype(BF16)
    r1 = x - hi.astype(F32)
    mid = r1.astype(BF16)
    lo = (r1 - mid.astype(F32)).astype(BF16)
    return hi, mid, lo


def _bias_placement():
    pq = np.zeros((LANES, FOX_HEADS * HEAD_PAD), np.float32)
    pk = np.zeros_like(pq)
    ones_q = np.zeros((1, FOX_HEADS * HEAD_PAD), np.float32)
    ones_k = np.zeros_like(ones_q)
    for h in range(FOX_HEADS):
        for piece in range(3):
            pq[piece * FOX_HEADS + h, h * HEAD_PAD + FOX_HEAD_DIM + piece] = 1.0
            pk[piece * FOX_HEADS + h, h * HEAD_PAD + FOX_HEAD_DIM + 3 + piece] = -1.0
            ones_q[0, h * HEAD_PAD + FOX_HEAD_DIM + 3 + piece] = 1.0
            ones_k[0, h * HEAD_PAD + FOX_HEAD_DIM + piece] = 1.0
    return (jnp.asarray(pq, BF16), jnp.asarray(pk, BF16), jnp.asarray(ones_q), jnp.asarray(ones_k))


def _fcum_chunk(zf_ref, bf_ref, v_ref, tril_ref, pq_ref, pk_ref, oq_ref, ok_ref, qadd_ref, kadd_ref, vt_ref,
                carry_ref, batches):
    tril = tril_ref[...]
    lane = lax.broadcasted_iota(jnp.int32, (T_SCAN, LANES), 1)
    ones_row = lax.broadcasted_iota(jnp.int32, (HEAD_PAD, T_SCAN), 0) == FOX_HEAD_DIM
    for b in batches:
        for n in range(FOX_HEADS):
            v_t = v_ref[b, :, n * HEAD_PAD:(n + 1) * HEAD_PAD].astype(F32).T
            vt_ref[b, n * HEAD_PAD:(n + 1) * HEAD_PAD, :] = jnp.where(ones_row, 1.0, v_t).astype(BF16)
        logit = zf_ref[b] + bf_ref[...]
        log_f = -_softplus(-logit)
        hi, mid, lo = _split3(log_f)
        cs = (jnp.dot(tril, hi, preferred_element_type=F32)
              + jnp.dot(tril, mid, preferred_element_type=F32)
              + jnp.dot(tril, lo, preferred_element_type=F32))
        cum = cs + carry_ref[b]
        carry_ref[b] = cum[T_SCAN - 1:T_SCAN, :]
        hi, mid, lo = [p.astype(F32) for p in _split3(cum * LOG2E)]
        pieces = jnp.where(lane < FOX_HEADS, hi,
                           jnp.where(lane < 2 * FOX_HEADS, pltpu.roll(mid, FOX_HEADS, axis=1),
                                     jnp.where(lane < 3 * FOX_HEADS, pltpu.roll(lo, 2 * FOX_HEADS, axis=1), 0.0))
                           ).astype(BF16)
        qadd_ref[b] = (jnp.dot(pieces, pq_ref[...], preferred_element_type=F32) + oq_ref[...]).astype(BF16)
        kadd_ref[b] = (jnp.dot(pieces, pk_ref[...], preferred_element_type=F32) + ok_ref[...]).astype(BF16)


def _s5_load(z_ref, bbd_ref, s_ref, batches):
    for b in batches:
        u = z_ref[b, :, 0:MIXER_WIDTH]
        bu = jnp.dot(u.astype(BF16), bbd_ref[...], preferred_element_type=F32)
        for j in range(S5_ROWS):
            s_ref[b, pl.ds(j, T_SCAN, stride=S5_PITCH), :] = bu[:, j * LANES:(j + 1) * LANES]


def _s5_step(t, carry, lam_re, lam_im, s_ref):
    base = t * S5_PITCH
    new = []
    for b in range(BATCH):
        h_re, h_im = carry[2 * b], carry[2 * b + 1]
        n_re = lam_re * h_re - lam_im * h_im + s_ref[b, pl.ds(base, SUBLANES), :]
        n_im = lam_re * h_im + lam_im * h_re + s_ref[b, pl.ds(base + SUBLANES, SUBLANES), :]
        s_ref[b, pl.ds(base, SUBLANES), :] = n_re
        s_ref[b, pl.ds(base + SUBLANES, SUBLANES), :] = n_im
        new += [n_re, n_im]
    return new


def _s5_out(z_ref, cbd_ref, d_ref, gw_ref, gb_ref, o_ref, s_ref):
    states = jnp.concatenate(
        [jnp.concatenate([s_ref[b, pl.ds(j, T_SCAN, stride=S5_PITCH), :].astype(BF16) for j in range(S5_ROWS)],
                         axis=1) for b in range(BATCH)], axis=0)
    u = z_ref[:, :, 0:MIXER_WIDTH].reshape(BATCH * T_SCAN, MIXER_WIDTH)
    y = jnp.dot(states, cbd_ref[...], preferred_element_type=F32) + d_ref[...] * u
    y = jax.nn.gelu(y)
    gate = jnp.dot(y.astype(BF16), gw_ref[...], preferred_element_type=F32) + gb_ref[...]
    o_ref[...] = (y * jax.nn.sigmoid(gate)).astype(MIX_OUT).reshape(BATCH, T_SCAN, MIXER_WIDTH)


LRU_SLOTS = BATCH * MIXER_WIDTH // LANES
LRU_PITCH = LRU_SLOTS + SCAN_ROW_GAP
LRU_TILES = MIXER_WIDTH // LANES


def _lru_load(z_ref, cw_ref, cb_ref, wa_ref, ba_ref, wx_ref, bx_ref, lam_ref, tail_ref, a_ref, b_ref, batches):
    decay_rate = LRU_C * _softplus(-lam_ref[...])
    for b in batches:
        x = z_ref[b, :, MIXER_WIDTH:2 * MIXER_WIDTH]
        xp = jnp.concatenate([tail_ref[b], x], axis=0)
        tail_ref[b] = x[T_SCAN - SUBLANES:T_SCAN, :]
        xc = cb_ref[...]
        for k in range(LRU_CONV):
            off = SUBLANES - (LRU_CONV - 1) + k
            xc = xc + cw_ref[k:k + 1, :] * xp[off:off + T_SCAN, :]
        xcb = xc.astype(BF16)
        r = jax.nn.sigmoid(jnp.dot(xcb, wa_ref[...], preferred_element_type=F32) + ba_ref[...])
        i = jax.nn.sigmoid(jnp.dot(xcb, wx_ref[...], preferred_element_type=F32) + bx_ref[...])
        log_a = -(r * decay_rate)
        a = jnp.exp(log_a)
        inp = jnp.sqrt(-jnp.tanh(log_a) * (a * a + 1.0)) * (i * xc)
        for j in range(LRU_TILES):
            slot = b * LRU_TILES + j
            a_ref[pl.ds(slot, T_SCAN, stride=LRU_PITCH), :] = a[:, j * LANES:(j + 1) * LANES]
            b_ref[pl.ds(slot, T_SCAN, stride=LRU_PITCH), :] = inp[:, j * LANES:(j + 1) * LANES]


def _lru_step(t, h, a_ref, b_ref):
    base = t * LRU_PITCH
    h = a_ref[pl.ds(base, LRU_SLOTS), :] * h + b_ref[pl.ds(base, LRU_SLOTS), :]
    b_ref[pl.ds(base, LRU_SLOTS), :] = h
    return h


def _lru_out(z_ref, o_ref, b_ref):
    for b in range(BATCH):
        h = jnp.concatenate(
            [b_ref[pl.ds(b * LRU_TILES + j, T_SCAN, stride=LRU_PITCH), :] for j in range(LRU_TILES)], axis=1)
        o_ref[b] = (h * jax.nn.gelu(z_ref[b, :, 2 * MIXER_WIDTH:3 * MIXER_WIDTH])).astype(MIX_OUT)


def _mixers_kernel(z_ref, zf_ref, v_ref, pv_ref,
                   bbd_ref, lam5_ref, cbd_ref, gw_ref, cw_ref, wa_ref, wx_ref,
                   tril_ref, pq_ref, pk_ref, oq_ref, ok_ref,
                   yb_ref, yc_ref, qadd_ref, kadd_ref, vt_ref,
                   s_ref, h5_ref, tail_ref, a_ref, b_ref, hl_ref, carry_ref):
    w = MIXER_WIDTH
    d_ref, gb_ref, cb_ref, ba_ref, bx_ref, lam_ref = [pv_ref.at[:, k * w:(k + 1) * w] for k in range(6)]
    bf_ref = pv_ref.at[:, 6 * w:6 * w + LANES]
    @pl.when(pl.program_id(0) == 0)
    def _():
        h5_ref[...] = jnp.zeros_like(h5_ref)
        tail_ref[...] = jnp.zeros_like(tail_ref)
        hl_ref[...] = jnp.zeros_like(hl_ref)
        carry_ref[...] = jnp.zeros_like(carry_ref)

    for b in range(BATCH):
        _s5_load(z_ref, bbd_ref, s_ref, (b,))
        _lru_load(z_ref, cw_ref, cb_ref, wa_ref, ba_ref, wx_ref, bx_ref, lam_ref, tail_ref, a_ref, b_ref, (b,))
        _fcum_chunk(zf_ref, bf_ref, v_ref, tril_ref, pq_ref, pk_ref, oq_ref, ok_ref, qadd_ref, kadd_ref, vt_ref,
                    carry_ref, (b,))

    lam_re = lam5_ref[0:SUBLANES, :]
    lam_im = lam5_ref[SUBLANES:S5_ROWS, :]

    def step(t, carry):
        new5 = _s5_step(t, carry[:-1], lam_re, lam_im, s_ref)
        return tuple(new5) + (_lru_step(t, carry[-1], a_ref, b_ref),)

    init = []
    for b in range(BATCH):
        init += [h5_ref[b, 0:SUBLANES, :], h5_ref[b, SUBLANES:S5_ROWS, :]]
    fin = lax.fori_loop(0, T_SCAN, step, tuple(init) + (hl_ref[...],), unroll=8)
    for b in range(BATCH):
        h5_ref[b, 0:SUBLANES, :] = fin[2 * b]
        h5_ref[b, SUBLANES:S5_ROWS, :] = fin[2 * b + 1]
    hl_ref[...] = fin[-1]

    _s5_out(z_ref, cbd_ref, d_ref, gw_ref, gb_ref, yb_ref, s_ref)
    _lru_out(z_ref, yc_ref, b_ref)


MIXER_VEC_COLS = 6 * MIXER_WIDTH + LANES


def _mixers(zmix3, zf3, qkv3, vecs, s5_params, lru_params, layer):
    bbd, lam5, cbd, gw = s5_params
    cw, wa, wx = lru_params
    pq, pk, ones_q, ones_k = _bias_placement()
    tril = jnp.asarray(np.tril(np.ones((T_SCAN, T_SCAN), np.float32)), BF16)
    full = lambda c: (0, 0)
    chunk = lambda c: (0, c, 0)
    wide = FOX_HEADS * HEAD_PAD
    mat = _layer_block((MIXER_WIDTH, MIXER_WIDTH), layer)
    mix_out = pl.BlockSpec((BATCH, T_SCAN, MIXER_WIDTH), chunk)
    bias_out = pl.BlockSpec((BATCH, T_SCAN, wide), chunk)
    return pl.pallas_call(
        _mixers_kernel,
        grid=(SEQ // T_SCAN,),
        in_specs=[pl.BlockSpec((BATCH, T_SCAN, ZMIX_COLS), chunk),
                  pl.BlockSpec((BATCH, T_SCAN, LANES), chunk),
                  pl.BlockSpec((BATCH, T_SCAN, wide), lambda c: (0, c, 2)),
                  _layer_block((1, MIXER_VEC_COLS), layer),
                  _layer_block((MIXER_WIDTH, 2 * S5_NSTATE), layer),
                  _layer_block((S5_ROWS, LANES), layer),
                  _layer_block((2 * S5_NSTATE, MIXER_WIDTH), layer),
                  mat,
                  _layer_block((LRU_CONV, MIXER_WIDTH), layer), mat, mat,
                  pl.BlockSpec((T_SCAN, T_SCAN), full),
                  pl.BlockSpec((LANES, wide), full),
                  pl.BlockSpec((LANES, wide), full),
                  pl.BlockSpec((1, wide), full),
                  pl.BlockSpec((1, wide), full)],
        out_specs=[mix_out, mix_out, bias_out, bias_out,
                   pl.BlockSpec((BATCH, wide, T_SCAN), lambda c: (0, 0, c))],
        out_shape=[jax.ShapeDtypeStruct((BATCH, SEQ, MIXER_WIDTH), MIX_OUT),
                   jax.ShapeDtypeStruct((BATCH, SEQ, MIXER_WIDTH), MIX_OUT),
                   jax.ShapeDtypeStruct((BATCH, SEQ, wide), BF16),
                   jax.ShapeDtypeStruct((BATCH, SEQ, wide), BF16),
                   jax.ShapeDtypeStruct((BATCH, wide, SEQ), BF16)],
        scratch_shapes=[pltpu.VMEM((BATCH, T_SCAN * S5_PITCH, LANES), F32),
                        pltpu.VMEM((BATCH, S5_ROWS, LANES), F32),
                        pltpu.VMEM((BATCH, SUBLANES, MIXER_WIDTH), F32),
                        pltpu.VMEM((T_SCAN * LRU_PITCH, LANES), F32),
                        pltpu.VMEM((T_SCAN * LRU_PITCH, LANES), F32),
                        pltpu.VMEM((LRU_SLOTS, LANES), F32),
                        pltpu.VMEM((BATCH, 1, LANES), F32)],
        compiler_params=_params(1, VMEM_LIMIT_MIXERS),
        name="recurrent_mixers",
    )(zmix3, zf3, qkv3, vecs, bbd, lam5, cbd, gw, cw, wa, wx, tril, pq, pk, ones_q, ones_k)


def _attn_kernel(q_ref, qadd_ref, k_ref, kadd_ref, vt_ref, o_ref, s_ref, m_ref, acc_ref):
    i = pl.program_id(2)
    scale = FOX_HEAD_DIM ** -0.5 * LOG2E
    neg = jnp.finfo(F32).min
    slots = [slice(n * HEAD_PAD, (n + 1) * HEAD_PAD) for n in range(FOX_HEADS)]
    qs = [(q_ref[:, sl].astype(F32) * scale + qadd_ref[:, sl].astype(F32)).astype(BF16) for sl in slots]

    def logits_t(n, j):
        start = pl.multiple_of(j * TK, TK)
        ks = k_ref[pl.ds(start, TK), slots[n]] + kadd_ref[pl.ds(start, TK), slots[n]]
        return lax.dot_general(ks, qs[n], (((1,), (1,)), ((), ())), preferred_element_type=F32)

    def block(j, masked, prefetch):
        start = pl.multiple_of(j * TK, TK)
        for n in range(FOX_HEADS):
            s = s_ref[n]
            if masked:
                key = lax.broadcasted_iota(jnp.int32, (TK, TQ), 0)
                qry = lax.broadcasted_iota(jnp.int32, (TK, TQ), 1)
                s = jnp.where(key <= qry, s, neg)
            m = m_ref[n]
            m_new = jnp.maximum(m, jnp.max(s, axis=0, keepdims=True))
            alpha = jnp.exp2(m - m_new)
            p = jnp.exp2(s - jnp.concatenate([m_new] * (TK // SUBLANES), axis=0))
            m_ref[n] = m_new
            vt = vt_ref[0, n * HEAD_PAD:n * HEAD_PAD + ACC_ROWS, pl.ds(start, TK)]
            acc_ref[n] = (jnp.concatenate([alpha] * (ACC_ROWS // SUBLANES), axis=0) * acc_ref[n]
                          + jnp.dot(vt, p.astype(BF16), preferred_element_type=F32))
            if prefetch:
                s_ref[n] = logits_t(n, j + 1)

    for n in range(FOX_HEADS):
        s_ref[n] = logits_t(n, 0)
    m_ref[...] = jnp.full(m_ref.shape, neg, F32)
    acc_ref[...] = jnp.zeros_like(acc_ref)

    @pl.loop(0, i // 2)
    def _(jj):
        block(2 * jj, False, True)
        block(2 * jj + 1, False, True)

    @pl.when(i % 2 == 1)
    def _():
        block(i - 1, False, True)

    block(i, True, False)
    for n, sl in enumerate(slots):
        acc = acc_ref[n]
        out_t = acc[0:FOX_HEAD_DIM] / acc[FOX_HEAD_DIM:FOX_HEAD_DIM + 1]
        out_t = jnp.concatenate([out_t, jnp.zeros((HEAD_PAD - FOX_HEAD_DIM, TQ), F32)], axis=0)
        o_ref[:, sl] = out_t.T.astype(MIX_OUT)


def _attn(qkv, qadd, kadd, vt):
    n_q = SEQ // TQ
    width = FOX_HEADS * HEAD_PAD
    qrow = lambda b, h, i: (b * n_q + i, 0)
    return pl.pallas_call(
        _attn_kernel,
        grid=(BATCH, 1, n_q),
        in_specs=[pl.BlockSpec((TQ, width), qrow),
                  pl.BlockSpec((TQ, width), qrow),
                  pl.BlockSpec((SEQ, width), lambda b, h, i: (b, 1)),
                  pl.BlockSpec((SEQ, width), lambda b, h, i: (b, 0)),
                  pl.BlockSpec((1, width, SEQ), lambda b, h, i: (b, 0, 0))],
        out_specs=pl.BlockSpec((TQ, width), qrow),
        out_shape=jax.ShapeDtypeStruct((N_TOK, width), MIX_OUT),
        scratch_shapes=[pltpu.VMEM((FOX_HEADS, TK, TQ), F32),
                        pltpu.VMEM((FOX_HEADS, SUBLANES, TQ), F32),
                        pltpu.VMEM((FOX_HEADS, ACC_ROWS, TQ), F32)],
        compiler_params=_params(3),
        name="fox_attention",
    )(qkv, qadd, qkv, kadd, vt)


def _merge_mlp_kernel(ya_ref, yb_ref, yc_ref, yd_ref, x_ref, pv_ref, wo_ref, w1_ref, w2_ref,
                      fg_ref, o_ref, h_ref, *, final_norm):
    j = pl.program_id(1)
    n_abc = 3 * MIXER_WIDTH
    n_d = FOX_HEADS * HEAD_PAD
    gm_ref = pv_ref.at[:, 0:n_abc]
    gmd_ref = pv_ref.at[:, n_abc:n_abc + n_d]
    g2_ref = pv_ref.at[:, n_abc + n_d:n_abc + n_d + D_MODEL]

    @pl.when(j == 0)
    def _():
        w = MIXER_WIDTH
        parts = [_rms(ya_ref[...].astype(F32), gm_ref[:, 0:w], w),
                 _rms(yb_ref[...].astype(F32), gm_ref[:, w:2 * w], w),
                 _rms(yc_ref[...].astype(F32), gm_ref[:, 2 * w:3 * w], w),
                 _rms(yd_ref[...].astype(F32), gmd_ref[...], w)]
        y = jnp.concatenate(parts, axis=1).astype(BF16)
        x1 = x_ref[...] + jnp.dot(y, wo_ref[...], preferred_element_type=F32)
        o_ref[...] = x1
        h_ref[...] = _rms(x1, g2_ref[...], D_MODEL).astype(BF16)

    a = jnp.dot(h_ref[...], w1_ref[...], preferred_element_type=F32)
    a = jnp.square(jnp.maximum(a, 0.0)).astype(BF16)
    o_ref[...] += jnp.dot(a, w2_ref[...], preferred_element_type=F32)

    if final_norm:
        @pl.when(j == pl.num_programs(1) - 1)
        def _():
            o_ref[...] = _rms(o_ref[...], fg_ref[...], D_MODEL)


def _merge_mlp(ya, yb, yc, yd, x, vecs, wo, w1, w2, fg, layer, final_norm):
    row = lambda i, j: (i, 0)
    k_dim = 3 * MIXER_WIDTH + FOX_HEADS * HEAD_PAD
    mix = pl.BlockSpec((TM_MLP, MIXER_WIDTH), row)
    return pl.pallas_call(
        functools.partial(_merge_mlp_kernel, final_norm=final_norm),
        grid=(N_TOK // TM_MLP, D_FF // TF_MLP),
        in_specs=[mix, mix, mix,
                  pl.BlockSpec((TM_MLP, FOX_HEADS * HEAD_PAD), row),
                  pl.BlockSpec((TM_MLP, D_MODEL), row),
                  _layer_block((1, k_dim + D_MODEL), layer),
                  _layer_block((k_dim, D_MODEL), layer),
                  pl.BlockSpec((None, D_MODEL, TF_MLP), lambda i, j: (layer, 0, j)),
                  pl.BlockSpec((None, TF_MLP, D_MODEL), lambda i, j: (layer, j, 0)),
                  pl.BlockSpec((1, D_MODEL), lambda i, j: (0, 0))],
        out_specs=pl.BlockSpec((TM_MLP, D_MODEL), row),
        out_shape=jax.ShapeDtypeStruct((N_TOK, D_MODEL), F32),
        scratch_shapes=[pltpu.VMEM((TM_MLP, D_MODEL), BF16)],
        compiler_params=_params(2),
        name="merge_mlp",
    )(ya, yb, yc, yd, x, vecs, wo, w1, w2, fg)


def _pad_heads(w):
    lead = w.shape[:-1]
    w = w.reshape(*lead, FOX_HEADS, FOX_HEAD_DIM)
    w = jnp.pad(w, [(0, 0)] * (len(lead) + 1) + [(0, HEAD_PAD - FOX_HEAD_DIM)])
    return w.reshape(*lead, FOX_HEADS * HEAD_PAD)


def _block_diag(blocks):
    n, g, r, c = blocks.shape
    tiled = jnp.tile(blocks.reshape(n, g * r, c), (1, 1, g))
    on_diag = (np.arange(g * r)[:, None] // r) == (np.arange(g * c)[None, :] // c)
    return jnp.where(jnp.asarray(on_diag), tiled, 0.0)


def _s5_discretize(lam_re, lam_im, log_dt, b_re, b_im):
    dt = jnp.exp(log_dt)[..., None]
    mag = jnp.exp(lam_re * dt)
    abar_re = mag * jnp.cos(lam_im * dt)
    abar_im = mag * jnp.sin(lam_im * dt)
    denom = jnp.square(lam_re) + jnp.square(lam_im)
    num_re = abar_re - 1.0
    num_im = abar_im
    fac_re = (num_re * lam_re + num_im * lam_im) / denom
    fac_im = (num_im * lam_re - num_re * lam_im) / denom
    bbar_re = fac_re[..., None] * b_re - fac_im[..., None] * b_im
    bbar_im = fac_re[..., None] * b_im + fac_im[..., None] * b_re
    return abar_re, abar_im, bbar_re, bbar_im


def kernel(x, norm1_g, w_in, sgu_norm_g, sgu_w, sgu_b, s5_lambda_re, s5_lambda_im, s5_log_dt, s5_b_re, s5_b_im, s5_c_re, s5_c_im, s5_d, s5_glu_w, s5_glu_b, lru_conv_w, lru_conv_b, lru_wa, lru_ba, lru_wx, lru_bx, lru_lambda, fox_fgate_b, mix_norm_g, w_out, norm2_g, w_mlp_in, w_mlp_out, final_g):
    w = MIXER_WIDTH
    row = lambda v: v.reshape(DEPTH, 1, -1)

    w_in_main = w_in[..., 0:8 * w].astype(BF16)
    w_in_f = jnp.pad(w_in[..., 8 * w:], ((0, 0), (0, 0), (0, LANES - FOX_HEADS))).astype(BF16)
    vec_in = row(jnp.concatenate([norm1_g, sgu_norm_g], axis=-1))
    vec_mix = row(jnp.concatenate(
        [s5_d, s5_glu_b, lru_conv_b, lru_ba.reshape(DEPTH, w), lru_bx.reshape(DEPTH, w), lru_lambda,
         jnp.pad(fox_fgate_b, ((0, 0), (0, LANES - FOX_HEADS)))], axis=-1))
    vec_mlp = row(jnp.concatenate(
        [mix_norm_g[:, 0:3 * w], _pad_heads(mix_norm_g[:, 3 * w:]), norm2_g], axis=-1))

    sgu_wcat = jnp.transpose(sgu_w, (0, 2, 1, 3)).reshape(DEPTH, SGU_CHUNK, SGU_HEADS * SGU_CHUNK)
    sgu_bias = jnp.repeat(jnp.transpose(sgu_b, (0, 2, 1)), w // SGU_HEADS, axis=2)

    abar_re, abar_im, bbar_re, bbar_im = _s5_discretize(s5_lambda_re, s5_lambda_im, s5_log_dt, s5_b_re, s5_b_im)
    swap = lambda t: jnp.transpose(t, (0, 1, 3, 2))
    bbd = jnp.concatenate([_block_diag(swap(bbar_re)), _block_diag(swap(bbar_im))], axis=2).astype(BF16)
    cbd = jnp.concatenate([_block_diag(swap(s5_c_re)), -_block_diag(swap(s5_c_im))], axis=1).astype(BF16)
    lam = jnp.concatenate([abar_re.reshape(DEPTH, SUBLANES, LANES), abar_im.reshape(DEPTH, SUBLANES, LANES)], axis=1)
    glu_w = s5_glu_w.astype(BF16)

    wa_bd = _block_diag(lru_wa).astype(BF16)
    wx_bd = _block_diag(lru_wx).astype(BF16)

    w_o_pad = jnp.concatenate(
        [w_out[:, 0:3 * w],
         jnp.pad(w_out[:, 3 * w:].reshape(DEPTH, FOX_HEADS, FOX_HEAD_DIM, D_MODEL),
                 ((0, 0), (0, 0), (0, HEAD_PAD - FOX_HEAD_DIM), (0, 0))).reshape(DEPTH, FOX_HEADS * HEAD_PAD, D_MODEL)],
        axis=1).astype(BF16)
    w1 = w_mlp_in.astype(BF16)
    w2 = w_mlp_out.astype(BF16)
    fg = final_g.reshape(1, D_MODEL)

    xf = x.reshape(N_TOK, D_MODEL)
    for l in range(DEPTH):
        y_a, zmix, qkv, zf = _in_proj(xf, vec_in, w_in_main, w_in_f, sgu_wcat, sgu_bias, l)
        zmix3 = zmix.reshape(BATCH, SEQ, ZMIX_COLS)
        y_b, y_c, qadd, kadd, vt = _mixers(
            zmix3, zf.reshape(BATCH, SEQ, LANES), qkv.reshape(BATCH, SEQ, QKV_COLS), vec_mix,
            (bbd, lam, cbd, glu_w), (lru_conv_w, wa_bd, wx_bd), l)
        qadd = qadd.reshape(N_TOK, FOX_HEADS * HEAD_PAD)
        kadd = kadd.reshape(N_TOK, FOX_HEADS * HEAD_PAD)
        y_d = _attn(qkv, qadd, kadd, vt)
        xf = _merge_mlp(y_a, y_b.reshape(N_TOK, w), y_c.reshape(N_TOK, w), y_d, xf, vec_mlp, w_o_pad,
                        w1, w2, fg, l, final_norm=(l == DEPTH - 1))
    return xf.reshape(BATCH, SEQ, D_MODEL)
```

```python
import functools

import jax
import jax.numpy as jnp
import numpy as np
from jax import lax
from jax.experimental import pallas as pl
from jax.experimental.pallas import tpu as pltpu

D_MODEL = 1024
BATCH = 4
SEQ = 4096
DEPTH = 4
N_TOK = BATCH * SEQ
MIXER_WIDTH = 256
SGU_HEADS = 4
SGU_CHUNK = 128
S5_GROUP = 16
S5_GROUPS = 16
S5_STATE = 64
LRU_HEADS = 4
LRU_CONV = 4
LRU_C = 8.0
FOX_HEADS = 4
FOX_HEAD_DIM = 64
D_FF = 4 * D_MODEL
RMS_EPS = 1e-6
LOG2E = 1.4426950408889634

LANES = 128
SUBLANES = 8
HEAD_PAD = LANES
ZMIX_COLS = 3 * MIXER_WIDTH
QKV_COLS = 3 * FOX_HEADS * HEAD_PAD
S5_NSTATE = S5_GROUPS * S5_STATE
S5_ROWS = 2 * S5_NSTATE // LANES
SCAN_ROW_GAP = 4
S5_PITCH = S5_ROWS + SCAN_ROW_GAP

TM_IN = 1024
TM_MLP = 1024
TF_MLP = 2048
T_SCAN = 256
TQ = 512
TK = 512
ACC_ROWS = 80
VMEM_LIMIT = 48 * 1024 * 1024
VMEM_LIMIT_MIXERS = 56 * 1024 * 1024

F32 = jnp.float32
BF16 = jnp.bfloat16
MIX_OUT = BF16


def _params(n_axes, vmem_limit=VMEM_LIMIT):
    return pltpu.CompilerParams(dimension_semantics=("arbitrary",) * n_axes,
                                vmem_limit_bytes=vmem_limit)


def _layer_block(shape, layer):
    zeros = (0,) * len(shape)
    return pl.BlockSpec((None,) + tuple(shape), lambda *_: (layer,) + zeros)


def _rms(x, g, width):
    ms = jnp.sum(jnp.square(x), axis=-1, keepdims=True) * (1.0 / width)
    return x * lax.rsqrt(ms + RMS_EPS) * g


def _softplus(x):
    return jnp.maximum(x, 0.0) + jnp.log1p(jnp.exp(-jnp.abs(x)))


def _sgu_mix(zu, zv, g, w, bias, o_ref):
    n_rows = zu.shape[0]
    u = jax.nn.gelu(zu)
    v = _rms(jax.nn.gelu(zv), g, MIXER_WIDTH)
    hd = MIXER_WIDTH // SGU_HEADS
    lane_head = lax.broadcasted_iota(jnp.int32, (SGU_CHUNK, MIXER_WIDTH), 1) // hd
    t_idx = lax.broadcasted_iota(jnp.int32, (SGU_CHUNK, SGU_HEADS * SGU_CHUNK), 0)
    s_idx = lax.broadcasted_iota(jnp.int32, (SGU_CHUNK, SGU_HEADS * SGU_CHUNK), 1) % SGU_CHUNK
    wm = jnp.where(s_idx <= t_idx, w, 0.0).astype(BF16)
    for c in range(n_rows // SGU_CHUNK):
        rows = slice(c * SGU_CHUNK, (c + 1) * SGU_CHUNK)
        vc = v[rows]
        vstack = jnp.concatenate(
            [jnp.where(lane_head == h, vc, 0.0) for h in range(SGU_HEADS)], axis=0).astype(BF16)
        mixed = jnp.dot(wm, vstack, preferred_element_type=F32) + bias
        o_ref[rows, :] = (u[rows] * mixed).astype(MIX_OUT)


def _in_proj_kernel(x_ref, pv_ref, w_ref, wf_ref, sw_ref, sb_ref, ya_ref, zmix_ref, qkv_ref, zf_ref):
    w = MIXER_WIDTH
    h = _rms(x_ref[...], pv_ref[:, 0:D_MODEL], D_MODEL).astype(BF16)
    z5 = jnp.dot(h, w_ref[:, 0:5 * w], preferred_element_type=F32)
    zmix_ref[...] = z5[:, 2 * w:5 * w]
    _sgu_mix(z5[:, 0:w], z5[:, w:2 * w], pv_ref[:, D_MODEL:D_MODEL + w], sw_ref[...], sb_ref[...], ya_ref)
    w_qkvf = jnp.concatenate([w_ref[:, 5 * w:8 * w], wf_ref[...]], axis=1)
    zqkvf = jnp.dot(h, w_qkvf, preferred_element_type=F32)
    zf_ref[...] = zqkvf[:, 3 * w:]
    pad = jnp.zeros((TM_IN, HEAD_PAD - FOX_HEAD_DIM), F32)
    for slot in range(3 * FOX_HEADS):
        head = zqkvf[:, slot * FOX_HEAD_DIM:(slot + 1) * FOX_HEAD_DIM]
        qkv_ref[:, slot * HEAD_PAD:(slot + 1) * HEAD_PAD] = jnp.concatenate([head, pad], axis=1).astype(BF16)


def _in_proj(x, vecs, w, wf, sgu_w, sgu_b, layer):
    row = lambda i: (i, 0)
    return pl.pallas_call(
        _in_proj_kernel,
        grid=(N_TOK // TM_IN,),
        in_specs=[pl.BlockSpec((TM_IN, D_MODEL), row),
                  _layer_block((1, D_MODEL + MIXER_WIDTH), layer),
                  _layer_block((D_MODEL, 8 * MIXER_WIDTH), layer),
                  _layer_block((D_MODEL, LANES), layer),
                  _layer_block((SGU_CHUNK, SGU_HEADS * SGU_CHUNK), layer),
                  _layer_block((SGU_CHUNK, MIXER_WIDTH), layer)],
        out_specs=[pl.BlockSpec((TM_IN, MIXER_WIDTH), row),
                   pl.BlockSpec((TM_IN, ZMIX_COLS), row),
                   pl.BlockSpec((TM_IN, QKV_COLS), row),
                   pl.BlockSpec((TM_IN, LANES), row)],
        out_shape=[jax.ShapeDtypeStruct((N_TOK, MIXER_WIDTH), MIX_OUT),
                   jax.ShapeDtypeStruct((N_TOK, ZMIX_COLS), F32),
                   jax.ShapeDtypeStruct((N_TOK, QKV_COLS), BF16),
                   jax.ShapeDtypeStruct((N_TOK, LANES), F32)],
        compiler_params=_params(1),
        name="in_proj_sgu",
    )(x, vecs, w, wf, sgu_w, sgu_b)


def _split3(x):
    hi = x.astype(BF16)
    r1 = x - hi.astype(F32)
    mid = r1.astype(BF16)
    lo = (r1 - mid.astype(F32)).astype(BF16)
    return hi, mid, lo


def _bias_placement():
    pq = np.zeros((LANES, FOX_HEADS * HEAD_PAD), np.float32)
    pk = np.zeros_like(pq)
    ones_q = np.zeros((1, FOX_HEADS * HEAD_PAD), np.float32)
    ones_k = np.zeros_like(ones_q)
    for h in range(FOX_HEADS):
        for piece in range(3):
            pq[piece * FOX_HEADS + h, h * HEAD_PAD + FOX_HEAD_DIM + piece] = 1.0
            pk[piece * FOX_HEADS + h, h * HEAD_PAD + FOX_HEAD_DIM + 3 + piece] = -1.0
            ones_q[0, h * HEAD_PAD + FOX_HEAD_DIM + 3 + piece] = 1.0
            ones_k[0, h * HEAD_PAD + FOX_HEAD_DIM + piece] = 1.0
    return (jnp.asarray(pq, BF16), jnp.asarray(pk, BF16), jnp.asarray(ones_q), jnp.asarray(ones_k))


def _fcum_chunk(zf_ref, bf_ref, v_ref, tril_ref, pq_ref, pk_ref, oq_ref, ok_ref, qadd_ref, kadd_ref, vt_ref,
                carry_ref, batches):
    tril = tril_ref[...]
    lane = lax.broadcasted_iota(jnp.int32, (T_SCAN, LANES), 1)
    ones_row = lax.broadcasted_iota(jnp.int32, (HEAD_PAD, T_SCAN), 0) == FOX_HEAD_DIM
    for b in batches:
        for n in range(FOX_HEADS):
            v_t = v_ref[b, :, n * HEAD_PAD:(n + 1) * HEAD_PAD].astype(F32).T
            vt_ref[b, n * HEAD_PAD:(n + 1) * HEAD_PAD, :] = jnp.where(ones_row, 1.0, v_t).astype(BF16)
        logit = zf_ref[b] + bf_ref[...]
        log_f = -_softplus(-logit)
        hi, mid, lo = _split3(log_f)
        cs = (jnp.dot(tril, hi, preferred_element_type=F32)
              + jnp.dot(tril, mid, preferred_element_type=F32)
              + jnp.dot(tril, lo, preferred_element_type=F32))
        cum = cs + carry_ref[b]
        carry_ref[b] = cum[T_SCAN - 1:T_SCAN, :]
        hi, mid, lo = [p.astype(F32) for p in _split3(cum * LOG2E)]
        pieces = jnp.where(lane < FOX_HEADS, hi,
                           jnp.where(lane < 2 * FOX_HEADS, pltpu.roll(mid, FOX_HEADS, axis=1),
                                     jnp.where(lane < 3 * FOX_HEADS, pltpu.roll(lo, 2 * FOX_HEADS, axis=1), 0.0))
                           ).astype(BF16)
        qadd_ref[b] = (jnp.dot(pieces, pq_ref[...], preferred_element_type=F32) + oq_ref[...]).astype(BF16)
        kadd_ref[b] = (jnp.dot(pieces, pk_ref[...], preferred_element_type=F32) + ok_ref[...]).astype(BF16)


def _s5_load(z_ref, bbd_ref, s_ref, batches):
    for b in batches:
        u = z_ref[b, :, 0:MIXER_WIDTH]
        bu = jnp.dot(u.astype(BF16), bbd_ref[...], preferred_element_type=F32)
        for j in range(S5_ROWS):
            s_ref[b, pl.ds(j, T_SCAN, stride=S5_PITCH), :] = bu[:, j * LANES:(j + 1) * LANES]


def _s5_step(t, carry, lam_re, lam_im, s_ref):
    base = t * S5_PITCH
    new = []
    for b in range(BATCH):
        h_re, h_im = carry[2 * b], carry[2 * b + 1]
        n_re = lam_re * h_re - lam_im * h_im + s_ref[b, pl.ds(base, SUBLANES), :]
        n_im = lam_re * h_im + lam_im * h_re + s_ref[b, pl.ds(base + SUBLANES, SUBLANES), :]
        s_ref[b, pl.ds(base, SUBLANES), :] = n_re
        s_ref[b, pl.ds(base + SUBLANES, SUBLANES), :] = n_im
        new += [n_re, n_im]
    return new


def _s5_out(z_ref, cbd_ref, d_ref, gw_ref, gb_ref, o_ref, s_ref):
    states = jnp.concatenate(
        [jnp.concatenate([s_ref[b, pl.ds(j, T_SCAN, stride=S5_PITCH), :].astype(BF16) for j in range(S5_ROWS)],
                         axis=1) for b in range(BATCH)], axis=0)
    u = z_ref[:, :, 0:MIXER_WIDTH].reshape(BATCH * T_SCAN, MIXER_WIDTH)
    y = jnp.dot(states, cbd_ref[...], preferred_element_type=F32) + d_ref[...] * u
    y = jax.nn.gelu(y)
    gate = jnp.dot(y.astype(BF16), gw_ref[...], preferred_element_type=F32) + gb_ref[...]
    o_ref[...] = (y * jax.nn.sigmoid(gate)).astype(MIX_OUT).reshape(BATCH, T_SCAN, MIXER_WIDTH)


LRU_SLOTS = BATCH * MIXER_WIDTH // LANES
LRU_PITCH = LRU_SLOTS + SCAN_ROW_GAP
LRU_TILES = MIXER_WIDTH // LANES


def _lru_load(z_ref, cw_ref, cb_ref, wa_ref, ba_ref, wx_ref, bx_ref, lam_ref, tail_ref, a_ref, b_ref, batches):
    decay_rate = LRU_C * _softplus(-lam_ref[...])
    for b in batches:
        x = z_ref[b, :, MIXER_WIDTH:2 * MIXER_WIDTH]
        xp = jnp.concatenate([tail_ref[b], x], axis=0)
        tail_ref[b] = x[T_SCAN - SUBLANES:T_SCAN, :]
        xc = cb_ref[...]
        for k in range(LRU_CONV):
            off = SUBLANES - (LRU_CONV - 1) + k
            xc = xc + cw_ref[k:k + 1, :] * xp[off:off + T_SCAN, :]
        xcb = xc.astype(BF16)
        r = jax.nn.sigmoid(jnp.dot(xcb, wa_ref[...], preferred_element_type=F32) + ba_ref[...])
        i = jax.nn.sigmoid(jnp.dot(xcb, wx_ref[...], preferred_element_type=F32) + bx_ref[...])
        log_a = -(r * decay_rate)
        a = jnp.exp(log_a)
        inp = jnp.sqrt(-jnp.tanh(log_a) * (a * a + 1.0)) * (i * xc)
        for j in range(LRU_TILES):
            slot = b * LRU_TILES + j
            a_ref[pl.ds(slot, T_SCAN, stride=LRU_PITCH), :] = a[:, j * LANES:(j + 1) * LANES]
            b_ref[pl.ds(slot, T_SCAN, stride=LRU_PITCH), :] = inp[:, j * LANES:(j + 1) * LANES]


def _lru_step(t, h, a_ref, b_ref):
    base = t * LRU_PITCH
    h = a_ref[pl.ds(base, LRU_SLOTS), :] * h + b_ref[pl.ds(base, LRU_SLOTS), :]
    b_ref[pl.ds(base, LRU_SLOTS), :] = h
    return h


def _lru_out(z_ref, o_ref, b_ref):
    for b in range(BATCH):
        h = jnp.concatenate(
            [b_ref[pl.ds(b * LRU_TILES + j, T_SCAN, stride=LRU_PITCH), :] for j in range(LRU_TILES)], axis=1)
        o_ref[b] = (h * jax.nn.gelu(z_ref[b, :, 2 * MIXER_WIDTH:3 * MIXER_WIDTH])).astype(MIX_OUT)


def _mixers_kernel(z_ref, zf_ref, v_ref, pv_ref,
                   bbd_ref, lam5_ref, cbd_ref, gw_ref, cw_ref, wa_ref, wx_ref,
                   tril_ref, pq_ref, pk_ref, oq_ref, ok_ref,
                   yb_ref, yc_ref, qadd_ref, kadd_ref, vt_ref,
                   s_ref, h5_ref, tail_ref, a_ref, b_ref, hl_ref, carry_ref):
    w = MIXER_WIDTH
    d_ref, gb_ref, cb_ref, ba_ref, bx_ref, lam_ref = [pv_ref.at[:, k * w:(k + 1) * w] for k in range(6)]
    bf_ref = pv_ref.at[:, 6 * w:6 * w + LANES]
    @pl.when(pl.program_id(0) == 0)
    def _():
        h5_ref[...] = jnp.zeros_like(h5_ref)
        tail_ref[...] = jnp.zeros_like(tail_ref)
        hl_ref[...] = jnp.zeros_like(hl_ref)
        carry_ref[...] = jnp.zeros_like(carry_ref)

    for b in range(BATCH):
        _s5_load(z_ref, bbd_ref, s_ref, (b,))
        _lru_load(z_ref, cw_ref, cb_ref, wa_ref, ba_ref, wx_ref, bx_ref, lam_ref, tail_ref, a_ref, b_ref, (b,))
        _fcum_chunk(zf_ref, bf_ref, v_ref, tril_ref, pq_ref, pk_ref, oq_ref, ok_ref, qadd_ref, kadd_ref, vt_ref,
                    carry_ref, (b,))

    lam_re = lam5_ref[0:SUBLANES, :]
    lam_im = lam5_ref[SUBLANES:S5_ROWS, :]

    def step(t, carry):
        new5 = _s5_step(t, carry[:-1], lam_re, lam_im, s_ref)
        return tuple(new5) + (_lru_step(t, carry[-1], a_ref, b_ref),)

    init = []
    for b in range(BATCH):
        init += [h5_ref[b, 0:SUBLANES, :], h5_ref[b, SUBLANES:S5_ROWS, :]]
    fin = lax.fori_loop(0, T_SCAN, step, tuple(init) + (hl_ref[...],), unroll=8)
    for b in range(BATCH):
        h5_ref[b, 0:SUBLANES, :] = fin[2 * b]
        h5_ref[b, SUBLANES:S5_ROWS, :] = fin[2 * b + 1]
    hl_ref[...] = fin[-1]

    _s5_out(z_ref, cbd_ref, d_ref, gw_ref, gb_ref, yb_ref, s_ref)
    _lru_out(z_ref, yc_ref, b_ref)


MIXER_VEC_COLS = 6 * MIXER_WIDTH + LANES


def _mixers(zmix3, zf3, qkv3, vecs, s5_params, lru_params, layer):
    bbd, lam5, cbd, gw = s5_params
    cw, wa, wx = lru_params
    pq, pk, ones_q, ones_k = _bias_placement()
    tril = jnp.asarray(np.tril(np.ones((T_SCAN, T_SCAN), np.float32)), BF16)
    full = lambda c: (0, 0)
    chunk = lambda c: (0, c, 0)
    wide = FOX_HEADS * HEAD_PAD
    mat = _layer_block((MIXER_WIDTH, MIXER_WIDTH), layer)
    mix_out = pl.BlockSpec((BATCH, T_SCAN, MIXER_WIDTH), chunk)
    bias_out = pl.BlockSpec((BATCH, T_SCAN, wide), chunk)
    return pl.pallas_call(
        _mixers_kernel,
        grid=(SEQ // T_SCAN,),
        in_specs=[pl.BlockSpec((BATCH, T_SCAN, ZMIX_COLS), chunk),
                  pl.BlockSpec((BATCH, T_SCAN, LANES), chunk),
                  pl.BlockSpec((BATCH, T_SCAN, wide), lambda c: (0, c, 2)),
                  _layer_block((1, MIXER_VEC_COLS), layer),
                  _layer_block((MIXER_WIDTH, 2 * S5_NSTATE), layer),
                  _layer_block((S5_ROWS, LANES), layer),
                  _layer_block((2 * S5_NSTATE, MIXER_WIDTH), layer),
                  mat,
                  _layer_block((LRU_CONV, MIXER_WIDTH), layer), mat, mat,
                  pl.BlockSpec((T_SCAN, T_SCAN), full),
                  pl.BlockSpec((LANES, wide), full),
                  pl.BlockSpec((LANES, wide), full),
                  pl.BlockSpec((1, wide), full),
                  pl.BlockSpec((1, wide), full)],
        out_specs=[mix_out, mix_out, bias_out, bias_out,
                   pl.BlockSpec((BATCH, wide, T_SCAN), lambda c: (0, 0, c))],
        out_shape=[jax.ShapeDtypeStruct((BATCH, SEQ, MIXER_WIDTH), MIX_OUT),
                   jax.ShapeDtypeStruct((BATCH, SEQ, MIXER_WIDTH), MIX_OUT),
                   jax.ShapeDtypeStruct((BATCH, SEQ, wide), BF16),
                   jax.ShapeDtypeStruct((BATCH, SEQ, wide), BF16),
                   jax.ShapeDtypeStruct((BATCH, wide, SEQ), BF16)],
        scratch_shapes=[pltpu.VMEM((BATCH, T_SCAN * S5_PITCH, LANES), F32),
                        pltpu.VMEM((BATCH, S5_ROWS, LANES), F32),
                        pltpu.VMEM((BATCH, SUBLANES, MIXER_WIDTH), F32),
                        pltpu.VMEM((T_SCAN * LRU_PITCH, LANES), F32),
                        pltpu.VMEM((T_SCAN * LRU_PITCH, LANES), F32),
                        pltpu.VMEM((LRU_SLOTS, LANES), F32),
                        pltpu.VMEM((BATCH, 1, LANES), F32)],
        compiler_params=_params(1, VMEM_LIMIT_MIXERS),
        name="recurrent_mixers",
    )(zmix3, zf3, qkv3, vecs, bbd, lam5, cbd, gw, cw, wa, wx, tril, pq, pk, ones_q, ones_k)


def _attn_kernel(q_ref, qadd_ref, k_ref, kadd_ref, vt_ref, o_ref, s_ref, m_ref, acc_ref):
    i = pl.program_id(2)
    scale = FOX_HEAD_DIM ** -0.5 * LOG2E
    neg = jnp.finfo(F32).min
    slots = [slice(n * HEAD_PAD, (n + 1) * HEAD_PAD) for n in range(FOX_HEADS)]
    qs = [(q_ref[:, sl].astype(F32) * scale + qadd_ref[:, sl].astype(F32)).astype(BF16) for sl in slots]

    def logits_t(n, j):
        start = pl.multiple_of(j * TK, TK)
        ks = k_ref[pl.ds(start, TK), slots[n]] + kadd_ref[pl.ds(start, TK), slots[n]]
        return lax.dot_general(ks, qs[n], (((1,), (1,)), ((), ())), preferred_element_type=F32)

    def block(j, masked, prefetch):
        start = pl.multiple_of(j * TK, TK)
        for n in range(FOX_HEADS):
            s = s_ref[n]
            if masked:
                key = lax.broadcasted_iota(jnp.int32, (TK, TQ), 0)
                qry = lax.broadcasted_iota(jnp.int32, (TK, TQ), 1)
                s = jnp.where(key <= qry, s, neg)
            m = m_ref[n]
            m_new = jnp.maximum(m, jnp.max(s, axis=0, keepdims=True))
            alpha = jnp.exp2(m - m_new)
            p = jnp.exp2(s - jnp.concatenate([m_new] * (TK // SUBLANES), axis=0))
            m_ref[n] = m_new
            vt = vt_ref[0, n * HEAD_PAD:n * HEAD_PAD + ACC_ROWS, pl.ds(start, TK)]
            acc_ref[n] = (jnp.concatenate([alpha] * (ACC_ROWS // SUBLANES), axis=0) * acc_ref[n]
                          + jnp.dot(vt, p.astype(BF16), preferred_element_type=F32))
            if prefetch:
                s_ref[n] = logits_t(n, j + 1)

    for n in range(FOX_HEADS):
        s_ref[n] = logits_t(n, 0)
    m_ref[...] = jnp.full(m_ref.shape, neg, F32)
    acc_ref[...] = jnp.zeros_like(acc_ref)

    @pl.loop(0, i // 2)
    def _(jj):
        block(2 * jj, False, True)
        block(2 * jj + 1, False, True)

    @pl.when(i % 2 == 1)
    def _():
        block(i - 1, False, True)

    block(i, True, False)
    for n, sl in enumerate(slots):
        acc = acc_ref[n]
        out_t = acc[0:FOX_HEAD_DIM] / acc[FOX_HEAD_DIM:FOX_HEAD_DIM + 1]
        out_t = jnp.concatenate([out_t, jnp.zeros((HEAD_PAD - FOX_HEAD_DIM, TQ), F32)], axis=0)
        o_ref[:, sl] = out_t.T.astype(MIX_OUT)


def _attn(qkv, qadd, kadd, vt):
    n_q = SEQ // TQ
    width = FOX_HEADS * HEAD_PAD
    qrow = lambda b, h, i: (b * n_q + i, 0)
    return pl.pallas_call(
        _attn_kernel,
        grid=(BATCH, 1, n_q),
        in_specs=[pl.BlockSpec((TQ, width), qrow),
                  pl.BlockSpec((TQ, width), qrow),
                  pl.BlockSpec((SEQ, width), lambda b, h, i: (b, 1)),
                  pl.BlockSpec((SEQ, width), lambda b, h, i: (b, 0)),
                  pl.BlockSpec((1, width, SEQ), lambda b, h, i: (b, 0, 0))],
        out_specs=pl.BlockSpec((TQ, width), qrow),
        out_shape=jax.ShapeDtypeStruct((N_TOK, width), MIX_OUT),
        scratch_shapes=[pltpu.VMEM((FOX_HEADS, TK, TQ), F32),
                        pltpu.VMEM((FOX_HEADS, SUBLANES, TQ), F32),
                        pltpu.VMEM((FOX_HEADS, ACC_ROWS, TQ), F32)],
        compiler_params=_params(3),
        name="fox_attention",
    )(qkv, qadd, qkv, kadd, vt)


def _merge_mlp_kernel(ya_ref, yb_ref, yc_ref, yd_ref, x_ref, pv_ref, wo_ref, w1_ref, w2_ref,
                      fg_ref, o_ref, h_ref, *, final_norm):
    j = pl.program_id(1)
    n_abc = 3 * MIXER_WIDTH
    n_d = FOX_HEADS * HEAD_PAD
    gm_ref = pv_ref.at[:, 0:n_abc]
    gmd_ref = pv_ref.at[:, n_abc:n_abc + n_d]
    g2_ref = pv_ref.at[:, n_abc + n_d:n_abc + n_d + D_MODEL]

    @pl.when(j == 0)
    def _():
        w = MIXER_WIDTH
        parts = [_rms(ya_ref[...].astype(F32), gm_ref[:, 0:w], w),
                 _rms(yb_ref[...].astype(F32), gm_ref[:, w:2 * w], w),
                 _rms(yc_ref[...].astype(F32), gm_ref[:, 2 * w:3 * w], w),
                 _rms(yd_ref[...].astype(F32), gmd_ref[...], w)]
        y = jnp.concatenate(parts, axis=1).astype(BF16)
        x1 = x_ref[...] + jnp.dot(y, wo_ref[...], preferred_element_type=F32)
        o_ref[...] = x1
        h_ref[...] = _rms(x1, g2_ref[...], D_MODEL).astype(BF16)

    a = jnp.dot(h_ref[...], w1_ref[...], preferred_element_type=F32)
    a = jnp.square(jnp.maximum(a, 0.0)).astype(BF16)
    o_ref[...] += jnp.dot(a, w2_ref[...], preferred_element_type=F32)

    if final_norm:
        @pl.when(j == pl.num_programs(1) - 1)
        def _():
            o_ref[...] = _rms(o_ref[...], fg_ref[...], D_MODEL)


def _merge_mlp(ya, yb, yc, yd, x, vecs, wo, w1, w2, fg, layer, final_norm):
    row = lambda i, j: (i, 0)
    k_dim = 3 * MIXER_WIDTH + FOX_HEADS * HEAD_PAD
    mix = pl.BlockSpec((TM_MLP, MIXER_WIDTH), row)
    return pl.pallas_call(
        functools.partial(_merge_mlp_kernel, final_norm=final_norm),
        grid=(N_TOK // TM_MLP, D_FF // TF_MLP),
        in_specs=[mix, mix, mix,
                  pl.BlockSpec((TM_MLP, FOX_HEADS * HEAD_PAD), row),
                  pl.BlockSpec((TM_MLP, D_MODEL), row),
                  _layer_block((1, k_dim + D_MODEL), layer),
                  _layer_block((k_dim, D_MODEL), layer),
                  pl.BlockSpec((None, D_MODEL, TF_MLP), lambda i, j: (layer, 0, j)),
                  pl.BlockSpec((None, TF_MLP, D_MODEL), lambda i, j: (layer, j, 0)),
                  pl.BlockSpec((1, D_MODEL), lambda i, j: (0, 0))],
        out_specs=pl.BlockSpec((TM_MLP, D_MODEL), row),
        out_shape=jax.ShapeDtypeStruct((N_TOK, D_MODEL), F32),
        scratch_shapes=[pltpu.VMEM((TM_MLP, D_MODEL), BF16)],
        compiler_params=_params(2, VMEM_LIMIT_MIXERS),
        name="merge_mlp",
    )(ya, yb, yc, yd, x, vecs, wo, w1, w2, fg)


def _pad_heads(w):
    lead = w.shape[:-1]
    w = w.reshape(*lead, FOX_HEADS, FOX_HEAD_DIM)
    w = jnp.pad(w, [(0, 0)] * (len(lead) + 1) + [(0, HEAD_PAD - FOX_HEAD_DIM)])
    return w.reshape(*lead, FOX_HEADS * HEAD_PAD)


def _block_diag(blocks):
    n, g, r, c = blocks.shape
    tiled = jnp.tile(blocks.reshape(n, g * r, c), (1, 1, g))
    on_diag = (np.arange(g * r)[:, None] // r) == (np.arange(g * c)[None, :] // c)
    return jnp.where(jnp.asarray(on_diag), tiled, 0.0)


def _s5_discretize(lam_re, lam_im, log_dt, b_re, b_im):
    dt = jnp.exp(log_dt)[..., None]
    mag = jnp.exp(lam_re * dt)
    abar_re = mag * jnp.cos(lam_im * dt)
    abar_im = mag * jnp.sin(lam_im * dt)
    denom = jnp.square(lam_re) + jnp.square(lam_im)
    num_re = abar_re - 1.0
    num_im = abar_im
    fac_re = (num_re * lam_re + num_im * lam_im) / denom
    fac_im = (num_im * lam_re - num_re * lam_im) / denom
    bbar_re = fac_re[..., None] * b_re - fac_im[..., None] * b_im
    bbar_im = fac_re[..., None] * b_im + fac_im[..., None] * b_re
    return abar_re, abar_im, bbar_re, bbar_im


def kernel(x, norm1_g, w_in, sgu_norm_g, sgu_w, sgu_b, s5_lambda_re, s5_lambda_im, s5_log_dt, s5_b_re, s5_b_im, s5_c_re, s5_c_im, s5_d, s5_glu_w, s5_glu_b, lru_conv_w, lru_conv_b, lru_wa, lru_ba, lru_wx, lru_bx, lru_lambda, fox_fgate_b, mix_norm_g, w_out, norm2_g, w_mlp_in, w_mlp_out, final_g):
    w = MIXER_WIDTH
    row = lambda v: v.reshape(DEPTH, 1, -1)

    w_in_main = w_in[..., 0:8 * w].astype(BF16)
    w_in_f = jnp.pad(w_in[..., 8 * w:], ((0, 0), (0, 0), (0, LANES - FOX_HEADS))).astype(BF16)
    vec_in = row(jnp.concatenate([norm1_g, sgu_norm_g], axis=-1))
    vec_mix = row(jnp.concatenate(
        [s5_d, s5_glu_b, lru_conv_b, lru_ba.reshape(DEPTH, w), lru_bx.reshape(DEPTH, w), lru_lambda,
         jnp.pad(fox_fgate_b, ((0, 0), (0, LANES - FOX_HEADS)))], axis=-1))
    vec_mlp = row(jnp.concatenate(
        [mix_norm_g[:, 0:3 * w], _pad_heads(mix_norm_g[:, 3 * w:]), norm2_g], axis=-1))

    sgu_wcat = jnp.transpose(sgu_w, (0, 2, 1, 3)).reshape(DEPTH, SGU_CHUNK, SGU_HEADS * SGU_CHUNK)
    sgu_bias = jnp.repeat(jnp.transpose(sgu_b, (0, 2, 1)), w // SGU_HEADS, axis=2)

    abar_re, abar_im, bbar_re, bbar_im = _s5_discretize(s5_lambda_re, s5_lambda_im, s5_log_dt, s5_b_re, s5_b_im)
    swap = lambda t: jnp.transpose(t, (0, 1, 3, 2))
    bbd = jnp.concatenate([_block_diag(swap(bbar_re)), _block_diag(swap(bbar_im))], axis=2).astype(BF16)
    cbd = jnp.concatenate([_block_diag(swap(s5_c_re)), -_block_diag(swap(s5_c_im))], axis=1).astype(BF16)
    lam = jnp.concatenate([abar_re.reshape(DEPTH, SUBLANES, LANES), abar_im.reshape(DEPTH, SUBLANES, LANES)], axis=1)
    glu_w = s5_glu_w.astype(BF16)

    wa_bd = _block_diag(lru_wa).astype(BF16)
    wx_bd = _block_diag(lru_wx).astype(BF16)

    w_o_pad = jnp.concatenate(
        [w_out[:, 0:3 * w],
         jnp.pad(w_out[:, 3 * w:].reshape(DEPTH, FOX_HEADS, FOX_HEAD_DIM, D_MODEL),
                 ((0, 0), (0, 0), (0, HEAD_PAD - FOX_HEAD_DIM), (0, 0))).reshape(DEPTH, FOX_HEADS * HEAD_PAD, D_MODEL)],
        axis=1).astype(BF16)
    w1 = w_mlp_in.astype(BF16)
    w2 = w_mlp_out.astype(BF16)
    fg = final_g.reshape(1, D_MODEL)

    xf = x.reshape(N_TOK, D_MODEL)
    for l in range(DEPTH):
        y_a, zmix, qkv, zf = _in_proj(xf, vec_in, w_in_main, w_in_f, sgu_wcat, sgu_bias, l)
        zmix3 = zmix.reshape(BATCH, SEQ, ZMIX_COLS)
        y_b, y_c, qadd, kadd, vt = _mixers(
            zmix3, zf.reshape(BATCH, SEQ, LANES), qkv.reshape(BATCH, SEQ, QKV_COLS), vec_mix,
            (bbd, lam, cbd, glu_w), (lru_conv_w, wa_bd, wx_bd), l)
        qadd = qadd.reshape(N_TOK, FOX_HEADS * HEAD_PAD)
        kadd = kadd.reshape(N_TOK, FOX_HEADS * HEAD_PAD)
        y_d = _attn(qkv, qadd, kadd, vt)
        xf = _merge_mlp(y_a, y_b.reshape(N_TOK, w), y_c.reshape(N_TOK, w), y_d, xf, vec_mlp, w_o_pad,
                        w1, w2, fg, l, final_norm=(l == DEPTH - 1))
    return xf.reshape(BATCH, SEQ, D_MODEL)
```

```python
import functools

import jax
import jax.numpy as jnp
import numpy as np
from jax import lax
from jax.experimental import pallas as pl
from jax.experimental.pallas import tpu as pltpu

D_MODEL = 1024
BATCH = 4
SEQ = 4096
DEPTH = 4
N_TOK = BATCH * SEQ
MIXER_WIDTH = 256
SGU_HEADS = 4
SGU_CHUNK = 128
S5_GROUP = 16
S5_GROUPS = 16
S5_STATE = 64
LRU_HEADS = 4
LRU_CONV = 4
LRU_C = 8.0
FOX_HEADS = 4
FOX_HEAD_DIM = 64
D_FF = 4 * D_MODEL
RMS_EPS = 1e-6
LOG2E = 1.4426950408889634

LANES = 128
SUBLANES = 8
HEAD_PAD = LANES
ZMIX_COLS = 3 * MIXER_WIDTH
QKV_COLS = 3 * FOX_HEADS * HEAD_PAD
W_IN_RAW = 8 * MIXER_WIDTH + FOX_HEADS
S5_NSTATE = S5_GROUPS * S5_STATE
S5_ROWS = 2 * S5_NSTATE // LANES
SCAN_ROW_GAP = 4
S5_PITCH = S5_ROWS + SCAN_ROW_GAP

TM_IN = 1024
TM_MLP = 1024
TF_MLP = 2048
T_SCAN = 256
TQ = 512
TK = 512
ACC_ROWS = 80
VMEM_LIMIT = 48 * 1024 * 1024
VMEM_LIMIT_MIXERS = 56 * 1024 * 1024

F32 = jnp.float32
BF16 = jnp.bfloat16
MIX_OUT = BF16


def _params(n_axes, vmem_limit=VMEM_LIMIT):
    return pltpu.CompilerParams(dimension_semantics=("arbitrary",) * n_axes,
                                vmem_limit_bytes=vmem_limit)


def _layer_block(shape, layer):
    zeros = (0,) * len(shape)
    return pl.BlockSpec((None,) + tuple(shape), lambda *_: (layer,) + zeros)


def _rms(x, g, width):
    ms = jnp.sum(jnp.square(x), axis=-1, keepdims=True) * (1.0 / width)
    return x * lax.rsqrt(ms + RMS_EPS) * g


def _softplus(x):
    return jnp.maximum(x, 0.0) + jnp.log1p(jnp.exp(-jnp.abs(x)))


def _sgu_mix(zu, zv, g, w, bias, o_ref):
    n_rows = zu.shape[0]
    u = jax.nn.gelu(zu)
    v = _rms(jax.nn.gelu(zv), g, MIXER_WIDTH)
    hd = MIXER_WIDTH // SGU_HEADS
    lane_head = lax.broadcasted_iota(jnp.int32, (SGU_CHUNK, MIXER_WIDTH), 1) // hd
    t_idx = lax.broadcasted_iota(jnp.int32, (SGU_CHUNK, SGU_HEADS * SGU_CHUNK), 0)
    s_idx = lax.broadcasted_iota(jnp.int32, (SGU_CHUNK, SGU_HEADS * SGU_CHUNK), 1) % SGU_CHUNK
    wm = jnp.where(s_idx <= t_idx, w, 0.0).astype(BF16)
    for c in range(n_rows // SGU_CHUNK):
        rows = slice(c * SGU_CHUNK, (c + 1) * SGU_CHUNK)
        vc = v[rows]
        vstack = jnp.concatenate(
            [jnp.where(lane_head == h, vc, 0.0) for h in range(SGU_HEADS)], axis=0).astype(BF16)
        mixed = jnp.dot(wm, vstack, preferred_element_type=F32) + bias
        o_ref[rows, :] = (u[rows] * mixed).astype(MIX_OUT)


def _in_proj_kernel(x_ref, pv_ref, w_ref, sw_ref, sb_ref, ya_ref, zmix_ref, qkv_ref, zf_ref, wb_ref):
    w = MIXER_WIDTH

    @pl.when(pl.program_id(0) == 0)
    def _():
        wb_ref[:, 0:8 * w] = w_ref[:, 0:8 * w].astype(BF16)
        forget = jnp.concatenate([w_ref[:, 8 * w:W_IN_RAW], jnp.zeros((D_MODEL, LANES - FOX_HEADS), F32)], axis=1)
        wb_ref[:, 8 * w:8 * w + LANES] = forget.astype(BF16)

    h = _rms(x_ref[...], pv_ref[:, 0:D_MODEL], D_MODEL).astype(BF16)
    z5 = jnp.dot(h, wb_ref[:, 0:5 * w], preferred_element_type=F32)
    zmix_ref[...] = z5[:, 2 * w:5 * w]
    _sgu_mix(z5[:, 0:w], z5[:, w:2 * w], pv_ref[:, D_MODEL:D_MODEL + w], sw_ref[...], sb_ref[...], ya_ref)
    zqkvf = jnp.dot(h, wb_ref[:, 5 * w:8 * w + LANES], preferred_element_type=F32)
    zf_ref[...] = zqkvf[:, 3 * w:]
    pad = jnp.zeros((TM_IN, HEAD_PAD - FOX_HEAD_DIM), F32)
    for slot in range(3 * FOX_HEADS):
        head = zqkvf[:, slot * FOX_HEAD_DIM:(slot + 1) * FOX_HEAD_DIM]
        qkv_ref[:, slot * HEAD_PAD:(slot + 1) * HEAD_PAD] = jnp.concatenate([head, pad], axis=1).astype(BF16)


def _in_proj(x, vecs, w, sgu_w, sgu_b, layer):
    row = lambda i: (i, 0)
    return pl.pallas_call(
        _in_proj_kernel,
        grid=(N_TOK // TM_IN,),
        in_specs=[pl.BlockSpec((TM_IN, D_MODEL), row),
                  _layer_block((1, D_MODEL + MIXER_WIDTH), layer),
                  _layer_block((D_MODEL, W_IN_RAW), layer),
                  _layer_block((SGU_CHUNK, SGU_HEADS * SGU_CHUNK), layer),
                  _layer_block((SGU_CHUNK, MIXER_WIDTH), layer)],
        out_specs=[pl.BlockSpec((TM_IN, MIXER_WIDTH), row),
                   pl.BlockSpec((TM_IN, ZMIX_COLS), row),
                   pl.BlockSpec((TM_IN, QKV_COLS), row),
                   pl.BlockSpec((TM_IN, LANES), row)],
        out_shape=[jax.ShapeDtypeStruct((N_TOK, MIXER_WIDTH), MIX_OUT),
                   jax.ShapeDtypeStruct((N_TOK, ZMIX_COLS), F32),
                   jax.ShapeDtypeStruct((N_TOK, QKV_COLS), BF16),
                   jax.ShapeDtypeStruct((N_TOK, LANES), F32)],
        scratch_shapes=[pltpu.VMEM((D_MODEL, 8 * MIXER_WIDTH + LANES), BF16)],
        compiler_params=_params(1, VMEM_LIMIT_MIXERS),
        name="in_proj_sgu",
    )(x, vecs, w, sgu_w, sgu_b)


def _split3(x):
    hi = x.astype(BF16)
    r1 = x - hi.astype(F32)
    mid = r1.astype(BF16)
    lo = (r1 - mid.astype(F32)).astype(BF16)
    return hi, mid, lo


def _bias_placement():
    pq = np.zeros((LANES, FOX_HEADS * HEAD_PAD), np.float32)
    pk = np.zeros_like(pq)
    ones_q = np.zeros((1, FOX_HEADS * HEAD_PAD), np.float32)
    ones_k = np.zeros_like(ones_q)
    for h in range(FOX_HEADS):
        for piece in range(3):
            pq[piece * FOX_HEADS + h, h * HEAD_PAD + FOX_HEAD_DIM + piece] = 1.0
            pk[piece * FOX_HEADS + h, h * HEAD_PAD + FOX_HEAD_DIM + 3 + piece] = -1.0
            ones_q[0, h * HEAD_PAD + FOX_HEAD_DIM + 3 + piece] = 1.0
            ones_k[0, h * HEAD_PAD + FOX_HEAD_DIM + piece] = 1.0
    return (jnp.asarray(pq, BF16), jnp.asarray(pk, BF16), jnp.asarray(ones_q), jnp.asarray(ones_k))


def _fcum_chunk(zf_ref, bf_ref, v_ref, tril_ref, pq_ref, pk_ref, oq_ref, ok_ref, qadd_ref, kadd_ref, vt_ref,
                carry_ref, batches):
    tril = tril_ref[...]
    lane = lax.broadcasted_iota(jnp.int32, (T_SCAN, LANES), 1)
    ones_row = lax.broadcasted_iota(jnp.int32, (HEAD_PAD, T_SCAN), 0) == FOX_HEAD_DIM
    for b in batches:
        for n in range(FOX_HEADS):
            v_t = v_ref[b, :, n * HEAD_PAD:(n + 1) * HEAD_PAD].astype(F32).T
            vt_ref[b, n * HEAD_PAD:(n + 1) * HEAD_PAD, :] = jnp.where(ones_row, 1.0, v_t).astype(BF16)
        logit = zf_ref[b] + bf_ref[...]
        log_f = -_softplus(-logit)
        hi, mid, lo = _split3(log_f)
        cs = (jnp.dot(tril, hi, preferred_element_type=F32)
              + jnp.dot(tril, mid, preferred_element_type=F32)
              + jnp.dot(tril, lo, preferred_element_type=F32))
        cum = cs + carry_ref[b]
        carry_ref[b] = cum[T_SCAN - 1:T_SCAN, :]
        hi, mid, lo = [p.astype(F32) for p in _split3(cum * LOG2E)]
        pieces = jnp.where(lane < FOX_HEADS, hi,
                           jnp.where(lane < 2 * FOX_HEADS, pltpu.roll(mid, FOX_HEADS, axis=1),
                                     jnp.where(lane < 3 * FOX_HEADS, pltpu.roll(lo, 2 * FOX_HEADS, axis=1), 0.0))
                           ).astype(BF16)
        qadd_ref[b] = (jnp.dot(pieces, pq_ref[...], preferred_element_type=F32) + oq_ref[...]).astype(BF16)
        kadd_ref[b] = (jnp.dot(pieces, pk_ref[...], preferred_element_type=F32) + ok_ref[...]).astype(BF16)


def _s5_load(z_ref, bbd_ref, s_ref, batches):
    for b in batches:
        u = z_ref[b, :, 0:MIXER_WIDTH]
        bu = jnp.dot(u.astype(BF16), bbd_ref[...], preferred_element_type=F32)
        for j in range(S5_ROWS):
            s_ref[b, pl.ds(j, T_SCAN, stride=S5_PITCH), :] = bu[:, j * LANES:(j + 1) * LANES]


def _s5_step(t, carry, lam_re, lam_im, s_ref):
    base = t * S5_PITCH
    new = []
    for b in range(BATCH):
        h_re, h_im = carry[2 * b], carry[2 * b + 1]
        n_re = lam_re * h_re - lam_im * h_im + s_ref[b, pl.ds(base, SUBLANES), :]
        n_im = lam_re * h_im + lam_im * h_re + s_ref[b, pl.ds(base + SUBLANES, SUBLANES), :]
        s_ref[b, pl.ds(base, SUBLANES), :] = n_re
        s_ref[b, pl.ds(base + SUBLANES, SUBLANES), :] = n_im
        new += [n_re, n_im]
    return new


def _s5_out(z_ref, cbd_ref, d_ref, gw_ref, gb_ref, o_ref, s_ref):
    states = jnp.concatenate(
        [jnp.concatenate([s_ref[b, pl.ds(j, T_SCAN, stride=S5_PITCH), :].astype(BF16) for j in range(S5_ROWS)],
                         axis=1) for b in range(BATCH)], axis=0)
    u = z_ref[:, :, 0:MIXER_WIDTH].reshape(BATCH * T_SCAN, MIXER_WIDTH)
    y = jnp.dot(states, cbd_ref[...], preferred_element_type=F32) + d_ref[...] * u
    y = jax.nn.gelu(y)
    gate = jnp.dot(y.astype(BF16), gw_ref[...], preferred_element_type=F32) + gb_ref[...]
    o_ref[...] = (y * jax.nn.sigmoid(gate)).astype(MIX_OUT).reshape(BATCH, T_SCAN, MIXER_WIDTH)


LRU_SLOTS = BATCH * MIXER_WIDTH // LANES
LRU_PITCH = LRU_SLOTS + SCAN_ROW_GAP
LRU_TILES = MIXER_WIDTH // LANES


def _lru_load(z_ref, cw_ref, cb_ref, wa_ref, ba_ref, wx_ref, bx_ref, lam_ref, tail_ref, a_ref, b_ref, batches):
    decay_rate = LRU_C * _softplus(-lam_ref[...])
    for b in batches:
        x = z_ref[b, :, MIXER_WIDTH:2 * MIXER_WIDTH]
        xp = jnp.concatenate([tail_ref[b], x], axis=0)
        tail_ref[b] = x[T_SCAN - SUBLANES:T_SCAN, :]
        xc = cb_ref[...]
        for k in range(LRU_CONV):
            off = SUBLANES - (LRU_CONV - 1) + k
            xc = xc + cw_ref[k:k + 1, :] * xp[off:off + T_SCAN, :]
        xcb = xc.astype(BF16)
        r = jax.nn.sigmoid(jnp.dot(xcb, wa_ref[...], preferred_element_type=F32) + ba_ref[...])
        i = jax.nn.sigmoid(jnp.dot(xcb, wx_ref[...], preferred_element_type=F32) + bx_ref[...])
        log_a = -(r * decay_rate)
        a = jnp.exp(log_a)
        inp = jnp.sqrt(-jnp.tanh(log_a) * (a * a + 1.0)) * (i * xc)
        for j in range(LRU_TILES):
            slot = b * LRU_TILES + j
            a_ref[pl.ds(slot, T_SCAN, stride=LRU_PITCH), :] = a[:, j * LANES:(j + 1) * LANES]
            b_ref[pl.ds(slot, T_SCAN, stride=LRU_PITCH), :] = inp[:, j * LANES:(j + 1) * LANES]


def _lru_step(t, h, a_ref, b_ref):
    base = t * LRU_PITCH
    h = a_ref[pl.ds(base, LRU_SLOTS), :] * h + b_ref[pl.ds(base, LRU_SLOTS), :]
    b_ref[pl.ds(base, LRU_SLOTS), :] = h
    return h


def _lru_out(z_ref, o_ref, b_ref):
    for b in range(BATCH):
        h = jnp.concatenate(
            [b_ref[pl.ds(b * LRU_TILES + j, T_SCAN, stride=LRU_PITCH), :] for j in range(LRU_TILES)], axis=1)
        o_ref[b] = (h * jax.nn.gelu(z_ref[b, :, 2 * MIXER_WIDTH:3 * MIXER_WIDTH])).astype(MIX_OUT)


def _mixers_kernel(z_ref, zf_ref, v_ref, pv_ref,
                   bbd_ref, lam5_ref, cbd_ref, gw_ref, cw_ref, wa_ref, wx_ref,
                   tril_ref, pq_ref, pk_ref, oq_ref, ok_ref,
                   yb_ref, yc_ref, qadd_ref, kadd_ref, vt_ref,
                   s_ref, h5_ref, tail_ref, a_ref, b_ref, hl_ref, carry_ref):
    w = MIXER_WIDTH
    d_ref, gb_ref, cb_ref, ba_ref, bx_ref, lam_ref = [pv_ref.at[:, k * w:(k + 1) * w] for k in range(6)]
    bf_ref = pv_ref.at[:, 6 * w:6 * w + LANES]
    @pl.when(pl.program_id(0) == 0)
    def _():
        h5_ref[...] = jnp.zeros_like(h5_ref)
        tail_ref[...] = jnp.zeros_like(tail_ref)
        hl_ref[...] = jnp.zeros_like(hl_ref)
        carry_ref[...] = jnp.zeros_like(carry_ref)

    for b in range(BATCH):
        _s5_load(z_ref, bbd_ref, s_ref, (b,))
        _lru_load(z_ref, cw_ref, cb_ref, wa_ref, ba_ref, wx_ref, bx_ref, lam_ref, tail_ref, a_ref, b_ref, (b,))
        _fcum_chunk(zf_ref, bf_ref, v_ref, tril_ref, pq_ref, pk_ref, oq_ref, ok_ref, qadd_ref, kadd_ref, vt_ref,
                    carry_ref, (b,))

    lam_re = lam5_ref[0:SUBLANES, :]
    lam_im = lam5_ref[SUBLANES:S5_ROWS, :]

    def step(t, carry):
        new5 = _s5_step(t, carry[:-1], lam_re, lam_im, s_ref)
        return tuple(new5) + (_lru_step(t, carry[-1], a_ref, b_ref),)

    init = []
    for b in range(BATCH):
        init += [h5_ref[b, 0:SUBLANES, :], h5_ref[b, SUBLANES:S5_ROWS, :]]
    fin = lax.fori_loop(0, T_SCAN, step, tuple(init) + (hl_ref[...],), unroll=8)
    for b in range(BATCH):
        h5_ref[b, 0:SUBLANES, :] = fin[2 * b]
        h5_ref[b, SUBLANES:S5_ROWS, :] = fin[2 * b + 1]
    hl_ref[...] = fin[-1]

    _s5_out(z_ref, cbd_ref, d_ref, gw_ref, gb_ref, yb_ref, s_ref)
    _lru_out(z_ref, yc_ref, b_ref)


MIXER_VEC_COLS = 6 * MIXER_WIDTH + LANES


def _mixers(zmix3, zf3, qkv3, vecs, s5_params, lru_params, layer):
    bbd, lam5, cbd, gw = s5_params
    cw, wa, wx = lru_params
    pq, pk, ones_q, ones_k = _bias_placement()
    tril = jnp.asarray(np.tril(np.ones((T_SCAN, T_SCAN), np.float32)), BF16)
    full = lambda c: (0, 0)
    chunk = lambda c: (0, c, 0)
    wide = FOX_HEADS * HEAD_PAD
    mat = _layer_block((MIXER_WIDTH, MIXER_WIDTH), layer)
    mix_out = pl.BlockSpec((BATCH, T_SCAN, MIXER_WIDTH), chunk)
    bias_out = pl.BlockSpec((BATCH, T_SCAN, wide), chunk)
    return pl.pallas_call(
        _mixers_kernel,
        grid=(SEQ // T_SCAN,),
        in_specs=[pl.BlockSpec((BATCH, T_SCAN, ZMIX_COLS), chunk),
                  pl.BlockSpec((BATCH, T_SCAN, LANES), chunk),
                  pl.BlockSpec((BATCH, T_SCAN, wide), lambda c: (0, c, 2)),
                  _layer_block((1, MIXER_VEC_COLS), layer),
                  _layer_block((MIXER_WIDTH, 2 * S5_NSTATE), layer),
                  _layer_block((S5_ROWS, LANES), layer),
                  _layer_block((2 * S5_NSTATE, MIXER_WIDTH), layer),
                  mat,
                  _layer_block((LRU_CONV, MIXER_WIDTH), layer), mat, mat,
                  pl.BlockSpec((T_SCAN, T_SCAN), full),
                  pl.BlockSpec((LANES, wide), full),
                  pl.BlockSpec((LANES, wide), full),
                  pl.BlockSpec((1, wide), full),
                  pl.BlockSpec((1, wide), full)],
        out_specs=[mix_out, mix_out, bias_out, bias_out,
                   pl.BlockSpec((BATCH, wide, T_SCAN), lambda c: (0, 0, c))],
        out_shape=[jax.ShapeDtypeStruct((BATCH, SEQ, MIXER_WIDTH), MIX_OUT),
                   jax.ShapeDtypeStruct((BATCH, SEQ, MIXER_WIDTH), MIX_OUT),
                   jax.ShapeDtypeStruct((BATCH, SEQ, wide), BF16),
                   jax.ShapeDtypeStruct((BATCH, SEQ, wide), BF16),
                   jax.ShapeDtypeStruct((BATCH, wide, SEQ), BF16)],
        scratch_shapes=[pltpu.VMEM((BATCH, T_SCAN * S5_PITCH, LANES), F32),
                        pltpu.VMEM((BATCH, S5_ROWS, LANES), F32),
                        pltpu.VMEM((BATCH, SUBLANES, MIXER_WIDTH), F32),
                        pltpu.VMEM((T_SCAN * LRU_PITCH, LANES), F32),
                        pltpu.VMEM((T_SCAN * LRU_PITCH, LANES), F32),
                        pltpu.VMEM((LRU_SLOTS, LANES), F32),
                        pltpu.VMEM((BATCH, 1, LANES), F32)],
        compiler_params=_params(1, VMEM_LIMIT_MIXERS),
        name="recurrent_mixers",
    )(zmix3, zf3, qkv3, vecs, bbd, lam5, cbd, gw, cw, wa, wx, tril, pq, pk, ones_q, ones_k)


def _attn_kernel(q_ref, qadd_ref, k_ref, kadd_ref, vt_ref, w1_ref, w2_ref, o_ref, w1o_ref, w2o_ref,
                 s_ref, m_ref, acc_ref):
    i = pl.program_id(2)
    scale = FOX_HEAD_DIM ** -0.5 * LOG2E
    neg = jnp.finfo(F32).min
    slots = [slice(n * HEAD_PAD, (n + 1) * HEAD_PAD) for n in range(FOX_HEADS)]
    qs = [(q_ref[:, sl].astype(F32) * scale + qadd_ref[:, sl].astype(F32)).astype(BF16) for sl in slots]

    def logits_t(n, j):
        start = pl.multiple_of(j * TK, TK)
        ks = k_ref[pl.ds(start, TK), slots[n]] + kadd_ref[pl.ds(start, TK), slots[n]]
        return lax.dot_general(ks, qs[n], (((1,), (1,)), ((), ())), preferred_element_type=F32)

    def block(j, masked, prefetch):
        start = pl.multiple_of(j * TK, TK)
        for n in range(FOX_HEADS):
            s = s_ref[n]
            if masked:
                key = lax.broadcasted_iota(jnp.int32, (TK, TQ), 0)
                qry = lax.broadcasted_iota(jnp.int32, (TK, TQ), 1)
                s = jnp.where(key <= qry, s, neg)
            m = m_ref[n]
            m_new = jnp.maximum(m, jnp.max(s, axis=0, keepdims=True))
            alpha = jnp.exp2(m - m_new)
            p = jnp.exp2(s - jnp.concatenate([m_new] * (TK // SUBLANES), axis=0))
            m_ref[n] = m_new
            vt = vt_ref[0, n * HEAD_PAD:n * HEAD_PAD + ACC_ROWS, pl.ds(start, TK)]
            acc_ref[n] = (jnp.concatenate([alpha] * (ACC_ROWS // SUBLANES), axis=0) * acc_ref[n]
                          + jnp.dot(vt, p.astype(BF16), preferred_element_type=F32))
            if prefetch:
                s_ref[n] = logits_t(n, j + 1)

    for n in range(FOX_HEADS):
        s_ref[n] = logits_t(n, 0)
    m_ref[...] = jnp.full(m_ref.shape, neg, F32)
    acc_ref[...] = jnp.zeros_like(acc_ref)

    @pl.loop(0, i // 2)
    def _(jj):
        block(2 * jj, False, True)
        block(2 * jj + 1, False, True)

    @pl.when(i % 2 == 1)
    def _():
        block(i - 1, False, True)

    block(i, True, False)
    for n, sl in enumerate(slots):
        acc = acc_ref[n]
        out_t = acc[0:FOX_HEAD_DIM] / acc[FOX_HEAD_DIM:FOX_HEAD_DIM + 1]
        out_t = jnp.concatenate([out_t, jnp.zeros((HEAD_PAD - FOX_HEAD_DIM, TQ), F32)], axis=0)
        o_ref[:, sl] = out_t.T.astype(MIX_OUT)
    w1o_ref[...] = w1_ref[...].astype(BF16)
    w2o_ref[...] = w2_ref[...].astype(BF16)


def _attn(qkv, qadd, kadd, vt, w1, w2, layer):
    n_q = SEQ // TQ
    n_steps = BATCH * n_q
    width = FOX_HEADS * HEAD_PAD
    qrow = lambda b, h, i: (b * n_q + i, 0)
    slab = lambda b, h, i: (b * n_q + i, 0)
    w_slab = lambda b, h, i: (layer, b * n_q + i, 0)
    return pl.pallas_call(
        _attn_kernel,
        grid=(BATCH, 1, n_q),
        in_specs=[pl.BlockSpec((TQ, width), qrow),
                  pl.BlockSpec((TQ, width), qrow),
                  pl.BlockSpec((SEQ, width), lambda b, h, i: (b, 1)),
                  pl.BlockSpec((SEQ, width), lambda b, h, i: (b, 0)),
                  pl.BlockSpec((1, width, SEQ), lambda b, h, i: (b, 0, 0)),
                  pl.BlockSpec((None, D_MODEL // n_steps, D_FF), w_slab),
                  pl.BlockSpec((None, D_FF // n_steps, D_MODEL), w_slab)],
        out_specs=[pl.BlockSpec((TQ, width), qrow),
                   pl.BlockSpec((D_MODEL // n_steps, D_FF), slab),
                   pl.BlockSpec((D_FF // n_steps, D_MODEL), slab)],
        out_shape=[jax.ShapeDtypeStruct((N_TOK, width), MIX_OUT),
                   jax.ShapeDtypeStruct((D_MODEL, D_FF), BF16),
                   jax.ShapeDtypeStruct((D_FF, D_MODEL), BF16)],
        scratch_shapes=[pltpu.VMEM((FOX_HEADS, TK, TQ), F32),
                        pltpu.VMEM((FOX_HEADS, SUBLANES, TQ), F32),
                        pltpu.VMEM((FOX_HEADS, ACC_ROWS, TQ), F32)],
        compiler_params=_params(3),
        name="fox_attention",
    )(qkv, qadd, qkv, kadd, vt, w1, w2)


def _merge_mlp_kernel(ya_ref, yb_ref, yc_ref, yd_ref, x_ref, pv_ref, wo_ref, w1_ref, w2_ref,
                      fg_ref, o_ref, h_ref, *, final_norm):
    j = pl.program_id(1)
    n_abc = 3 * MIXER_WIDTH
    n_d = FOX_HEADS * HEAD_PAD
    gm_ref = pv_ref.at[:, 0:n_abc]
    gmd_ref = pv_ref.at[:, n_abc:n_abc + n_d]
    g2_ref = pv_ref.at[:, n_abc + n_d:n_abc + n_d + D_MODEL]

    @pl.when(j == 0)
    def _():
        w = MIXER_WIDTH
        parts = [_rms(ya_ref[...].astype(F32), gm_ref[:, 0:w], w),
                 _rms(yb_ref[...].astype(F32), gm_ref[:, w:2 * w], w),
                 _rms(yc_ref[...].astype(F32), gm_ref[:, 2 * w:3 * w], w),
                 _rms(yd_ref[...].astype(F32), gmd_ref[...], w)]
        y = jnp.concatenate(parts, axis=1).astype(BF16)
        x1 = x_ref[...] + jnp.dot(y, wo_ref[...], preferred_element_type=F32)
        o_ref[...] = x1
        h_ref[...] = _rms(x1, g2_ref[...], D_MODEL).astype(BF16)

    a = jnp.dot(h_ref[...], w1_ref[...], preferred_element_type=F32)
    a = jnp.square(jnp.maximum(a, 0.0)).astype(BF16)
    o_ref[...] += jnp.dot(a, w2_ref[...], preferred_element_type=F32)

    if final_norm:
        @pl.when(j == pl.num_programs(1) - 1)
        def _():
            o_ref[...] = _rms(o_ref[...], fg_ref[...], D_MODEL)


def _merge_mlp(ya, yb, yc, yd, x, vecs, wo, w1, w2, fg, layer, final_norm):
    row = lambda i, j: (i, 0)
    k_dim = 3 * MIXER_WIDTH + FOX_HEADS * HEAD_PAD
    mix = pl.BlockSpec((TM_MLP, MIXER_WIDTH), row)
    return pl.pallas_call(
        functools.partial(_merge_mlp_kernel, final_norm=final_norm),
        grid=(N_TOK // TM_MLP, D_FF // TF_MLP),
        in_specs=[mix, mix, mix,
                  pl.BlockSpec((TM_MLP, FOX_HEADS * HEAD_PAD), row),
                  pl.BlockSpec((TM_MLP, D_MODEL), row),
                  _layer_block((1, k_dim + D_MODEL), layer),
                  _layer_block((k_dim, D_MODEL), layer),
                  pl.BlockSpec((D_MODEL, TF_MLP), lambda i, j: (0, j)),
                  pl.BlockSpec((TF_MLP, D_MODEL), lambda i, j: (j, 0)),
                  pl.BlockSpec((1, D_MODEL), lambda i, j: (0, 0))],
        out_specs=pl.BlockSpec((TM_MLP, D_MODEL), row),
        out_shape=jax.ShapeDtypeStruct((N_TOK, D_MODEL), F32),
        scratch_shapes=[pltpu.VMEM((TM_MLP, D_MODEL), BF16)],
        compiler_params=_params(2, VMEM_LIMIT_MIXERS),
        name="merge_mlp",
    )(ya, yb, yc, yd, x, vecs, wo, w1, w2, fg)


def _pad_heads(w):
    lead = w.shape[:-1]
    w = w.reshape(*lead, FOX_HEADS, FOX_HEAD_DIM)
    w = jnp.pad(w, [(0, 0)] * (len(lead) + 1) + [(0, HEAD_PAD - FOX_HEAD_DIM)])
    return w.reshape(*lead, FOX_HEADS * HEAD_PAD)


def _block_diag(blocks):
    n, g, r, c = blocks.shape
    tiled = jnp.tile(blocks.reshape(n, g * r, c), (1, 1, g))
    on_diag = (np.arange(g * r)[:, None] // r) == (np.arange(g * c)[None, :] // c)
    return jnp.where(jnp.asarray(on_diag), tiled, 0.0)


def _s5_discretize(lam_re, lam_im, log_dt, b_re, b_im):
    dt = jnp.exp(log_dt)[..., None]
    mag = jnp.exp(lam_re * dt)
    abar_re = mag * jnp.cos(lam_im * dt)
    abar_im = mag * jnp.sin(lam_im * dt)
    denom = jnp.square(lam_re) + jnp.square(lam_im)
    num_re = abar_re - 1.0
    num_im = abar_im
    fac_re = (num_re * lam_re + num_im * lam_im) / denom
    fac_im = (num_im * lam_re - num_re * lam_im) / denom
    bbar_re = fac_re[..., None] * b_re - fac_im[..., None] * b_im
    bbar_im = fac_re[..., None] * b_im + fac_im[..., None] * b_re
    return abar_re, abar_im, bbar_re, bbar_im


def kernel(x, norm1_g, w_in, sgu_norm_g, sgu_w, sgu_b, s5_lambda_re, s5_lambda_im, s5_log_dt, s5_b_re, s5_b_im, s5_c_re, s5_c_im, s5_d, s5_glu_w, s5_glu_b, lru_conv_w, lru_conv_b, lru_wa, lru_ba, lru_wx, lru_bx, lru_lambda, fox_fgate_b, mix_norm_g, w_out, norm2_g, w_mlp_in, w_mlp_out, final_g):
    w = MIXER_WIDTH
    row = lambda v: v.reshape(DEPTH, 1, -1)

    vec_in = row(jnp.concatenate([norm1_g, sgu_norm_g], axis=-1))
    vec_mix = row(jnp.concatenate(
        [s5_d, s5_glu_b, lru_conv_b, lru_ba.reshape(DEPTH, w), lru_bx.reshape(DEPTH, w), lru_lambda,
         jnp.pad(fox_fgate_b, ((0, 0), (0, LANES - FOX_HEADS)))], axis=-1))
    vec_mlp = row(jnp.concatenate(
        [mix_norm_g[:, 0:3 * w], _pad_heads(mix_norm_g[:, 3 * w:]), norm2_g], axis=-1))

    sgu_wcat = jnp.transpose(sgu_w, (0, 2, 1, 3)).reshape(DEPTH, SGU_CHUNK, SGU_HEADS * SGU_CHUNK)
    sgu_bias = jnp.repeat(jnp.transpose(sgu_b, (0, 2, 1)), w // SGU_HEADS, axis=2)

    abar_re, abar_im, bbar_re, bbar_im = _s5_discretize(s5_lambda_re, s5_lambda_im, s5_log_dt, s5_b_re, s5_b_im)
    swap = lambda t: jnp.transpose(t, (0, 1, 3, 2))
    bbd = jnp.concatenate([_block_diag(swap(bbar_re)), _block_diag(swap(bbar_im))], axis=2).astype(BF16)
    cbd = jnp.concatenate([_block_diag(swap(s5_c_re)), -_block_diag(swap(s5_c_im))], axis=1).astype(BF16)
    lam = jnp.concatenate([abar_re.reshape(DEPTH, SUBLANES, LANES), abar_im.reshape(DEPTH, SUBLANES, LANES)], axis=1)
    glu_w = s5_glu_w.astype(BF16)

    wa_bd = _block_diag(lru_wa).astype(BF16)
    wx_bd = _block_diag(lru_wx).astype(BF16)

    w_o_pad = jnp.concatenate(
        [w_out[:, 0:3 * w],
         jnp.pad(w_out[:, 3 * w:].reshape(DEPTH, FOX_HEADS, FOX_HEAD_DIM, D_MODEL),
                 ((0, 0), (0, 0), (0, HEAD_PAD - FOX_HEAD_DIM), (0, 0))).reshape(DEPTH, FOX_HEADS * HEAD_PAD, D_MODEL)],
        axis=1).astype(BF16)
    fg = final_g.reshape(1, D_MODEL)

    xf = x.reshape(N_TOK, D_MODEL)
    for l in range(DEPTH):
        y_a, zmix, qkv, zf = _in_proj(xf, vec_in, w_in, sgu_wcat, sgu_bias, l)
        zmix3 = zmix.reshape(BATCH, SEQ, ZMIX_COLS)
        y_b, y_c, qadd, kadd, vt = _mixers(
            zmix3, zf.reshape(BATCH, SEQ, LANES), qkv.reshape(BATCH, SEQ, QKV_COLS), vec_mix,
            (bbd, lam, cbd, glu_w), (lru_conv_w, wa_bd, wx_bd), l)
        qadd = qadd.reshape(N_TOK, FOX_HEADS * HEAD_PAD)
        kadd = kadd.reshape(N_TOK, FOX_HEADS * HEAD_PAD)
        y_d, w1, w2 = _attn(qkv, qadd, kadd, vt, w_mlp_in, w_mlp_out, l)
        xf = _merge_mlp(y_a, y_b.reshape(N_TOK, w), y_c.reshape(N_TOK, w), y_d, xf, vec_mlp, w_o_pad,
                        w1, w2, fg, l, final_norm=(l == DEPTH - 1))
    return xf.reshape(BATCH, SEQ, D_MODEL)
```

```python
import functools

import jax
import jax.numpy as jnp
import numpy as np
from jax import lax
from jax.experimental import pallas as pl
from jax.experimental.pallas import tpu as pltpu

D_MODEL = 1024
BATCH = 4
SEQ = 4096
DEPTH = 4
N_TOK = BATCH * SEQ
MIXER_WIDTH = 256
SGU_HEADS = 4
SGU_CHUNK = 128
S5_GROUPS = 16
S5_STATE = 64
LRU_CONV = 4
LRU_C = 8.0
FOX_HEADS = 4
FOX_HEAD_DIM = 64
D_FF = 4 * D_MODEL
RMS_EPS = 1e-6
LOG2E = 1.4426950408889634

LANES = 128
SUBLANES = 8
HEAD_PAD = LANES
ZMIX_COLS = 3 * MIXER_WIDTH
QKV_COLS = 3 * FOX_HEADS * HEAD_PAD
W_IN_RAW = 8 * MIXER_WIDTH + FOX_HEADS
S5_NSTATE = S5_GROUPS * S5_STATE
S5_ROWS = 2 * S5_NSTATE // LANES
SCAN_ROW_GAP = 4
S5_PITCH = S5_ROWS + SCAN_ROW_GAP

TM_IN = 1024
TM_MLP = 1024
TF_MLP = 2048
T_SCAN = 256
TQ = 512
TK = 512
ACC_ROWS = 80
MIB = 1024 * 1024
VMEM_LIMIT = 48 * MIB
VMEM_LIMIT_LARGE = 56 * MIB

F32 = jnp.float32
BF16 = jnp.bfloat16
MIX_OUT = BF16


def _params(n_axes, vmem_limit=VMEM_LIMIT):
    return pltpu.CompilerParams(dimension_semantics=("arbitrary",) * n_axes,
                                vmem_limit_bytes=vmem_limit)


def _layer_block(shape, layer):
    zeros = (0,) * len(shape)
    return pl.BlockSpec((None,) + tuple(shape), lambda *_: (layer,) + zeros)


def _rms(x, g, width):
    ms = jnp.sum(jnp.square(x), axis=-1, keepdims=True) * (1.0 / width)
    return x * lax.rsqrt(ms + RMS_EPS) * g


def _softplus(x):
    return jnp.maximum(x, 0.0) + jnp.log1p(jnp.exp(-jnp.abs(x)))


def _sgu_mix(zu, zv, g, w, bias, o_ref):
    n_rows = zu.shape[0]
    u = jax.nn.gelu(zu)
    v = _rms(jax.nn.gelu(zv), g, MIXER_WIDTH)
    hd = MIXER_WIDTH // SGU_HEADS
    lane_head = lax.broadcasted_iota(jnp.int32, (SGU_CHUNK, MIXER_WIDTH), 1) // hd
    t_idx = lax.broadcasted_iota(jnp.int32, (SGU_CHUNK, SGU_HEADS * SGU_CHUNK), 0)
    s_idx = lax.broadcasted_iota(jnp.int32, (SGU_CHUNK, SGU_HEADS * SGU_CHUNK), 1) % SGU_CHUNK
    wm = jnp.where(s_idx <= t_idx, w, 0.0).astype(BF16)
    for c in range(n_rows // SGU_CHUNK):
        rows = slice(c * SGU_CHUNK, (c + 1) * SGU_CHUNK)
        vc = v[rows]
        vstack = jnp.concatenate(
            [jnp.where(lane_head == h, vc, 0.0) for h in range(SGU_HEADS)], axis=0).astype(BF16)
        mixed = jnp.dot(wm, vstack, preferred_element_type=F32) + bias
        o_ref[rows, :] = (u[rows] * mixed).astype(MIX_OUT)


def _in_proj_kernel(x_ref, pv_ref, w_ref, sw_ref, sb_ref, ya_ref, zmix_ref, qkv_ref, zf_ref, wb_ref):
    w = MIXER_WIDTH

    @pl.when(pl.program_id(0) == 0)
    def _():
        wb_ref[:, 0:8 * w] = w_ref[:, 0:8 * w].astype(BF16)
        forget = jnp.concatenate([w_ref[:, 8 * w:W_IN_RAW], jnp.zeros((D_MODEL, LANES - FOX_HEADS), F32)], axis=1)
        wb_ref[:, 8 * w:8 * w + LANES] = forget.astype(BF16)

    h = _rms(x_ref[...], pv_ref[:, 0:D_MODEL], D_MODEL).astype(BF16)
    z5 = jnp.dot(h, wb_ref[:, 0:5 * w], preferred_element_type=F32)
    zmix_ref[...] = z5[:, 2 * w:5 * w]
    _sgu_mix(z5[:, 0:w], z5[:, w:2 * w], pv_ref[:, D_MODEL:D_MODEL + w], sw_ref[...], sb_ref[...], ya_ref)
    zqkvf = jnp.dot(h, wb_ref[:, 5 * w:8 * w + LANES], preferred_element_type=F32)
    zf_ref[...] = zqkvf[:, 3 * w:]
    pad = jnp.zeros((TM_IN, HEAD_PAD - FOX_HEAD_DIM), F32)
    for slot in range(3 * FOX_HEADS):
        head = zqkvf[:, slot * FOX_HEAD_DIM:(slot + 1) * FOX_HEAD_DIM]
        qkv_ref[:, slot * HEAD_PAD:(slot + 1) * HEAD_PAD] = jnp.concatenate([head, pad], axis=1).astype(BF16)


def _in_proj(x, vecs, w, sgu_w, sgu_b, layer):
    row = lambda i: (i, 0)
    return pl.pallas_call(
        _in_proj_kernel,
        grid=(N_TOK // TM_IN,),
        in_specs=[pl.BlockSpec((TM_IN, D_MODEL), row),
                  _layer_block((1, D_MODEL + MIXER_WIDTH), layer),
                  _layer_block((D_MODEL, W_IN_RAW), layer),
                  _layer_block((SGU_CHUNK, SGU_HEADS * SGU_CHUNK), layer),
                  _layer_block((SGU_CHUNK, MIXER_WIDTH), layer)],
        out_specs=[pl.BlockSpec((TM_IN, MIXER_WIDTH), row),
                   pl.BlockSpec((TM_IN, ZMIX_COLS), row),
                   pl.BlockSpec((TM_IN, QKV_COLS), row),
                   pl.BlockSpec((TM_IN, LANES), row)],
        out_shape=[jax.ShapeDtypeStruct((N_TOK, MIXER_WIDTH), MIX_OUT),
                   jax.ShapeDtypeStruct((N_TOK, ZMIX_COLS), F32),
                   jax.ShapeDtypeStruct((N_TOK, QKV_COLS), BF16),
                   jax.ShapeDtypeStruct((N_TOK, LANES), F32)],
        scratch_shapes=[pltpu.VMEM((D_MODEL, 8 * MIXER_WIDTH + LANES), BF16)],
        compiler_params=_params(1, VMEM_LIMIT_LARGE),
        name="in_proj_sgu",
    )(x, vecs, w, sgu_w, sgu_b)


def _split3(x):
    hi = x.astype(BF16)
    r1 = x - hi.astype(F32)
    mid = r1.astype(BF16)
    lo = (r1 - mid.astype(F32)).astype(BF16)
    return hi, mid, lo


def _bias_placement():
    pq = np.zeros((LANES, FOX_HEADS * HEAD_PAD), np.float32)
    pk = np.zeros_like(pq)
    ones_q = np.zeros((1, FOX_HEADS * HEAD_PAD), np.float32)
    ones_k = np.zeros_like(ones_q)
    for h in range(FOX_HEADS):
        for piece in range(3):
            pq[piece * FOX_HEADS + h, h * HEAD_PAD + FOX_HEAD_DIM + piece] = 1.0
            pk[piece * FOX_HEADS + h, h * HEAD_PAD + FOX_HEAD_DIM + 3 + piece] = -1.0
            ones_q[0, h * HEAD_PAD + FOX_HEAD_DIM + 3 + piece] = 1.0
            ones_k[0, h * HEAD_PAD + FOX_HEAD_DIM + piece] = 1.0
    return (jnp.asarray(pq, BF16), jnp.asarray(pk, BF16), jnp.asarray(ones_q), jnp.asarray(ones_k))


def _fcum_chunk(zf_ref, bf_ref, v_ref, tril_ref, pq_ref, pk_ref, oq_ref, ok_ref, qadd_ref, kadd_ref, vt_ref,
                carry_ref, batches):
    tril = tril_ref[...]
    lane = lax.broadcasted_iota(jnp.int32, (T_SCAN, LANES), 1)
    ones_row = lax.broadcasted_iota(jnp.int32, (HEAD_PAD, T_SCAN), 0) == FOX_HEAD_DIM
    for b in batches:
        for n in range(FOX_HEADS):
            v_t = v_ref[b, :, n * HEAD_PAD:(n + 1) * HEAD_PAD].astype(F32).T
            vt_ref[b, n * HEAD_PAD:(n + 1) * HEAD_PAD, :] = jnp.where(ones_row, 1.0, v_t).astype(BF16)
        logit = zf_ref[b] + bf_ref[...]
        log_f = -_softplus(-logit)
        hi, mid, lo = _split3(log_f)
        cs = (jnp.dot(tril, hi, preferred_element_type=F32)
              + jnp.dot(tril, mid, preferred_element_type=F32)
              + jnp.dot(tril, lo, preferred_element_type=F32))
        cum = cs + carry_ref[b]
        carry_ref[b] = cum[T_SCAN - 1:T_SCAN, :]
        hi, mid, lo = [p.astype(F32) for p in _split3(cum * LOG2E)]
        pieces = jnp.where(lane < FOX_HEADS, hi,
                           jnp.where(lane < 2 * FOX_HEADS, pltpu.roll(mid, FOX_HEADS, axis=1),
                                     jnp.where(lane < 3 * FOX_HEADS, pltpu.roll(lo, 2 * FOX_HEADS, axis=1), 0.0))
                           ).astype(BF16)
        qadd_ref[b] = (jnp.dot(pieces, pq_ref[...], preferred_element_type=F32) + oq_ref[...]).astype(BF16)
        kadd_ref[b] = (jnp.dot(pieces, pk_ref[...], preferred_element_type=F32) + ok_ref[...]).astype(BF16)


def _s5_load(z_ref, bbd_ref, s_ref, batches):
    for b in batches:
        u = z_ref[b, :, 0:MIXER_WIDTH]
        bu = jnp.dot(u.astype(BF16), bbd_ref[...], preferred_element_type=F32)
        for j in range(S5_ROWS):
            s_ref[b, pl.ds(j, T_SCAN, stride=S5_PITCH), :] = bu[:, j * LANES:(j + 1) * LANES]


def _s5_step(t, carry, lam_re, lam_im, s_ref):
    base = t * S5_PITCH
    new = []
    for b in range(BATCH):
        h_re, h_im = carry[2 * b], carry[2 * b + 1]
        n_re = lam_re * h_re - lam_im * h_im + s_ref[b, pl.ds(base, SUBLANES), :]
        n_im = lam_re * h_im + lam_im * h_re + s_ref[b, pl.ds(base + SUBLANES, SUBLANES), :]
        s_ref[b, pl.ds(base, SUBLANES), :] = n_re
        s_ref[b, pl.ds(base + SUBLANES, SUBLANES), :] = n_im
        new += [n_re, n_im]
    return new


def _s5_out(z_ref, cbd_ref, d_ref, gw_ref, gb_ref, o_ref, s_ref):
    states = jnp.concatenate(
        [jnp.concatenate([s_ref[b, pl.ds(j, T_SCAN, stride=S5_PITCH), :].astype(BF16) for j in range(S5_ROWS)],
                         axis=1) for b in range(BATCH)], axis=0)
    u = z_ref[:, :, 0:MIXER_WIDTH].reshape(BATCH * T_SCAN, MIXER_WIDTH)
    y = jnp.dot(states, cbd_ref[...], preferred_element_type=F32) + d_ref[...] * u
    y = jax.nn.gelu(y)
    gate = jnp.dot(y.astype(BF16), gw_ref[...], preferred_element_type=F32) + gb_ref[...]
    o_ref[...] = (y * jax.nn.sigmoid(gate)).astype(MIX_OUT).reshape(BATCH, T_SCAN, MIXER_WIDTH)


LRU_SLOTS = BATCH * MIXER_WIDTH // LANES
LRU_PITCH = LRU_SLOTS + SCAN_ROW_GAP
LRU_TILES = MIXER_WIDTH // LANES


def _lru_load(z_ref, cw_ref, cb_ref, wa_ref, ba_ref, wx_ref, bx_ref, lam_ref, tail_ref, a_ref, b_ref, batches):
    decay_rate = LRU_C * _softplus(-lam_ref[...])
    for b in batches:
        x = z_ref[b, :, MIXER_WIDTH:2 * MIXER_WIDTH]
        xp = jnp.concatenate([tail_ref[b], x], axis=0)
        tail_ref[b] = x[T_SCAN - SUBLANES:T_SCAN, :]
        xc = cb_ref[...]
        for k in range(LRU_CONV):
            off = SUBLANES - (LRU_CONV - 1) + k
            xc = xc + cw_ref[k:k + 1, :] * xp[off:off + T_SCAN, :]
        xcb = xc.astype(BF16)
        r = jax.nn.sigmoid(jnp.dot(xcb, wa_ref[...], preferred_element_type=F32) + ba_ref[...])
        i = jax.nn.sigmoid(jnp.dot(xcb, wx_ref[...], preferred_element_type=F32) + bx_ref[...])
        log_a = -(r * decay_rate)
        a = jnp.exp(log_a)
        inp = jnp.sqrt(-jnp.tanh(log_a) * (a * a + 1.0)) * (i * xc)
        for j in range(LRU_TILES):
            slot = b * LRU_TILES + j
            a_ref[pl.ds(slot, T_SCAN, stride=LRU_PITCH), :] = a[:, j * LANES:(j + 1) * LANES]
            b_ref[pl.ds(slot, T_SCAN, stride=LRU_PITCH), :] = inp[:, j * LANES:(j + 1) * LANES]


def _lru_step(t, h, a_ref, b_ref):
    base = t * LRU_PITCH
    h = a_ref[pl.ds(base, LRU_SLOTS), :] * h + b_ref[pl.ds(base, LRU_SLOTS), :]
    b_ref[pl.ds(base, LRU_SLOTS), :] = h
    return h


def _lru_out(z_ref, o_ref, b_ref):
    for b in range(BATCH):
        h = jnp.concatenate(
            [b_ref[pl.ds(b * LRU_TILES + j, T_SCAN, stride=LRU_PITCH), :] for j in range(LRU_TILES)], axis=1)
        o_ref[b] = (h * jax.nn.gelu(z_ref[b, :, 2 * MIXER_WIDTH:3 * MIXER_WIDTH])).astype(MIX_OUT)


def _s5_zoh(lam_re, lam_im, log_dt):
    dt = jnp.exp(log_dt)
    mag = jnp.exp(lam_re * dt)
    abar_re = mag * jnp.cos(lam_im * dt)
    abar_im = mag * jnp.sin(lam_im * dt)
    denom = jnp.square(lam_re) + jnp.square(lam_im)
    num_re = abar_re - 1.0
    num_im = abar_im
    fac_re = (num_re * lam_re + num_im * lam_im) / denom
    fac_im = (num_im * lam_re - num_re * lam_im) / denom
    return abar_re, abar_im, fac_re, fac_im


def _mixers_kernel(z_ref, zf_ref, v_ref, pv_ref,
                   s5row_ref, s5tile_ref, bre_ref, bim_ref, cbd_ref, gw_ref, cw_ref, wa_ref, wx_ref,
                   tril_ref, pq_ref, pk_ref, oq_ref, ok_ref,
                   yb_ref, yc_ref, qadd_ref, kadd_ref, vt_ref,
                   s_ref, h5_ref, tail_ref, a_ref, b_ref, hl_ref, carry_ref, bbd_ref, lam5_ref):
    w = MIXER_WIDTH
    d_ref, gb_ref, cb_ref, ba_ref, bx_ref, lam_ref = [pv_ref.at[:, k * w:(k + 1) * w] for k in range(6)]
    bf_ref = pv_ref.at[:, 6 * w:6 * w + LANES]

    @pl.when(pl.program_id(0) == 0)
    def _():
        h5_ref[...] = jnp.zeros_like(h5_ref)
        tail_ref[...] = jnp.zeros_like(tail_ref)
        hl_ref[...] = jnp.zeros_like(hl_ref)
        carry_ref[...] = jnp.zeros_like(carry_ref)
        t = SUBLANES
        abar_re, abar_im, _, _ = _s5_zoh(s5tile_ref[0:t, :], s5tile_ref[t:2 * t, :], s5tile_ref[2 * t:3 * t, :])
        lam5_ref[0:t, :] = abar_re
        lam5_ref[t:2 * t, :] = abar_im
        _, _, fac_re, fac_im = _s5_zoh(s5row_ref[0:1, :], s5row_ref[1:2, :], s5row_ref[2:3, :])
        b_re, b_im = bre_ref[...], bim_ref[...]
        bbd_ref[:, 0:S5_NSTATE] = (fac_re * b_re - fac_im * b_im).astype(BF16)
        bbd_ref[:, S5_NSTATE:2 * S5_NSTATE] = (fac_re * b_im + fac_im * b_re).astype(BF16)

    for b in range(BATCH):
        _s5_load(z_ref, bbd_ref, s_ref, (b,))
        _lru_load(z_ref, cw_ref, cb_ref, wa_ref, ba_ref, wx_ref, bx_ref, lam_ref, tail_ref, a_ref, b_ref, (b,))
        _fcum_chunk(zf_ref, bf_ref, v_ref, tril_ref, pq_ref, pk_ref, oq_ref, ok_ref, qadd_ref, kadd_ref, vt_ref,
                    carry_ref, (b,))

    lam_re = lam5_ref[0:SUBLANES, :]
    lam_im = lam5_ref[SUBLANES:S5_ROWS, :]

    def step(t, carry):
        new5 = _s5_step(t, carry[:-1], lam_re, lam_im, s_ref)
        return tuple(new5) + (_lru_step(t, carry[-1], a_ref, b_ref),)

    init = []
    for b in range(BATCH):
        init += [h5_ref[b, 0:SUBLANES, :], h5_ref[b, SUBLANES:S5_ROWS, :]]
    fin = lax.fori_loop(0, T_SCAN, step, tuple(init) + (hl_ref[...],), unroll=8)
    for b in range(BATCH):
        h5_ref[b, 0:SUBLANES, :] = fin[2 * b]
        h5_ref[b, SUBLANES:S5_ROWS, :] = fin[2 * b + 1]
    hl_ref[...] = fin[-1]

    _s5_out(z_ref, cbd_ref, d_ref, gw_ref, gb_ref, yb_ref, s_ref)
    _lru_out(z_ref, yc_ref, b_ref)


MIXER_VEC_COLS = 6 * MIXER_WIDTH + LANES


def _mixers(zmix3, zf3, qkv3, vecs, s5_params, lru_params, layer):
    s5_row, s5_tile, b_re, b_im, cbd, gw = s5_params
    cw, wa, wx = lru_params
    pq, pk, ones_q, ones_k = _bias_placement()
    tril = jnp.asarray(np.tril(np.ones((T_SCAN, T_SCAN), np.float32)), BF16)
    full = lambda c: (0, 0)
    chunk = lambda c: (0, c, 0)
    wide = FOX_HEADS * HEAD_PAD
    mat = _layer_block((MIXER_WIDTH, MIXER_WIDTH), layer)
    mix_out = pl.BlockSpec((BATCH, T_SCAN, MIXER_WIDTH), chunk)
    bias_out = pl.BlockSpec((BATCH, T_SCAN, wide), chunk)
    return pl.pallas_call(
        _mixers_kernel,
        grid=(SEQ // T_SCAN,),
        in_specs=[pl.BlockSpec((BATCH, T_SCAN, ZMIX_COLS), chunk),
                  pl.BlockSpec((BATCH, T_SCAN, LANES), chunk),
                  pl.BlockSpec((BATCH, T_SCAN, wide), lambda c: (0, c, 2)),
                  _layer_block((1, MIXER_VEC_COLS), layer),
                  _layer_block((3, S5_NSTATE), layer),
                  _layer_block((3 * SUBLANES, LANES), layer),
                  _layer_block((MIXER_WIDTH, S5_NSTATE), layer),
                  _layer_block((MIXER_WIDTH, S5_NSTATE), layer),
                  _layer_block((2 * S5_NSTATE, MIXER_WIDTH), layer),
                  mat,
                  _layer_block((LRU_CONV, MIXER_WIDTH), layer), mat, mat,
                  pl.BlockSpec((T_SCAN, T_SCAN), full),
                  pl.BlockSpec((LANES, wide), full),
                  pl.BlockSpec((LANES, wide), full),
                  pl.BlockSpec((1, wide), full),
                  pl.BlockSpec((1, wide), full)],
        out_specs=[mix_out, mix_out, bias_out, bias_out,
                   pl.BlockSpec((BATCH, wide, T_SCAN), lambda c: (0, 0, c))],
        out_shape=[jax.ShapeDtypeStruct((BATCH, SEQ, MIXER_WIDTH), MIX_OUT),
                   jax.ShapeDtypeStruct((BATCH, SEQ, MIXER_WIDTH), MIX_OUT),
                   jax.ShapeDtypeStruct((BATCH, SEQ, wide), BF16),
                   jax.ShapeDtypeStruct((BATCH, SEQ, wide), BF16),
                   jax.ShapeDtypeStruct((BATCH, wide, SEQ), BF16)],
        scratch_shapes=[pltpu.VMEM((BATCH, T_SCAN * S5_PITCH, LANES), F32),
                        pltpu.VMEM((BATCH, S5_ROWS, LANES), F32),
                        pltpu.VMEM((BATCH, SUBLANES, MIXER_WIDTH), F32),
                        pltpu.VMEM((T_SCAN * LRU_PITCH, LANES), F32),
                        pltpu.VMEM((T_SCAN * LRU_PITCH, LANES), F32),
                        pltpu.VMEM((LRU_SLOTS, LANES), F32),
                        pltpu.VMEM((BATCH, 1, LANES), F32),
                        pltpu.VMEM((MIXER_WIDTH, 2 * S5_NSTATE), BF16),
                        pltpu.VMEM((S5_ROWS, LANES), F32)],
        compiler_params=_params(1, VMEM_LIMIT_LARGE),
        name="recurrent_mixers",
    )(zmix3, zf3, qkv3, vecs, s5_row, s5_tile, b_re, b_im, cbd, gw, cw, wa, wx, tril, pq, pk, ones_q, ones_k)


def _attn_kernel(q_ref, qadd_ref, k_ref, kadd_ref, vt_ref, w1_ref, w2_ref, o_ref, w1o_ref, w2o_ref,
                 s_ref, m_ref, acc_ref):
    i = pl.program_id(2)
    scale = FOX_HEAD_DIM ** -0.5 * LOG2E
    neg = jnp.finfo(F32).min
    slots = [slice(n * HEAD_PAD, (n + 1) * HEAD_PAD) for n in range(FOX_HEADS)]
    qs = [(q_ref[:, sl].astype(F32) * scale + qadd_ref[:, sl].astype(F32)).astype(BF16) for sl in slots]

    def logits_t(n, j):
        start = pl.multiple_of(j * TK, TK)
        ks = k_ref[pl.ds(start, TK), slots[n]] + kadd_ref[pl.ds(start, TK), slots[n]]
        return lax.dot_general(ks, qs[n], (((1,), (1,)), ((), ())), preferred_element_type=F32)

    def block(j, masked, has_next):
        start = pl.multiple_of(j * TK, TK)
        for n in range(FOX_HEADS):
            s = s_ref[n]
            if n + 1 < FOX_HEADS:
                s_ref[n + 1] = logits_t(n + 1, j)
            elif has_next:
                s_ref[0] = logits_t(0, j + 1)
            if masked:
                key = lax.broadcasted_iota(jnp.int32, (TK, TQ), 0)
                qry = lax.broadcasted_iota(jnp.int32, (TK, TQ), 1)
                s = jnp.where(key <= qry, s, neg)
            m = m_ref[n]
            m_new = jnp.maximum(m, jnp.max(s, axis=0, keepdims=True))
            alpha = jnp.exp2(m - m_new)
            p = jnp.exp2(s - jnp.concatenate([m_new] * (TK // SUBLANES), axis=0))
            m_ref[n] = m_new
            vt = vt_ref[0, n * HEAD_PAD:n * HEAD_PAD + ACC_ROWS, pl.ds(start, TK)]
            acc_ref[n] = (jnp.concatenate([alpha] * (ACC_ROWS // SUBLANES), axis=0) * acc_ref[n]
                          + jnp.dot(vt, p.astype(BF16), preferred_element_type=F32))

    s_ref[0] = logits_t(0, 0)
    m_ref[...] = jnp.full(m_ref.shape, neg, F32)
    acc_ref[...] = jnp.zeros_like(acc_ref)

    @pl.loop(0, i // 2)
    def _(jj):
        block(2 * jj, False, True)
        block(2 * jj + 1, False, True)

    @pl.when(i % 2 == 1)
    def _():
        block(i - 1, False, True)

    block(i, True, False)
    for n, sl in enumerate(slots):
        acc = acc_ref[n]
        out_t = acc[0:FOX_HEAD_DIM] / acc[FOX_HEAD_DIM:FOX_HEAD_DIM + 1]
        out_t = jnp.concatenate([out_t, jnp.zeros((HEAD_PAD - FOX_HEAD_DIM, TQ), F32)], axis=0)
        o_ref[:, sl] = out_t.T.astype(MIX_OUT)
    w1o_ref[...] = w1_ref[...].astype(BF16)
    w2o_ref[...] = w2_ref[...].astype(BF16)


def _attn(qkv, qadd, kadd, vt, w1, w2, layer):
    n_q = SEQ // TQ
    n_steps = BATCH * n_q
    width = FOX_HEADS * HEAD_PAD
    qrow = lambda b, h, i: (b * n_q + i, 0)
    slab = lambda b, h, i: (b * n_q + i, 0)
    w_slab = lambda b, h, i: (layer, b * n_q + i, 0)
    return pl.pallas_call(
        _attn_kernel,
        grid=(BATCH, 1, n_q),
        in_specs=[pl.BlockSpec((TQ, width), qrow),
                  pl.BlockSpec((TQ, width), qrow),
                  pl.BlockSpec((SEQ, width), lambda b, h, i: (b, 1)),
                  pl.BlockSpec((SEQ, width), lambda b, h, i: (b, 0)),
                  pl.BlockSpec((1, width, SEQ), lambda b, h, i: (b, 0, 0)),
                  pl.BlockSpec((None, D_MODEL // n_steps, D_FF), w_slab),
                  pl.BlockSpec((None, D_FF // n_steps, D_MODEL), w_slab)],
        out_specs=[pl.BlockSpec((TQ, width), qrow),
                   pl.BlockSpec((D_MODEL // n_steps, D_FF), slab),
                   pl.BlockSpec((D_FF // n_steps, D_MODEL), slab)],
        out_shape=[jax.ShapeDtypeStruct((N_TOK, width), MIX_OUT),
                   jax.ShapeDtypeStruct((D_MODEL, D_FF), BF16),
                   jax.ShapeDtypeStruct((D_FF, D_MODEL), BF16)],
        scratch_shapes=[pltpu.VMEM((FOX_HEADS, TK, TQ), F32),
                        pltpu.VMEM((FOX_HEADS, SUBLANES, TQ), F32),
                        pltpu.VMEM((FOX_HEADS, ACC_ROWS, TQ), F32)],
        compiler_params=_params(3),
        name="fox_attention",
    )(qkv, qadd, qkv, kadd, vt, w1, w2)


def _merge_mlp_kernel(ya_ref, yb_ref, yc_ref, yd_ref, x_ref, pv_ref, wo_ref, w1_ref, w2_ref,
                      fg_ref, o_ref, h_ref, *, final_norm):
    j = pl.program_id(1)
    n_abc = 3 * MIXER_WIDTH
    n_d = FOX_HEADS * HEAD_PAD
    gm_ref = pv_ref.at[:, 0:n_abc]
    gmd_ref = pv_ref.at[:, n_abc:n_abc + n_d]
    g2_ref = pv_ref.at[:, n_abc + n_d:n_abc + n_d + D_MODEL]

    @pl.when(j == 0)
    def _():
        w = MIXER_WIDTH
        parts = [_rms(ya_ref[...].astype(F32), gm_ref[:, 0:w], w),
                 _rms(yb_ref[...].astype(F32), gm_ref[:, w:2 * w], w),
                 _rms(yc_ref[...].astype(F32), gm_ref[:, 2 * w:3 * w], w),
                 _rms(yd_ref[...].astype(F32), gmd_ref[...], w)]
        y = jnp.concatenate(parts, axis=1).astype(BF16)
        x1 = x_ref[...] + jnp.dot(y, wo_ref[...], preferred_element_type=F32)
        o_ref[...] = x1
        h_ref[...] = _rms(x1, g2_ref[...], D_MODEL).astype(BF16)

    a = jnp.dot(h_ref[...], w1_ref[...], preferred_element_type=F32)
    a = jnp.square(jnp.maximum(a, 0.0)).astype(BF16)
    o_ref[...] += jnp.dot(a, w2_ref[...], preferred_element_type=F32)

    if final_norm:
        @pl.when(j == pl.num_programs(1) - 1)
        def _():
            o_ref[...] = _rms(o_ref[...], fg_ref[...], D_MODEL)


def _merge_mlp(ya, yb, yc, yd, x, vecs, wo, w1, w2, fg, layer, final_norm):
    row = lambda i, j: (i, 0)
    k_dim = 3 * MIXER_WIDTH + FOX_HEADS * HEAD_PAD
    mix = pl.BlockSpec((TM_MLP, MIXER_WIDTH), row)
    return pl.pallas_call(
        functools.partial(_merge_mlp_kernel, final_norm=final_norm),
        grid=(N_TOK // TM_MLP, D_FF // TF_MLP),
        in_specs=[mix, mix, mix,
                  pl.BlockSpec((TM_MLP, FOX_HEADS * HEAD_PAD), row),
                  pl.BlockSpec((TM_MLP, D_MODEL), row),
                  _layer_block((1, k_dim + D_MODEL), layer),
                  _layer_block((k_dim, D_MODEL), layer),
                  pl.BlockSpec((D_MODEL, TF_MLP), lambda i, j: (0, j)),
                  pl.BlockSpec((TF_MLP, D_MODEL), lambda i, j: (j, 0)),
                  pl.BlockSpec((1, D_MODEL), lambda i, j: (0, 0))],
        out_specs=pl.BlockSpec((TM_MLP, D_MODEL), row),
        out_shape=jax.ShapeDtypeStruct((N_TOK, D_MODEL), F32),
        scratch_shapes=[pltpu.VMEM((TM_MLP, D_MODEL), BF16)],
        compiler_params=_params(2, VMEM_LIMIT_LARGE),
        name="merge_mlp",
    )(ya, yb, yc, yd, x, vecs, wo, w1, w2, fg)


def _pad_heads(w):
    lead = w.shape[:-1]
    w = w.reshape(*lead, FOX_HEADS, FOX_HEAD_DIM)
    w = jnp.pad(w, [(0, 0)] * (len(lead) + 1) + [(0, HEAD_PAD - FOX_HEAD_DIM)])
    return w.reshape(*lead, FOX_HEADS * HEAD_PAD)


def _block_diag(blocks):
    n, g, r, c = blocks.shape
    tiled = jnp.tile(blocks.reshape(n, g * r, c), (1, 1, g))
    on_diag = (np.arange(g * r)[:, None] // r) == (np.arange(g * c)[None, :] // c)
    return jnp.where(jnp.asarray(on_diag), tiled, 0.0)


def kernel(x, norm1_g, w_in, sgu_norm_g, sgu_w, sgu_b, s5_lambda_re, s5_lambda_im, s5_log_dt, s5_b_re, s5_b_im, s5_c_re, s5_c_im, s5_d, s5_glu_w, s5_glu_b, lru_conv_w, lru_conv_b, lru_wa, lru_ba, lru_wx, lru_bx, lru_lambda, fox_fgate_b, mix_norm_g, w_out, norm2_g, w_mlp_in, w_mlp_out, final_g):
    w = MIXER_WIDTH
    row = lambda v: v.reshape(DEPTH, 1, -1)

    vec_in = row(jnp.concatenate([norm1_g, sgu_norm_g], axis=-1))
    vec_mix = row(jnp.concatenate(
        [s5_d, s5_glu_b, lru_conv_b, lru_ba.reshape(DEPTH, w), lru_bx.reshape(DEPTH, w), lru_lambda,
         jnp.pad(fox_fgate_b, ((0, 0), (0, LANES - FOX_HEADS)))], axis=-1))
    vec_mlp = row(jnp.concatenate(
        [mix_norm_g[:, 0:3 * w], _pad_heads(mix_norm_g[:, 3 * w:]), norm2_g], axis=-1))

    sgu_wcat = jnp.transpose(sgu_w, (0, 2, 1, 3)).reshape(DEPTH, SGU_CHUNK, SGU_HEADS * SGU_CHUNK)
    sgu_bias = jnp.repeat(jnp.transpose(sgu_b, (0, 2, 1)), w // SGU_HEADS, axis=2)

    s5_par = jnp.stack([s5_lambda_re.reshape(DEPTH, S5_NSTATE), s5_lambda_im.reshape(DEPTH, S5_NSTATE),
                        jnp.repeat(s5_log_dt, S5_STATE, axis=-1)], axis=1)
    s5_tile = s5_par.reshape(DEPTH, 3 * SUBLANES, LANES)
    swap = lambda t: jnp.transpose(t, (0, 1, 3, 2))
    b_re_bd = _block_diag(swap(s5_b_re))
    b_im_bd = _block_diag(swap(s5_b_im))
    cbd = jnp.concatenate([_block_diag(swap(s5_c_re)), -_block_diag(swap(s5_c_im))], axis=1).astype(BF16)
    glu_w = s5_glu_w.astype(BF16)

    wa_bd = _block_diag(lru_wa).astype(BF16)
    wx_bd = _block_diag(lru_wx).astype(BF16)

    w_o_pad = jnp.concatenate(
        [w_out[:, 0:3 * w],
         jnp.pad(w_out[:, 3 * w:].reshape(DEPTH, FOX_HEADS, FOX_HEAD_DIM, D_MODEL),
                 ((0, 0), (0, 0), (0, HEAD_PAD - FOX_HEAD_DIM), (0, 0))).reshape(DEPTH, FOX_HEADS * HEAD_PAD, D_MODEL)],
        axis=1).astype(BF16)
    fg = final_g.reshape(1, D_MODEL)

    xf = x.reshape(N_TOK, D_MODEL)
    for l in range(DEPTH):
        y_a, zmix, qkv, zf = _in_proj(xf, vec_in, w_in, sgu_wcat, sgu_bias, l)
        zmix3 = zmix.reshape(BATCH, SEQ, ZMIX_COLS)
        y_b, y_c, qadd, kadd, vt = _mixers(
            zmix3, zf.reshape(BATCH, SEQ, LANES), qkv.reshape(BATCH, SEQ, QKV_COLS), vec_mix,
            (s5_par, s5_tile, b_re_bd, b_im_bd, cbd, glu_w), (lru_conv_w, wa_bd, wx_bd), l)
        qadd = qadd.reshape(N_TOK, FOX_HEADS * HEAD_PAD)
        kadd = kadd.reshape(N_TOK, FOX_HEADS * HEAD_PAD)
        y_d, w1, w2 = _attn(qkv, qadd, kadd, vt, w_mlp_in, w_mlp_out, l)
        xf = _merge_mlp(y_a, y_b.reshape(N_TOK, w), y_c.reshape(N_TOK, w), y_d, xf, vec_mlp, w_o_pad,
                        w1, w2, fg, l, final_norm=(l == DEPTH - 1))
    return xf.reshape(BATCH, SEQ, D_MODEL)
```

```python
import functools

import jax
import jax.numpy as jnp
import numpy as np
from jax import lax
from jax.experimental import pallas as pl
from jax.experimental.pallas import tpu as pltpu

D_MODEL = 1024
BATCH = 4
SEQ = 4096
DEPTH = 4
N_TOK = BATCH * SEQ
MIXER_WIDTH = 256
SGU_HEADS = 4
SGU_CHUNK = 128
S5_GROUPS = 16
S5_STATE = 64
LRU_CONV = 4
LRU_C = 8.0
FOX_HEADS = 4
FOX_HEAD_DIM = 64
D_FF = 4 * D_MODEL
RMS_EPS = 1e-6
LOG2E = 1.4426950408889634

LANES = 128
SUBLANES = 8
HEAD_PAD = LANES
ZMIX_COLS = 3 * MIXER_WIDTH
QKV_COLS = 3 * FOX_HEADS * HEAD_PAD
W_IN_RAW = 8 * MIXER_WIDTH + FOX_HEADS
S5_NSTATE = S5_GROUPS * S5_STATE
S5_ROWS = 2 * S5_NSTATE // LANES
SCAN_ROW_GAP = 4
S5_PITCH = S5_ROWS + SCAN_ROW_GAP

TM_IN = 1024
TM_MLP = 1024
TF_MLP = 1024
T_SCAN = 256
TQ = 512
TK = 512
ACC_ROWS = 80
MIB = 1024 * 1024
VMEM_LIMIT = 48 * MIB
VMEM_LIMIT_LARGE = 56 * MIB

F32 = jnp.float32
BF16 = jnp.bfloat16
MIX_OUT = BF16


def _params(n_axes, vmem_limit=VMEM_LIMIT):
    return pltpu.CompilerParams(dimension_semantics=("arbitrary",) * n_axes,
                                vmem_limit_bytes=vmem_limit)


def _layer_block(shape, layer):
    zeros = (0,) * len(shape)
    return pl.BlockSpec((None,) + tuple(shape), lambda *_: (layer,) + zeros)


def _rms(x, g, width):
    ms = jnp.sum(jnp.square(x), axis=-1, keepdims=True) * (1.0 / width)
    return x * lax.rsqrt(ms + RMS_EPS) * g


def _softplus(x):
    return jnp.maximum(x, 0.0) + jnp.log1p(jnp.exp(-jnp.abs(x)))


def _sgu_mix(zu, zv, g, w, bias, o_ref):
    n_rows = zu.shape[0]
    u = jax.nn.gelu(zu)
    v = _rms(jax.nn.gelu(zv), g, MIXER_WIDTH)
    hd = MIXER_WIDTH // SGU_HEADS
    lane_head = lax.broadcasted_iota(jnp.int32, (SGU_CHUNK, MIXER_WIDTH), 1) // hd
    t_idx = lax.broadcasted_iota(jnp.int32, (SGU_CHUNK, SGU_HEADS * SGU_CHUNK), 0)
    s_idx = lax.broadcasted_iota(jnp.int32, (SGU_CHUNK, SGU_HEADS * SGU_CHUNK), 1) % SGU_CHUNK
    wm = jnp.where(s_idx <= t_idx, w, 0.0).astype(BF16)
    for c in range(n_rows // SGU_CHUNK):
        rows = slice(c * SGU_CHUNK, (c + 1) * SGU_CHUNK)
        vc = v[rows]
        vstack = jnp.concatenate(
            [jnp.where(lane_head == h, vc, 0.0) for h in range(SGU_HEADS)], axis=0).astype(BF16)
        mixed = jnp.dot(wm, vstack, preferred_element_type=F32) + bias
        o_ref[rows, :] = (u[rows] * mixed).astype(MIX_OUT)


def _in_proj_kernel(x_ref, pv_ref, w_ref, sw_ref, sb_ref, ya_ref, zmix_ref, qkv_ref, zf_ref, wb_ref):
    w = MIXER_WIDTH

    @pl.when(pl.program_id(0) == 0)
    def _():
        wb_ref[:, 0:8 * w] = w_ref[:, 0:8 * w].astype(BF16)
        forget = jnp.concatenate([w_ref[:, 8 * w:W_IN_RAW], jnp.zeros((D_MODEL, LANES - FOX_HEADS), F32)], axis=1)
        wb_ref[:, 8 * w:8 * w + LANES] = forget.astype(BF16)

    h = _rms(x_ref[...], pv_ref[:, 0:D_MODEL], D_MODEL).astype(BF16)
    z5 = jnp.dot(h, wb_ref[:, 0:5 * w], preferred_element_type=F32)
    zmix_ref[...] = z5[:, 2 * w:5 * w]
    _sgu_mix(z5[:, 0:w], z5[:, w:2 * w], pv_ref[:, D_MODEL:D_MODEL + w], sw_ref[...], sb_ref[...], ya_ref)
    zqkvf = jnp.dot(h, wb_ref[:, 5 * w:8 * w + LANES], preferred_element_type=F32)
    zf_ref[...] = zqkvf[:, 3 * w:]
    pad = jnp.zeros((TM_IN, HEAD_PAD - FOX_HEAD_DIM), F32)
    for slot in range(3 * FOX_HEADS):
        head = zqkvf[:, slot * FOX_HEAD_DIM:(slot + 1) * FOX_HEAD_DIM]
        qkv_ref[:, slot * HEAD_PAD:(slot + 1) * HEAD_PAD] = jnp.concatenate([head, pad], axis=1).astype(BF16)


def _in_proj(x, vecs, w, sgu_w, sgu_b, layer):
    row = lambda i: (i, 0)
    return pl.pallas_call(
        _in_proj_kernel,
        grid=(N_TOK // TM_IN,),
        in_specs=[pl.BlockSpec((TM_IN, D_MODEL), row),
                  _layer_block((1, D_MODEL + MIXER_WIDTH), layer),
                  _layer_block((D_MODEL, W_IN_RAW), layer),
                  _layer_block((SGU_CHUNK, SGU_HEADS * SGU_CHUNK), layer),
                  _layer_block((SGU_CHUNK, MIXER_WIDTH), layer)],
        out_specs=[pl.BlockSpec((TM_IN, MIXER_WIDTH), row),
                   pl.BlockSpec((TM_IN, ZMIX_COLS), row),
                   pl.BlockSpec((TM_IN, QKV_COLS), row),
                   pl.BlockSpec((TM_IN, LANES), row)],
        out_shape=[jax.ShapeDtypeStruct((N_TOK, MIXER_WIDTH), MIX_OUT),
                   jax.ShapeDtypeStruct((N_TOK, ZMIX_COLS), F32),
                   jax.ShapeDtypeStruct((N_TOK, QKV_COLS), BF16),
                   jax.ShapeDtypeStruct((N_TOK, LANES), F32)],
        scratch_shapes=[pltpu.VMEM((D_MODEL, 8 * MIXER_WIDTH + LANES), BF16)],
        compiler_params=_params(1, VMEM_LIMIT_LARGE),
        name="in_proj_sgu",
    )(x, vecs, w, sgu_w, sgu_b)


def _split3(x):
    hi = x.astype(BF16)
    r1 = x - hi.astype(F32)
    mid = r1.astype(BF16)
    lo = (r1 - mid.astype(F32)).astype(BF16)
    return hi, mid, lo


def _bias_placement():
    pq = np.zeros((LANES, FOX_HEADS * HEAD_PAD), np.float32)
    pk = np.zeros_like(pq)
    ones_q = np.zeros((1, FOX_HEADS * HEAD_PAD), np.float32)
    ones_k = np.zeros_like(ones_q)
    for h in range(FOX_HEADS):
        for piece in range(3):
            pq[piece * FOX_HEADS + h, h * HEAD_PAD + FOX_HEAD_DIM + piece] = 1.0
            pk[piece * FOX_HEADS + h, h * HEAD_PAD + FOX_HEAD_DIM + 3 + piece] = -1.0
            ones_q[0, h * HEAD_PAD + FOX_HEAD_DIM + 3 + piece] = 1.0
            ones_k[0, h * HEAD_PAD + FOX_HEAD_DIM + piece] = 1.0
    return (jnp.asarray(pq, BF16), jnp.asarray(pk, BF16), jnp.asarray(ones_q), jnp.asarray(ones_k))


def _fcum_chunk(zf_ref, bf_ref, v_ref, tril_ref, pq_ref, pk_ref, oq_ref, ok_ref, qadd_ref, kadd_ref, vt_ref,
                carry_ref, batches):
    tril = tril_ref[...]
    lane = lax.broadcasted_iota(jnp.int32, (T_SCAN, LANES), 1)
    ones_row = lax.broadcasted_iota(jnp.int32, (HEAD_PAD, T_SCAN), 0) == FOX_HEAD_DIM
    for b in batches:
        for n in range(FOX_HEADS):
            v_t = v_ref[b, :, n * HEAD_PAD:(n + 1) * HEAD_PAD].astype(F32).T
            vt_ref[b, n * HEAD_PAD:(n + 1) * HEAD_PAD, :] = jnp.where(ones_row, 1.0, v_t).astype(BF16)
        logit = zf_ref[b] + bf_ref[...]
        log_f = -_softplus(-logit)
        hi, mid, lo = _split3(log_f)
        cs = (jnp.dot(tril, hi, preferred_element_type=F32)
              + jnp.dot(tril, mid, preferred_element_type=F32)
              + jnp.dot(tril, lo, preferred_element_type=F32))
        cum = cs + carry_ref[b]
        carry_ref[b] = cum[T_SCAN - 1:T_SCAN, :]
        hi, mid, lo = [p.astype(F32) for p in _split3(cum * LOG2E)]
        pieces = jnp.where(lane < FOX_HEADS, hi,
                           jnp.where(lane < 2 * FOX_HEADS, pltpu.roll(mid, FOX_HEADS, axis=1),
                                     jnp.where(lane < 3 * FOX_HEADS, pltpu.roll(lo, 2 * FOX_HEADS, axis=1), 0.0))
                           ).astype(BF16)
        qadd_ref[b] = (jnp.dot(pieces, pq_ref[...], preferred_element_type=F32) + oq_ref[...]).astype(BF16)
        kadd_ref[b] = (jnp.dot(pieces, pk_ref[...], preferred_element_type=F32) + ok_ref[...]).astype(BF16)


def _s5_load(z_ref, bbd_ref, s_ref, batches):
    for b in batches:
        u = z_ref[b, :, 0:MIXER_WIDTH]
        bu = jnp.dot(u.astype(BF16), bbd_ref[...], preferred_element_type=F32)
        for j in range(S5_ROWS):
            s_ref[b, pl.ds(j, T_SCAN, stride=S5_PITCH), :] = bu[:, j * LANES:(j + 1) * LANES]


def _s5_step(t, carry, lam_re, lam_im, s_ref):
    base = t * S5_PITCH
    new = []
    for b in range(BATCH):
        h_re, h_im = carry[2 * b], carry[2 * b + 1]
        n_re = lam_re * h_re - lam_im * h_im + s_ref[b, pl.ds(base, SUBLANES), :]
        n_im = lam_re * h_im + lam_im * h_re + s_ref[b, pl.ds(base + SUBLANES, SUBLANES), :]
        s_ref[b, pl.ds(base, SUBLANES), :] = n_re
        s_ref[b, pl.ds(base + SUBLANES, SUBLANES), :] = n_im
        new += [n_re, n_im]
    return new


def _s5_out(z_ref, cbd_ref, d_ref, gw_ref, gb_ref, o_ref, s_ref):
    states = jnp.concatenate(
        [jnp.concatenate([s_ref[b, pl.ds(j, T_SCAN, stride=S5_PITCH), :].astype(BF16) for j in range(S5_ROWS)],
                         axis=1) for b in range(BATCH)], axis=0)
    u = z_ref[:, :, 0:MIXER_WIDTH].reshape(BATCH * T_SCAN, MIXER_WIDTH)
    y = jnp.dot(states, cbd_ref[...], preferred_element_type=F32) + d_ref[...] * u
    y = jax.nn.gelu(y)
    gate = jnp.dot(y.astype(BF16), gw_ref[...], preferred_element_type=F32) + gb_ref[...]
    o_ref[...] = (y * jax.nn.sigmoid(gate)).astype(MIX_OUT).reshape(BATCH, T_SCAN, MIXER_WIDTH)


LRU_SLOTS = BATCH * MIXER_WIDTH // LANES
LRU_PITCH = LRU_SLOTS + SCAN_ROW_GAP
LRU_TILES = MIXER_WIDTH // LANES


def _lru_load(z_ref, cw_ref, cb_ref, wa_ref, ba_ref, wx_ref, bx_ref, lam_ref, tail_ref, a_ref, b_ref, batches):
    decay_rate = LRU_C * _softplus(-lam_ref[...])
    for b in batches:
        x = z_ref[b, :, MIXER_WIDTH:2 * MIXER_WIDTH]
        xp = jnp.concatenate([tail_ref[b], x], axis=0)
        tail_ref[b] = x[T_SCAN - SUBLANES:T_SCAN, :]
        xc = cb_ref[...]
        for k in range(LRU_CONV):
            off = SUBLANES - (LRU_CONV - 1) + k
            xc = xc + cw_ref[k:k + 1, :] * xp[off:off + T_SCAN, :]
        xcb = xc.astype(BF16)
        r = jax.nn.sigmoid(jnp.dot(xcb, wa_ref[...], preferred_element_type=F32) + ba_ref[...])
        i = jax.nn.sigmoid(jnp.dot(xcb, wx_ref[...], preferred_element_type=F32) + bx_ref[...])
        log_a = -(r * decay_rate)
        a = jnp.exp(log_a)
        inp = jnp.sqrt(-jnp.tanh(log_a) * (a * a + 1.0)) * (i * xc)
        for j in range(LRU_TILES):
            slot = b * LRU_TILES + j
            a_ref[pl.ds(slot, T_SCAN, stride=LRU_PITCH), :] = a[:, j * LANES:(j + 1) * LANES]
            b_ref[pl.ds(slot, T_SCAN, stride=LRU_PITCH), :] = inp[:, j * LANES:(j + 1) * LANES]


def _lru_step(t, h, a_ref, b_ref):
    base = t * LRU_PITCH
    h = a_ref[pl.ds(base, LRU_SLOTS), :] * h + b_ref[pl.ds(base, LRU_SLOTS), :]
    b_ref[pl.ds(base, LRU_SLOTS), :] = h
    return h


def _lru_out(z_ref, o_ref, b_ref):
    for b in range(BATCH):
        h = jnp.concatenate(
            [b_ref[pl.ds(b * LRU_TILES + j, T_SCAN, stride=LRU_PITCH), :] for j in range(LRU_TILES)], axis=1)
        o_ref[b] = (h * jax.nn.gelu(z_ref[b, :, 2 * MIXER_WIDTH:3 * MIXER_WIDTH])).astype(MIX_OUT)


def _s5_zoh(lam_re, lam_im, log_dt):
    dt = jnp.exp(log_dt)
    mag = jnp.exp(lam_re * dt)
    abar_re = mag * jnp.cos(lam_im * dt)
    abar_im = mag * jnp.sin(lam_im * dt)
    denom = jnp.square(lam_re) + jnp.square(lam_im)
    num_re = abar_re - 1.0
    num_im = abar_im
    fac_re = (num_re * lam_re + num_im * lam_im) / denom
    fac_im = (num_im * lam_re - num_re * lam_im) / denom
    return abar_re, abar_im, fac_re, fac_im


def _mixers_kernel(z_ref, zf_ref, v_ref, pv_ref,
                   s5row_ref, s5tile_ref, bre_ref, bim_ref, cbd_ref, gw_ref, cw_ref, wa_ref, wx_ref,
                   tril_ref, pq_ref, pk_ref, oq_ref, ok_ref,
                   yb_ref, yc_ref, qadd_ref, kadd_ref, vt_ref,
                   s_ref, h5_ref, tail_ref, a_ref, b_ref, hl_ref, carry_ref, bbd_ref, lam5_ref):
    w = MIXER_WIDTH
    d_ref, gb_ref, cb_ref, ba_ref, bx_ref, lam_ref = [pv_ref.at[:, k * w:(k + 1) * w] for k in range(6)]
    bf_ref = pv_ref.at[:, 6 * w:6 * w + LANES]

    @pl.when(pl.program_id(0) == 0)
    def _():
        h5_ref[...] = jnp.zeros_like(h5_ref)
        tail_ref[...] = jnp.zeros_like(tail_ref)
        hl_ref[...] = jnp.zeros_like(hl_ref)
        carry_ref[...] = jnp.zeros_like(carry_ref)
        t = SUBLANES
        abar_re, abar_im, _, _ = _s5_zoh(s5tile_ref[0:t, :], s5tile_ref[t:2 * t, :], s5tile_ref[2 * t:3 * t, :])
        lam5_ref[0:t, :] = abar_re
        lam5_ref[t:2 * t, :] = abar_im
        _, _, fac_re, fac_im = _s5_zoh(s5row_ref[0:1, :], s5row_ref[1:2, :], s5row_ref[2:3, :])
        b_re, b_im = bre_ref[...], bim_ref[...]
        bbd_ref[:, 0:S5_NSTATE] = (fac_re * b_re - fac_im * b_im).astype(BF16)
        bbd_ref[:, S5_NSTATE:2 * S5_NSTATE] = (fac_re * b_im + fac_im * b_re).astype(BF16)

    for b in range(BATCH):
        _s5_load(z_ref, bbd_ref, s_ref, (b,))
        _lru_load(z_ref, cw_ref, cb_ref, wa_ref, ba_ref, wx_ref, bx_ref, lam_ref, tail_ref, a_ref, b_ref, (b,))
        _fcum_chunk(zf_ref, bf_ref, v_ref, tril_ref, pq_ref, pk_ref, oq_ref, ok_ref, qadd_ref, kadd_ref, vt_ref,
                    carry_ref, (b,))

    lam_re = lam5_ref[0:SUBLANES, :]
    lam_im = lam5_ref[SUBLANES:S5_ROWS, :]

    def step(t, carry):
        new5 = _s5_step(t, carry[:-1], lam_re, lam_im, s_ref)
        return tuple(new5) + (_lru_step(t, carry[-1], a_ref, b_ref),)

    init = []
    for b in range(BATCH):
        init += [h5_ref[b, 0:SUBLANES, :], h5_ref[b, SUBLANES:S5_ROWS, :]]
    fin = lax.fori_loop(0, T_SCAN, step, tuple(init) + (hl_ref[...],), unroll=8)
    for b in range(BATCH):
        h5_ref[b, 0:SUBLANES, :] = fin[2 * b]
        h5_ref[b, SUBLANES:S5_ROWS, :] = fin[2 * b + 1]
    hl_ref[...] = fin[-1]

    _s5_out(z_ref, cbd_ref, d_ref, gw_ref, gb_ref, yb_ref, s_ref)
    _lru_out(z_ref, yc_ref, b_ref)


MIXER_VEC_COLS = 6 * MIXER_WIDTH + LANES


def _mixers(zmix3, zf3, qkv3, vecs, s5_params, lru_params, layer):
    s5_row, s5_tile, b_re, b_im, cbd, gw = s5_params
    cw, wa, wx = lru_params
    pq, pk, ones_q, ones_k = _bias_placement()
    tril = jnp.asarray(np.tril(np.ones((T_SCAN, T_SCAN), np.float32)), BF16)
    full = lambda c: (0, 0)
    chunk = lambda c: (0, c, 0)
    wide = FOX_HEADS * HEAD_PAD
    mat = _layer_block((MIXER_WIDTH, MIXER_WIDTH), layer)
    mix_out = pl.BlockSpec((BATCH, T_SCAN, MIXER_WIDTH), chunk)
    bias_out = pl.BlockSpec((BATCH, T_SCAN, wide), chunk)
    return pl.pallas_call(
        _mixers_kernel,
        grid=(SEQ // T_SCAN,),
        in_specs=[pl.BlockSpec((BATCH, T_SCAN, ZMIX_COLS), chunk),
                  pl.BlockSpec((BATCH, T_SCAN, LANES), chunk),
                  pl.BlockSpec((BATCH, T_SCAN, wide), lambda c: (0, c, 2)),
                  _layer_block((1, MIXER_VEC_COLS), layer),
                  _layer_block((3, S5_NSTATE), layer),
                  _layer_block((3 * SUBLANES, LANES), layer),
                  _layer_block((MIXER_WIDTH, S5_NSTATE), layer),
                  _layer_block((MIXER_WIDTH, S5_NSTATE), layer),
                  _layer_block((2 * S5_NSTATE, MIXER_WIDTH), layer),
                  mat,
                  _layer_block((LRU_CONV, MIXER_WIDTH), layer), mat, mat,
                  pl.BlockSpec((T_SCAN, T_SCAN), full),
                  pl.BlockSpec((LANES, wide), full),
                  pl.BlockSpec((LANES, wide), full),
                  pl.BlockSpec((1, wide), full),
                  pl.BlockSpec((1, wide), full)],
        out_specs=[mix_out, mix_out, bias_out, bias_out,
                   pl.BlockSpec((BATCH, wide, T_SCAN), lambda c: (0, 0, c))],
        out_shape=[jax.ShapeDtypeStruct((BATCH, SEQ, MIXER_WIDTH), MIX_OUT),
                   jax.ShapeDtypeStruct((BATCH, SEQ, MIXER_WIDTH), MIX_OUT),
                   jax.ShapeDtypeStruct((BATCH, SEQ, wide), BF16),
                   jax.ShapeDtypeStruct((BATCH, SEQ, wide), BF16),
                   jax.ShapeDtypeStruct((BATCH, wide, SEQ), BF16)],
        scratch_shapes=[pltpu.VMEM((BATCH, T_SCAN * S5_PITCH, LANES), F32),
                        pltpu.VMEM((BATCH, S5_ROWS, LANES), F32),
                        pltpu.VMEM((BATCH, SUBLANES, MIXER_WIDTH), F32),
                        pltpu.VMEM((T_SCAN * LRU_PITCH, LANES), F32),
                        pltpu.VMEM((T_SCAN * LRU_PITCH, LANES), F32),
                        pltpu.VMEM((LRU_SLOTS, LANES), F32),
                        pltpu.VMEM((BATCH, 1, LANES), F32),
                        pltpu.VMEM((MIXER_WIDTH, 2 * S5_NSTATE), BF16),
                        pltpu.VMEM((S5_ROWS, LANES), F32)],
        compiler_params=_params(1, VMEM_LIMIT_LARGE),
        name="recurrent_mixers",
    )(zmix3, zf3, qkv3, vecs, s5_row, s5_tile, b_re, b_im, cbd, gw, cw, wa, wx, tril, pq, pk, ones_q, ones_k)


def _attn_kernel(q_ref, qadd_ref, k_ref, kadd_ref, vt_ref, w1_ref, w2_ref, o_ref, w1o_ref, w2o_ref,
                 s_ref, m_ref, acc_ref):
    i = pl.program_id(2)
    scale = FOX_HEAD_DIM ** -0.5 * LOG2E
    neg = jnp.finfo(F32).min
    slots = [slice(n * HEAD_PAD, (n + 1) * HEAD_PAD) for n in range(FOX_HEADS)]
    qs = [(q_ref[:, sl].astype(F32) * scale + qadd_ref[:, sl].astype(F32)).astype(BF16) for sl in slots]

    def logits_t(n, j):
        start = pl.multiple_of(j * TK, TK)
        ks = k_ref[pl.ds(start, TK), slots[n]] + kadd_ref[pl.ds(start, TK), slots[n]]
        return lax.dot_general(ks, qs[n], (((1,), (1,)), ((), ())), preferred_element_type=F32)

    def block(j, masked, has_next):
        start = pl.multiple_of(j * TK, TK)
        for n in range(FOX_HEADS):
            s = s_ref[n]
            if n + 1 < FOX_HEADS:
                s_ref[n + 1] = logits_t(n + 1, j)
            elif has_next:
                s_ref[0] = logits_t(0, j + 1)
            if masked:
                key = lax.broadcasted_iota(jnp.int32, (TK, TQ), 0)
                qry = lax.broadcasted_iota(jnp.int32, (TK, TQ), 1)
                s = jnp.where(key <= qry, s, neg)
            m = m_ref[n]
            m_new = jnp.maximum(m, jnp.max(s, axis=0, keepdims=True))
            alpha = jnp.exp2(m - m_new)
            p = jnp.exp2(s - jnp.concatenate([m_new] * (TK // SUBLANES), axis=0))
            m_ref[n] = m_new
            vt = vt_ref[0, n * HEAD_PAD:n * HEAD_PAD + ACC_ROWS, pl.ds(start, TK)]
            acc_ref[n] = (jnp.concatenate([alpha] * (ACC_ROWS // SUBLANES), axis=0) * acc_ref[n]
                          + jnp.dot(vt, p.astype(BF16), preferred_element_type=F32))

    s_ref[0] = logits_t(0, 0)
    m_ref[...] = jnp.full(m_ref.shape, neg, F32)
    acc_ref[...] = jnp.zeros_like(acc_ref)

    @pl.loop(0, i // 2)
    def _(jj):
        block(2 * jj, False, True)
        block(2 * jj + 1, False, True)

    @pl.when(i % 2 == 1)
    def _():
        block(i - 1, False, True)

    block(i, True, False)
    for n, sl in enumerate(slots):
        acc = acc_ref[n]
        out_t = acc[0:FOX_HEAD_DIM] / acc[FOX_HEAD_DIM:FOX_HEAD_DIM + 1]
        out_t = jnp.concatenate([out_t, jnp.zeros((HEAD_PAD - FOX_HEAD_DIM, TQ), F32)], axis=0)
        o_ref[:, sl] = out_t.T.astype(MIX_OUT)
    w1o_ref[...] = w1_ref[...].astype(BF16)
    w2o_ref[...] = w2_ref[...].astype(BF16)


def _attn(qkv, qadd, kadd, vt, w1, w2, layer):
    n_q = SEQ // TQ
    n_steps = BATCH * n_q
    width = FOX_HEADS * HEAD_PAD
    qrow = lambda b, h, i: (b * n_q + i, 0)
    slab = lambda b, h, i: (b * n_q + i, 0)
    w_slab = lambda b, h, i: (layer, b * n_q + i, 0)
    return pl.pallas_call(
        _attn_kernel,
        grid=(BATCH, 1, n_q),
        in_specs=[pl.BlockSpec((TQ, width), qrow),
                  pl.BlockSpec((TQ, width), qrow),
                  pl.BlockSpec((SEQ, width), lambda b, h, i: (b, 1)),
                  pl.BlockSpec((SEQ, width), lambda b, h, i: (b, 0)),
                  pl.BlockSpec((1, width, SEQ), lambda b, h, i: (b, 0, 0)),
                  pl.BlockSpec((None, D_MODEL // n_steps, D_FF), w_slab),
                  pl.BlockSpec((None, D_FF // n_steps, D_MODEL), w_slab)],
        out_specs=[pl.BlockSpec((TQ, width), qrow),
                   pl.BlockSpec((D_MODEL // n_steps, D_FF), slab),
                   pl.BlockSpec((D_FF // n_steps, D_MODEL), slab)],
        out_shape=[jax.ShapeDtypeStruct((N_TOK, width), MIX_OUT),
                   jax.ShapeDtypeStruct((D_MODEL, D_FF), BF16),
                   jax.ShapeDtypeStruct((D_FF, D_MODEL), BF16)],
        scratch_shapes=[pltpu.VMEM((FOX_HEADS, TK, TQ), F32),
                        pltpu.VMEM((FOX_HEADS, SUBLANES, TQ), F32),
                        pltpu.VMEM((FOX_HEADS, ACC_ROWS, TQ), F32)],
        compiler_params=_params(3),
        name="fox_attention",
    )(qkv, qadd, qkv, kadd, vt, w1, w2)


def _merge_mlp_kernel(ya_ref, yb_ref, yc_ref, yd_ref, x_ref, pv_ref, wo_ref, w1_ref, w2_ref,
                      fg_ref, o_ref, h_ref, *, final_norm):
    n_abc = 3 * MIXER_WIDTH
    n_d = FOX_HEADS * HEAD_PAD
    gm_ref = pv_ref.at[:, 0:n_abc]
    gmd_ref = pv_ref.at[:, n_abc:n_abc + n_d]
    g2_ref = pv_ref.at[:, n_abc + n_d:n_abc + n_d + D_MODEL]

    w = MIXER_WIDTH
    parts = [_rms(ya_ref[...].astype(F32), gm_ref[:, 0:w], w),
             _rms(yb_ref[...].astype(F32), gm_ref[:, w:2 * w], w),
             _rms(yc_ref[...].astype(F32), gm_ref[:, 2 * w:3 * w], w),
             _rms(yd_ref[...].astype(F32), gmd_ref[...], w)]
    y = jnp.concatenate(parts, axis=1).astype(BF16)
    x1 = x_ref[...] + jnp.dot(y, wo_ref[...], preferred_element_type=F32)
    o_ref[...] = x1
    h_ref[...] = _rms(x1, g2_ref[...], D_MODEL).astype(BF16)

    for c in range(D_FF // TF_MLP):
        cols = slice(c * TF_MLP, (c + 1) * TF_MLP)
        a = jnp.dot(h_ref[...], w1_ref[:, cols], preferred_element_type=F32)
        a = jnp.square(jnp.maximum(a, 0.0)).astype(BF16)
        o_ref[...] += jnp.dot(a, w2_ref[cols, :], preferred_element_type=F32)

    if final_norm:
        o_ref[...] = _rms(o_ref[...], fg_ref[...], D_MODEL)


def _merge_mlp(ya, yb, yc, yd, x, vecs, wo, w1, w2, fg, layer, final_norm):
    row = lambda i: (i, 0)
    whole = lambda i: (0, 0)
    once = pl.Buffered(1)
    k_dim = 3 * MIXER_WIDTH + FOX_HEADS * HEAD_PAD
    mix = pl.BlockSpec((TM_MLP, MIXER_WIDTH), row)
    return pl.pallas_call(
        functools.partial(_merge_mlp_kernel, final_norm=final_norm),
        grid=(N_TOK // TM_MLP,),
        in_specs=[mix, mix, mix,
                  pl.BlockSpec((TM_MLP, FOX_HEADS * HEAD_PAD), row),
                  pl.BlockSpec((TM_MLP, D_MODEL), row),
                  _layer_block((1, k_dim + D_MODEL), layer),
                  pl.BlockSpec((None, k_dim, D_MODEL), lambda i: (layer, 0, 0), pipeline_mode=once),
                  pl.BlockSpec((D_MODEL, D_FF), whole, pipeline_mode=once),
                  pl.BlockSpec((D_FF, D_MODEL), whole, pipeline_mode=once),
                  pl.BlockSpec((1, D_MODEL), whole)],
        out_specs=pl.BlockSpec((TM_MLP, D_MODEL), row),
        out_shape=jax.ShapeDtypeStruct((N_TOK, D_MODEL), F32),
        scratch_shapes=[pltpu.VMEM((TM_MLP, D_MODEL), BF16)],
        compiler_params=_params(1, VMEM_LIMIT_LARGE),
        name="merge_mlp",
    )(ya, yb, yc, yd, x, vecs, wo, w1, w2, fg)


def _pad_heads(w):
    lead = w.shape[:-1]
    w = w.reshape(*lead, FOX_HEADS, FOX_HEAD_DIM)
    w = jnp.pad(w, [(0, 0)] * (len(lead) + 1) + [(0, HEAD_PAD - FOX_HEAD_DIM)])
    return w.reshape(*lead, FOX_HEADS * HEAD_PAD)


def _block_diag(blocks):
    n, g, r, c = blocks.shape
    tiled = jnp.tile(blocks.reshape(n, g * r, c), (1, 1, g))
    on_diag = (np.arange(g * r)[:, None] // r) == (np.arange(g * c)[None, :] // c)
    return jnp.where(jnp.asarray(on_diag), tiled, 0.0)


def kernel(x, norm1_g, w_in, sgu_norm_g, sgu_w, sgu_b, s5_lambda_re, s5_lambda_im, s5_log_dt, s5_b_re, s5_b_im, s5_c_re, s5_c_im, s5_d, s5_glu_w, s5_glu_b, lru_conv_w, lru_conv_b, lru_wa, lru_ba, lru_wx, lru_bx, lru_lambda, fox_fgate_b, mix_norm_g, w_out, norm2_g, w_mlp_in, w_mlp_out, final_g):
    w = MIXER_WIDTH
    row = lambda v: v.reshape(DEPTH, 1, -1)

    vec_in = row(jnp.concatenate([norm1_g, sgu_norm_g], axis=-1))
    vec_mix = row(jnp.concatenate(
        [s5_d, s5_glu_b, lru_conv_b, lru_ba.reshape(DEPTH, w), lru_bx.reshape(DEPTH, w), lru_lambda,
         jnp.pad(fox_fgate_b, ((0, 0), (0, LANES - FOX_HEADS)))], axis=-1))
    vec_mlp = row(jnp.concatenate(
        [mix_norm_g[:, 0:3 * w], _pad_heads(mix_norm_g[:, 3 * w:]), norm2_g], axis=-1))

    sgu_wcat = jnp.transpose(sgu_w, (0, 2, 1, 3)).reshape(DEPTH, SGU_CHUNK, SGU_HEADS * SGU_CHUNK)
    sgu_bias = jnp.repeat(jnp.transpose(sgu_b, (0, 2, 1)), w // SGU_HEADS, axis=2)

    s5_par = jnp.stack([s5_lambda_re.reshape(DEPTH, S5_NSTATE), s5_lambda_im.reshape(DEPTH, S5_NSTATE),
                        jnp.repeat(s5_log_dt, S5_STATE, axis=-1)], axis=1)
    s5_tile = s5_par.reshape(DEPTH, 3 * SUBLANES, LANES)
    swap = lambda t: jnp.transpose(t, (0, 1, 3, 2))
    b_re_bd = _block_diag(swap(s5_b_re))
    b_im_bd = _block_diag(swap(s5_b_im))
    cbd = jnp.concatenate([_block_diag(swap(s5_c_re)), -_block_diag(swap(s5_c_im))], axis=1).astype(BF16)
    glu_w = s5_glu_w.astype(BF16)

    wa_bd = _block_diag(lru_wa).astype(BF16)
    wx_bd = _block_diag(lru_wx).astype(BF16)

    w_o_pad = jnp.concatenate(
        [w_out[:, 0:3 * w],
         jnp.pad(w_out[:, 3 * w:].reshape(DEPTH, FOX_HEADS, FOX_HEAD_DIM, D_MODEL),
                 ((0, 0), (0, 0), (0, HEAD_PAD - FOX_HEAD_DIM), (0, 0))).reshape(DEPTH, FOX_HEADS * HEAD_PAD, D_MODEL)],
        axis=1).astype(BF16)
    fg = final_g.reshape(1, D_MODEL)

    xf = x.reshape(N_TOK, D_MODEL)
    for l in range(DEPTH):
        y_a, zmix, qkv, zf = _in_proj(xf, vec_in, w_in, sgu_wcat, sgu_bias, l)
        zmix3 = zmix.reshape(BATCH, SEQ, ZMIX_COLS)
        y_b, y_c, qadd, kadd, vt = _mixers(
            zmix3, zf.reshape(BATCH, SEQ, LANES), qkv.reshape(BATCH, SEQ, QKV_COLS), vec_mix,
            (s5_par, s5_tile, b_re_bd, b_im_bd, cbd, glu_w), (lru_conv_w, wa_bd, wx_bd), l)
        qadd = qadd.reshape(N_TOK, FOX_HEADS * HEAD_PAD)
        kadd = kadd.reshape(N_TOK, FOX_HEADS * HEAD_PAD)
        y_d, w1, w2 = _attn(qkv, qadd, kadd, vt, w_mlp_in, w_mlp_out, l)
        xf = _merge_mlp(y_a, y_b.reshape(N_TOK, w), y_c.reshape(N_TOK, w), y_d, xf, vec_mlp, w_o_pad,
                        w1, w2, fg, l, final_norm=(l == DEPTH - 1))
    return xf.reshape(BATCH, SEQ, D_MODEL)
```
